```python
import math
import jax
import jax.numpy as jnp
from jax import lax
import numpy as np

D_MODEL = 1024
BATCH = 32
SEQ = 2048
DEPTH = 2
DEC_BATCH = 32
DEC_SEQ = 16
PAST_LEN = 4096

CHUNK = 64
Q_BLOCK = 128
N_EVEN = (DEPTH + 1) // 2
N_ODD = DEPTH // 2
EPS = 1e-6

SB_HEADS = 8
SB_HEAD_DIM = 64
SB_WIDTH = SB_HEADS * SB_HEAD_DIM
SB_SCALE = 1.0 / math.sqrt(SB_HEAD_DIM)

CONV_DIM = D_MODEL // 2
CONV_W = 3

SGU_CHUNK = 128
SGU_GROUPS = 4
SGU_DIM = D_MODEL // 2
SGU_GROUP_DIM = SGU_DIM // SGU_GROUPS

MLA_HEADS = 8
Q_LORA = 384
KV_LORA = 256
NOPE_DIM = 64
ROPE_DIM = 32
V_DIM = 64
ROPE_THETA = 10000.0
MLA_SCALE = 1.0 / math.sqrt(NOPE_DIM + ROPE_DIM)

D_FF = 4 * D_MODEL

EVEN_SPLITS = (SB_WIDTH, 2 * SB_WIDTH, 3 * SB_WIDTH, 3 * SB_WIDTH + CONV_DIM, 3 * SB_WIDTH + 2 * CONV_DIM)
EVEN_IN = 3 * SB_WIDTH + 3 * CONV_DIM
EVEN_MIX = SB_WIDTH + CONV_DIM
ODD_SPLITS = (SGU_DIM, 2 * SGU_DIM, 2 * SGU_DIM + Q_LORA, 2 * SGU_DIM + Q_LORA + KV_LORA)
ODD_IN = 2 * SGU_DIM + Q_LORA + KV_LORA + ROPE_DIM
ODD_MIX = SGU_DIM + MLA_HEADS * V_DIM

kernel_name = "hybrid_streaming_encoder_step"


def rms_norm(x, g):
    xf = x.astype(jnp.float32)
    y = xf * lax.rsqrt(jnp.mean(xf * xf, axis=-1, keepdims=True) + EPS)
    return (y * g.astype(jnp.float32)).astype(x.dtype)


def layer_norm(x, g, b):
    xf = x.astype(jnp.float32)
    mu = jnp.mean(xf, axis=-1, keepdims=True)
    xc = xf - mu
    var = jnp.mean(xc * xc, axis=-1, keepdims=True)
    return (xc * lax.rsqrt(var + EPS) * g.astype(jnp.float32) + b.astype(jnp.float32)).astype(x.dtype)


def rope(x, pos):
    half = ROPE_DIM // 2
    inv = ROPE_THETA ** (-jnp.arange(half, dtype=jnp.float32) / half)
    ang = pos.astype(jnp.float32)[:, None] * inv[None, :]
    shape = (1, ang.shape[0]) + (1,) * (x.ndim - 3) + (half,)
    cos = jnp.cos(ang).reshape(shape)
    sin = jnp.sin(ang).reshape(shape)
    xf = x.astype(jnp.float32)
    x1, x2 = xf[..., :half], xf[..., half:]
    return jnp.concatenate([x1 * cos - x2 * sin, x1 * sin + x2 * cos], axis=-1).astype(x.dtype)


def attend_in_query_blocks(fn, q_arrays, q_pos):
    T = q_pos.shape[0]
    if T <= Q_BLOCK:
        return fn(*q_arrays, q_pos)
    nb = T // Q_BLOCK
    blocks = tuple(jnp.moveaxis(a.reshape((a.shape[0], nb, Q_BLOCK) + a.shape[2:]), 1, 0) for a in q_arrays)
    out = lax.map(lambda args: fn(*args), blocks + (q_pos.reshape(nb, Q_BLOCK),))
    out = jnp.moveaxis(out, 0, 1)
    return out.reshape((out.shape[0], T) + out.shape[3:])


def sb_attend(q, k, v, q_pos, k_pos):
    f32 = jnp.float32
    z = jnp.einsum("bqhd,bkhd->bhqk", q.astype(f32), k.astype(f32)) * SB_SCALE
    visible = k_pos[None, :] < q_pos[:, None]
    log_beta = jax.nn.log_sigmoid(z)
    log_1m_beta = jnp.where(visible, jax.nn.log_sigmoid(-z), 0.0)
    later = lax.cumsum(log_1m_beta, axis=3, reverse=True) - log_1m_beta
    w = jnp.where(visible, jnp.exp(log_beta + later), 0.0)
    return jnp.einsum("bhqk,bkhd->bqhd", w, v.astype(f32)).astype(q.dtype)


def short_conv(u, prev, w):
    ext = jnp.concatenate([prev, u], axis=1)
    y = lax.conv_general_dilated(ext, w[:, None, :].astype(ext.dtype), window_strides=(1,), padding="VALID",
                                 dimension_numbers=("NWC", "WIO", "NWC"), feature_group_count=u.shape[-1])
    return y, ext[:, -(CONV_W - 1):]


def spatial_gate(vn, w_s, b_s):
    B, T, _ = vn.shape
    L = min(T, SGU_CHUNK)
    nc = T // L
    vg = vn.reshape(B, nc, L, SGU_GROUPS, SGU_GROUP_DIM)
    tri = jnp.tril(jnp.ones((L, L), dtype=bool))
    ws = jnp.where(tri, w_s[:, :L, :L], 0.0)
    s = jnp.einsum("gts,bcsgd->bctgd", ws, vg) + jnp.transpose(b_s[:, :L])[None, None, :, :, None]
    return s.reshape(B, T, SGU_DIM)


def mla_attend(q_lat, q_pe, ckv, kpe, q_pos, k_pos):
    f32 = jnp.float32
    ckv32 = ckv.astype(f32)
    s = (jnp.einsum("bqhc,bkc->bhqk", q_lat.astype(f32), ckv32)
         + jnp.einsum("bqhr,bkr->bhqk", q_pe.astype(f32), kpe.astype(f32)))
    visible = (k_pos[None, :] // CHUNK) <= (q_pos[:, None] // CHUNK)
    p = jax.nn.softmax(jnp.where(visible, s * MLA_SCALE, -1e30), axis=-1)
    return jnp.einsum("bhqk,bkc->bqhc", p, ckv32).astype(q_lat.dtype)


def even_mixer(h, pos, past, w_in, w_conv, w_out):
    B, T, _ = h.shape
    q, k, v, g_post, g_pre, u = jnp.split(h @ w_in, EVEN_SPLITS, axis=-1)
    q = q.reshape(B, T, SB_HEADS, SB_HEAD_DIM)
    k = k.reshape(B, T, SB_HEADS, SB_HEAD_DIM)
    v = v.reshape(B, T, SB_HEADS, SB_HEAD_DIM)
    conv_in = g_pre * u
    if past is None:
        k_all, v_all, k_pos = k, v, pos
        conv_prev = jnp.zeros((B, CONV_W - 1, CONV_DIM), conv_in.dtype)
    else:
        k_past, v_past, conv_prev = past
        k_all = jnp.concatenate([k_past, k], axis=1)
        v_all = jnp.concatenate([v_past, v], axis=1)
        k_pos = jnp.concatenate([jnp.arange(k_past.shape[1], dtype=jnp.int32), pos])
    attn = attend_in_query_blocks(lambda qb, pb: sb_attend(qb, k_all, v_all, pb, k_pos), (q,), pos)
    conv_out, conv_state = short_conv(conv_in, conv_prev, w_conv)
    mixed = jnp.concatenate([attn.reshape(B, T, SB_WIDTH), g_post * conv_out], axis=-1)
    return mixed @ w_out, (k, v, conv_state)


def odd_mixer(h, pos, past, w_in, ln_g, ln_b, w_s, b_s, q_norm_g, kv_norm_g, w_uq, w_uk, w_uv, w_out):
    B, T, _ = h.shape
    u, v, cq, ckv, kpe = jnp.split(h @ w_in, ODD_SPLITS, axis=-1)
    vn = layer_norm(v, ln_g, ln_b)
    sgu = u * spatial_gate(vn, w_s, b_s)
    cq = rms_norm(cq, q_norm_g)
    qf = (cq @ w_uq).reshape(B, T, MLA_HEADS, NOPE_DIM + ROPE_DIM)
    q_nope = qf[..., :NOPE_DIM]
    q_pe = rope(qf[..., NOPE_DIM:], pos)
    q_lat = jnp.einsum("bthn,hnc->bthc", q_nope, w_uk)
    ckv = rms_norm(ckv, kv_norm_g)
    kpe = rope(kpe, pos)
    if past is None:
        ckv_all, kpe_all, k_pos = ckv, kpe, pos
    else:
        ckv_past, kpe_past = past
        ckv_all = jnp.concatenate([ckv_past, ckv], axis=1)
        kpe_all = jnp.concatenate([kpe_past, kpe], axis=1)
        k_pos = jnp.concatenate([jnp.arange(ckv_past.shape[1], dtype=jnp.int32), pos])
    o_lat = attend_in_query_blocks(lambda ql, qp, pb: mla_attend(ql, qp, ckv_all, kpe_all, pb, k_pos),
                                   (q_lat, q_pe), pos)
    attn = jnp.einsum("bthc,hcv->bthv", o_lat, w_uv).reshape(B, T, MLA_HEADS * V_DIM)
    out = jnp.concatenate([sgu, attn], axis=-1) @ w_out
    return out, (ckv, kpe, vn)


def sq_relu_mlp(h, w_up, w_down):
    return jnp.square(jax.nn.relu(h @ w_up)) @ w_down


def run_trunk(x, pos, past, p):
    sb_k, sb_v, conv, ckv, kpe, sgu_v = [], [], [], [], [], []
    for layer in range(DEPTH):
        j = layer // 2
        h = rms_norm(x, p["mix_pre_g"][layer])
        if layer % 2 == 0:
            lp = None if past is None else (past["sb_k"][j], past["sb_v"][j], past["conv"][j])
            out, (k_new, v_new, c_new) = even_mixer(h, pos, lp, p["even_w_in"][j], p["even_w_conv"][j],
                                                    p["even_w_out"][j])
            sb_k.append(k_new)
            sb_v.append(v_new)
            conv.append(c_new)
        else:
            lp = None if past is None else (past["ckv"][j], past["kpe"][j])
            out, (ckv_new, kpe_new, vn) = odd_mixer(
                h, pos, lp, p["odd_w_in"][j], p["sgu_ln_g"][j], p["sgu_ln_b"][j], p["sgu_w_s"][j],
                p["sgu_b_s"][j], p["mla_q_norm_g"][j], p["mla_kv_norm_g"][j], p["mla_w_uq"][j],
                p["mla_w_uk"][j], p["mla_w_uv"][j], p["odd_w_out"][j])
            ckv.append(ckv_new)
            kpe.append(kpe_new)
            if past is not None:
                sgu_v.append(vn)
        x = x + rms_norm(out, p["mix_post_g"][layer])
        h = rms_norm(x, p["ffn_pre_g"][layer])
        x = x + rms_norm(sq_relu_mlp(h, p["ffn_w_up"][layer], p["ffn_w_down"][layer]), p["ffn_post_g"][layer])
    states = {"sb_k": jnp.stack(sb_k), "sb_v": jnp.stack(sb_v), "conv": jnp.stack(conv),
              "ckv": jnp.stack(ckv), "kpe": jnp.stack(kpe)}
    if past is not None:
        states["sgu_v"] = jnp.stack(sgu_v)
    return x, states


def _normal(k, shape, scale):
    return scale * jax.random.normal(k, shape, jnp.float32)


def _gain(k, shape):
    return 1.0 + 0.05 * jax.random.normal(k, shape, jnp.float32)


def setup_inputs(seed: int = 0) -> dict:
    key = jax.random.key(seed)
    ks = jax.random.split(key, 27)
    return {
        "x_prompt": _normal(ks[0], (BATCH, SEQ, D_MODEL), 1.0),
        "x_sample": _normal(ks[1], (DEC_BATCH, DEC_SEQ, D_MODEL), 1.0),
        "cache_sb_k": _normal(ks[2], (N_EVEN, DEC_BATCH, PAST_LEN, SB_HEADS, SB_HEAD_DIM), 1.0),
        "cache_sb_v": _normal(ks[3], (N_EVEN, DEC_BATCH, PAST_LEN, SB_HEADS, SB_HEAD_DIM), 1.0),
        "state_conv": _normal(ks[4], (N_EVEN, DEC_BATCH, CONV_W - 1, CONV_DIM), 1.0),
        "cache_mla_ckv": _normal(ks[5], (N_ODD, DEC_BATCH, PAST_LEN, KV_LORA), 1.0),
        "cache_mla_kpe": _normal(ks[6], (N_ODD, DEC_BATCH, PAST_LEN, ROPE_DIM), 1.0),
        "mix_pre_g": _gain(ks[7], (DEPTH, D_MODEL)),
        "mix_post_g": _gain(ks[8], (DEPTH, D_MODEL)),
        "ffn_pre_g": _gain(ks[9], (DEPTH, D_MODEL)),
        "ffn_post_g": _gain(ks[10], (DEPTH, D_MODEL)),
        "even_w_in": _normal(ks[11], (N_EVEN, D_MODEL, EVEN_IN), D_MODEL ** -0.5),
        "even_w_conv": _normal(ks[12], (N_EVEN, CONV_W, CONV_DIM), CONV_W ** -0.5),
        "even_w_out": _normal(ks[13], (N_EVEN, EVEN_MIX, D_MODEL), EVEN_MIX ** -0.5),
        "odd_w_in": _normal(ks[14], (N_ODD, D_MODEL, ODD_IN), D_MODEL ** -0.5),
        "sgu_ln_g": _gain(ks[15], (N_ODD, SGU_DIM)),
        "sgu_ln_b": _normal(ks[16], (N_ODD, SGU_DIM), 0.02),
        "sgu_w_s": _normal(ks[17], (N_ODD, SGU_GROUPS, SGU_CHUNK, SGU_CHUNK), SGU_CHUNK ** -0.5),
        "sgu_b_s": _gain(ks[18], (N_ODD, SGU_GROUPS, SGU_CHUNK)),
        "mla_q_norm_g": _gain(ks[19], (N_ODD, Q_LORA)),
        "mla_kv_norm_g": _gain(ks[20], (N_ODD, KV_LORA)),
        "mla_w_uq": _normal(ks[21], (N_ODD, Q_LORA, MLA_HEADS * (NOPE_DIM + ROPE_DIM)), Q_LORA ** -0.5),
        "mla_w_uk": _normal(ks[22], (N_ODD, MLA_HEADS, NOPE_DIM, KV_LORA), KV_LORA ** -0.5),
        "mla_w_uv": _normal(ks[23], (N_ODD, MLA_HEADS, KV_LORA, V_DIM), KV_LORA ** -0.5),
        "odd_w_out": _normal(ks[24], (N_ODD, ODD_MIX, D_MODEL), ODD_MIX ** -0.5),
        "ffn_w_up": _normal(ks[25], (DEPTH, D_MODEL, D_FF), D_MODEL ** -0.5),
        "ffn_w_down": _normal(ks[26], (DEPTH, D_FF, D_MODEL), D_FF ** -0.5),
    }


def reference(x_prompt, x_sample, cache_sb_k, cache_sb_v, state_conv, cache_mla_ckv, cache_mla_kpe,
              mix_pre_g, mix_post_g, ffn_pre_g, ffn_post_g, even_w_in, even_w_conv, even_w_out,
              odd_w_in, sgu_ln_g, sgu_ln_b, sgu_w_s, sgu_b_s, mla_q_norm_g, mla_kv_norm_g,
              mla_w_uq, mla_w_uk, mla_w_uv, odd_w_out, ffn_w_up, ffn_w_down):
    params = {
        "mix_pre_g": mix_pre_g, "mix_post_g": mix_post_g, "ffn_pre_g": ffn_pre_g, "ffn_post_g": ffn_post_g,
        "even_w_in": even_w_in, "even_w_conv": even_w_conv, "even_w_out": even_w_out,
        "odd_w_in": odd_w_in, "sgu_ln_g": sgu_ln_g, "sgu_ln_b": sgu_ln_b, "sgu_w_s": sgu_w_s,
        "sgu_b_s": sgu_b_s, "mla_q_norm_g": mla_q_norm_g, "mla_kv_norm_g": mla_kv_norm_g,
        "mla_w_uq": mla_w_uq, "mla_w_uk": mla_w_uk, "mla_w_uv": mla_w_uv, "odd_w_out": odd_w_out,
        "ffn_w_up": ffn_w_up, "ffn_w_down": ffn_w_down,
    }
    pos_p = jnp.arange(x_prompt.shape[1], dtype=jnp.int32)
    y_prompt, st_p = run_trunk(x_prompt, pos_p, None, params)
    past_len = cache_sb_k.shape[2]
    pos_s = past_len + jnp.arange(x_sample.shape[1], dtype=jnp.int32)
    past = {"sb_k": cache_sb_k, "sb_v": cache_sb_v, "conv": state_conv,
            "ckv": cache_mla_ckv, "kpe": cache_mla_kpe}
    y_sample, st_s = run_trunk(x_sample, pos_s, past, params)
    return (y_prompt, y_sample,
            st_p["sb_k"], st_p["sb_v"], st_p["conv"], st_p["ckv"], st_p["kpe"],
            st_s["sb_k"], st_s["sb_v"], st_s["conv"], st_s["ckv"], st_s["kpe"], st_s["sgu_v"])
```

```python
import functools
import math

import numpy as np
import jax
import jax.numpy as jnp
from jax import lax
from jax.experimental import pallas as pl
from jax.experimental.pallas import tpu as pltpu

F32 = jnp.float32
BF16 = jnp.bfloat16

EPS = 1e-6
D_MODEL = 1024
CHUNK = 64
SB_HEADS = 8
SB_HEAD_DIM = 64
SB_WIDTH = SB_HEADS * SB_HEAD_DIM
SB_SCALE = 1.0 / math.sqrt(SB_HEAD_DIM)
CONV_DIM = D_MODEL // 2
CONV_W = 3
SGU_CHUNK = 128
SGU_GROUPS = 4
SGU_DIM = D_MODEL // 2
SGU_GROUP_DIM = SGU_DIM // SGU_GROUPS
MLA_HEADS = 8
Q_LORA = 384
KV_LORA = 256
NOPE_DIM = 64
ROPE_DIM = 32
V_DIM = 64
ROPE_THETA = 10000.0
MLA_SCALE = 1.0 / math.sqrt(NOPE_DIM + ROPE_DIM)
D_FF = 4 * D_MODEL
FF_CHUNK = 1024

LANES = 128
HEAD_PAIRS = SB_HEADS // 2
ODD_IN_PAD = 2 * SGU_DIM + Q_LORA + KV_LORA + LANES
VMEM_LIMIT = 56 * 1024 * 1024
NEG_INF = -1e30


def _cparams(sem):
    return pltpu.CompilerParams(dimension_semantics=sem, vmem_limit_bytes=VMEM_LIMIT)


def _const_spec(shape):
    nd = len(shape)
    return pl.BlockSpec(shape, lambda *_: (0,) * nd, pipeline_mode=pl.Buffered(1))


def _rms(x, g):
    return x * lax.rsqrt(jnp.mean(x * x, axis=-1, keepdims=True) + EPS) * g


def _dot(a, b):
    return jnp.dot(a, b, preferred_element_type=F32)


def _dot_nt(a, b):
    return lax.dot_general(a, b, (((1,), (1,)), ((), ())), preferred_element_type=F32)


def _log_sigmoid(z):
    return jnp.minimum(z, 0.0) - jnp.log(1.0 + jnp.exp(-jnp.abs(z)))


def _split_hi_lo(x):
    hi = x.astype(BF16)
    lo = (x - hi.astype(F32)).astype(BF16)
    return hi, lo


def _rope_block(x, cos, sin):
    half = ROPE_DIM // 2
    lane = lax.broadcasted_iota(jnp.int32, x.shape, 1)
    partner = jnp.where(lane < half, pltpu.roll(x, LANES - half, 1), pltpu.roll(x, half, 1))
    return x * cos + partner * sin


def _proj_even_body(*refs, tm, seq_len, whole_seqs):
    if whole_seqs:
        (x_ref, g_ref, w_ref, wc_ref, e1_ref, e2_ref,
         q_ref, k_ref, v_ref, kb_ref, vb_ref, gc_ref, ci_ref) = refs
    else:
        (x_ref, g_ref, w_ref, wc_ref,
         q_ref, k_ref, v_ref, kb_ref, vb_ref, gc_ref, cs_ref, tail_ref) = refs
    h = _rms(x_ref[...], g_ref[...]).astype(BF16)

    def proj(j):
        return _dot(h, w_ref[:, j * SB_WIDTH:(j + 1) * SB_WIDTH])

    q_ref[...] = (proj(0) * SB_SCALE).astype(BF16)
    k = proj(1)
    k_ref[...] = k
    kb_ref[...] = k.astype(BF16)
    v = proj(2)
    v_ref[...] = v
    vb_ref[...] = v.astype(BF16)
    g_post = proj(3)
    ci = proj(4) * proj(5)
    r1 = pltpu.roll(ci, 1, 0)
    r2 = pltpu.roll(ci, 2, 0)
    row = lax.broadcasted_iota(jnp.int32, (tm, 1), 0)
    if whole_seqs:
        tpos = row % seq_len
        s1 = jnp.where(tpos < 1, e1_ref[...], r1)
        s2 = jnp.where(tpos < 2, e2_ref[...], r2)
        ci_ref[...] = ci
    else:
        first = (pl.program_id(0) % (seq_len // tm)) == 0

        @pl.when(first)
        def _():
            tail_ref[...] = jnp.zeros_like(tail_ref)

        t1 = tail_ref[7:8, :]
        t2 = tail_ref[6:7, :]
        s1 = jnp.where(row == 0, t1, r1)
        s2 = jnp.where(row == 0, t2, jnp.where(row == 1, t1, r2))
        tail_ref[...] = ci[tm - 8:, :]
        cs_ref[0] = ci[tm - (CONV_W - 1):, :]
    conv = wc_ref[0:1, :] * s2 + wc_ref[1:2, :] * s1 + wc_ref[2:3, :] * ci
    gc_ref[...] = (g_post * conv).astype(BF16)


def _proj_even(x, g, w_in, w_conv, conv_prev, *, batch, seq_len, tm):
    n = x.shape[0]
    whole = tm % seq_len == 0
    grid = (n // tm,)
    row_spec = lambda w: pl.BlockSpec((tm, w), lambda i: (i, 0))
    in_specs = [row_spec(D_MODEL), _const_spec((1, D_MODEL)), _const_spec(w_in.shape), _const_spec(w_conv.shape)]
    args = [x, g, w_in, w_conv]
    out_shape = [jax.ShapeDtypeStruct((n, SB_WIDTH), BF16), jax.ShapeDtypeStruct((n, SB_WIDTH), F32),
                 jax.ShapeDtypeStruct((n, SB_WIDTH), F32), jax.ShapeDtypeStruct((n, SB_WIDTH), BF16),
                 jax.ShapeDtypeStruct((n, SB_WIDTH), BF16), jax.ShapeDtypeStruct((n, CONV_DIM), BF16)]
    out_specs = [row_spec(SB_WIDTH)] * 5 + [row_spec(CONV_DIM)]
    scratch = []
    if whole:
        if conv_prev is None:
            conv_prev = jnp.zeros((batch, CONV_W - 1, CONV_DIM), F32)
        e1 =jnp.zeros((batch, seq_len, CONV_DIM), F32).at[:, 0].set(conv_prev[:, 1])
        e2 = jnp.zeros((batch, seq_len, CONV_DIM), F32).at[:, 0].set(conv_prev[:, 0]).at[:, 1].set(conv_prev[:, 1])
        args += [e1.reshape(n, CONV_DIM), e2.reshape(n, CONV_DIM)]
        in_specs += [row_spec(CONV_DIM), row_spec(CONV_DIM)]
        out_shape.append(jax.ShapeDtypeStruct((n, CONV_DIM), F32))
        out_specs.append(row_spec(CONV_DIM))
    else:
        assert seq_len % tm == 0 and conv_prev is None
        seq_tiles = seq_len // tm
        out_shape.append(jax.ShapeDtypeStruct((batch, CONV_W - 1, CONV_DIM), F32))
        out_specs.append(pl.BlockSpec((1, CONV_W - 1, CONV_DIM), lambda i: (i // seq_tiles, 0, 0)))
        scratch.append(pltpu.VMEM((8, CONV_DIM), F32))
    outs = pl.pallas_call(
        functools.partial(_proj_even_body, tm=tm, seq_len=seq_len, whole_seqs=whole),
        grid=grid, in_specs=in_specs, out_specs=out_specs, out_shape=out_shape, scratch_shapes=scratch,
        compiler_params=_cparams(("arbitrary",)), name="proj_even")(*args)
    q, k, v, kb, vb, gc, last = outs
    if whole:
        last = last.reshape(batch, seq_len, CONV_DIM)[:, seq_len - (CONV_W - 1):]
    return q, k, v, kb, vb, gc, last


def _suffix_rhs(tk):
    j = np.arange(2 * tk)[:, None] % tk
    c = np.arange(2 * tk)[None, :]
    return jnp.asarray(np.where(c < tk, j > c, True), dtype=BF16)


def _sb_prompt_body(q_ref, k_ref, v_ref, r_ref, o_ref, carry_ref, acc_ref, *, tq):
    tk = tq
    qi = pl.program_id(2)
    q = q_ref[...]
    lane = lax.broadcasted_iota(jnp.int32, (tq, LANES), 1)
    low = lane < SB_HEAD_DIM
    q_heads = (jnp.where(low, q, jnp.zeros_like(q)), jnp.where(low, jnp.zeros_like(q), q))
    rhs = r_ref[...]
    carry_ref[...] = jnp.zeros_like(carry_ref)
    acc_ref[...] = jnp.zeros_like(acc_ref)

    def tile(kj, visible):
        start = pl.multiple_of(kj * tk, tk)
        kblk = k_ref[pl.ds(start, tk), :]
        vblk = v_ref[pl.ds(start, tk), :]
        for hh in range(2):
            z = _dot_nt(q_heads[hh], kblk)
            log_beta = _log_sigmoid(z)
            log_1m = log_beta - z
            if visible is not None:
                log_1m = jnp.where(visible, log_1m, 0.0)
            hi, lo = _split_hi_lo(log_1m)
            sums = _dot(jnp.concatenate([hi, lo], axis=1), rhs)
            w = jnp.exp(log_beta + sums[:, :tk] + carry_ref[hh])
            if visible is not None:
                w = jnp.where(visible, w, 0.0)
            acc_ref[hh] += _dot(w.astype(BF16), vblk)
            carry_ref[hh] += sums[:, tk:]

    rr = lax.broadcasted_iota(jnp.int32, (tq, tk), 0)
    cc = lax.broadcasted_iota(jnp.int32, (tq, tk), 1)
    tile(qi, cc < rr)

    def body(it, c):
        tile(qi - 1 - it, None)
        return c

    lax.fori_loop(0, qi, body, 0)
    o_ref[...] = jnp.where(low, acc_ref[0], acc_ref[1]).astype(BF16)


def _sb_prompt(q, kb, vb, *, batch, seq_len, tq):
    n = q.shape[0]
    nq = seq_len // tq
    rhs = _suffix_rhs(tq)
    return pl.pallas_call(
        functools.partial(_sb_prompt_body, tq=tq),
        grid=(batch, HEAD_PAIRS, nq),
        in_specs=[pl.BlockSpec((tq, LANES), lambda b, p, i: (b * nq + i, p)),
                  pl.BlockSpec((seq_len, LANES), lambda b, p, i: (b, p)),
                  pl.BlockSpec((seq_len, LANES), lambda b, p, i: (b, p)),
                  _const_spec(rhs.shape)],
        out_specs=pl.BlockSpec((tq, LANES), lambda b, p, i: (b * nq + i, p)),
        out_shape=jax.ShapeDtypeStruct((n, SB_WIDTH), BF16),
        scratch_shapes=[pltpu.VMEM((2, tq, LANES), F32), pltpu.VMEM((2, tq, LANES), F32)],
        compiler_params=_cparams(("arbitrary", "arbitrary", "arbitrary")), name="sb_attn_prompt")(q, kb, vb, rhs)


def _suffix_lhs(tk):
    r = np.arange(2 * tk)[:, None]
    c = np.arange(2 * tk)[None, :] % tk
    return jnp.asarray(np.where(r < tk, c > r, True), dtype=BF16)


def _sb_sample_body(q_ref, kn_ref, vn_ref, kc_ref, vc_ref, l_ref, o_ref, qrows_ref, carry_ref, acc_ref,
                    *, dec, kblk, tk):
    j = pl.program_id(1)
    rows = SB_HEADS * dec
    lhs = l_ref[...]

    def tile(kb, vb, visible):
        zt = _dot_nt(kb, qrows_ref[...])
        log_beta = _log_sigmoid(zt)
        log_1m = log_beta - zt
        if visible is not None:
            log_1m = jnp.where(visible, log_1m, 0.0)
        hi, lo = _split_hi_lo(log_1m)
        sums = _dot(lhs, jnp.concatenate([hi, lo], axis=0))
        w = jnp.exp(log_beta + sums[:tk] + carry_ref[...])
        if visible is not None:
            w = jnp.where(visible, w, 0.0)
        acc_ref[...] += _dot(w.T.astype(BF16), vb)
        carry_ref[...] += sums[tk:]

    @pl.when(j == 0)
    def _():
        q = q_ref[...]
        qt = jnp.concatenate([q] * SB_HEADS, axis=0)
        rr = lax.broadcasted_iota(jnp.int32, (rows, SB_WIDTH), 0)
        cc = lax.broadcasted_iota(jnp.int32, (rows, SB_WIDTH), 1)
        qrows_ref[...] = jnp.where(rr // dec == cc // SB_HEAD_DIM, qt, jnp.zeros_like(qt))
        carry_ref[...] = jnp.zeros_like(carry_ref)
        acc_ref[...] = jnp.zeros_like(acc_ref)
        pad = jnp.zeros((tk - dec, SB_WIDTH), BF16)
        key_i = lax.broadcasted_iota(jnp.int32, (tk, rows), 0)
        query_t = lax.broadcasted_iota(jnp.int32, (tk, rows), 1) % dec
        tile(jnp.concatenate([kn_ref[...], pad], axis=0), jnp.concatenate([vn_ref[...], pad], axis=0),
             key_i < query_t)

    def body(it, c):
        start = pl.multiple_of(kblk - tk - it * tk, tk)
        tile(kc_ref[0, pl.ds(start, tk), :].astype(BF16), vc_ref[0, pl.ds(start, tk), :].astype(BF16), None)
        return c

    lax.fori_loop(0, kblk // tk, body, 0)

    @pl.when(j == pl.num_programs(1) - 1)
    def _():
        cc = lax.broadcasted_iota(jnp.int32, (dec, SB_WIDTH), 1)
        out = jnp.zeros((dec, SB_WIDTH), F32)
        for h in range(SB_HEADS):
            out = out + jnp.where(cc // SB_HEAD_DIM == h, acc_ref[h * dec:(h + 1) * dec, :], 0.0)
        o_ref[...] = out.astype(BF16)


def _sb_sample(q, kb, vb, cache_k, cache_v, *, batch, dec, kblk=1024, tk=128):
    n = q.shape[0]
    past = cache_k.shape[1]
    nkb = past // kblk
    rows = SB_HEADS * dec
    lhs = _suffix_lhs(tk)
    new_spec = pl.BlockSpec((dec, SB_WIDTH), lambda b, j: (b, 0))
    cache_spec = pl.BlockSpec((1, kblk, SB_WIDTH), lambda b, j: (b, nkb - 1 - j, 0))
    return pl.pallas_call(
        functools.partial(_sb_sample_body, dec=dec, kblk=kblk, tk=tk),
        grid=(batch, nkb),
        in_specs=[new_spec, new_spec, new_spec, cache_spec, cache_spec, _const_spec(lhs.shape)],
        out_specs=new_spec,
        out_shape=jax.ShapeDtypeStruct((n, SB_WIDTH), BF16),
        scratch_shapes=[pltpu.VMEM((rows, SB_WIDTH), BF16), pltpu.VMEM((tk, rows), F32),
                        pltpu.VMEM((rows, SB_WIDTH), F32)],
        compiler_params=_cparams(("arbitrary", "arbitrary")), name="sb_attn_sample")(
            q, kb, vb, cache_k, cache_v, lhs)


def _proj_odd_body(*refs, tm, gate_len, emit_vn):
    (x_ref, g_ref, w_ref, lng_ref, lnb_ref, ws_ref, bs_ref, qg_ref, kvg_ref, wqn_ref, wqp_ref,
     cos_ref, sin_ref) = refs[:13]
    outs = refs[13:]
    if emit_vn:
        sgu_ref, vn_ref, qn_ref, qp_ref, ckv_ref, ckvb_ref, kpe_ref, kpeb_ref = outs
    else:
        sgu_ref, qn_ref, qp_ref, ckv_ref, ckvb_ref, kpe_ref, kpeb_ref = outs
    h = _rms(x_ref[...], g_ref[...]).astype(BF16)
    o_v, o_q, o_kv, o_pe = SGU_DIM, 2 * SGU_DIM, 2 * SGU_DIM + Q_LORA, 2 * SGU_DIM + Q_LORA + KV_LORA
    u = _dot(h, w_ref[:, :o_v])
    v = _dot(h, w_ref[:, o_v:o_q])
    mu = jnp.mean(v, axis=-1, keepdims=True)
    vc = v - mu
    var = jnp.mean(vc * vc, axis=-1, keepdims=True)
    vn = vc * lax.rsqrt(var + EPS) * lng_ref[...] + lnb_ref[...]
    if emit_vn:
        vn_ref[...] = vn
    vnb = vn.astype(BF16)
    rr = lax.broadcasted_iota(jnp.int32, (SGU_CHUNK, SGU_CHUNK), 0)
    cc = lax.broadcasted_iota(jnp.int32, (SGU_CHUNK, SGU_CHUNK), 1)
    causal = (rr // gate_len == cc // gate_len) & (cc <= rr)
    for g in range(SGU_GROUPS):
        gs = slice(g * SGU_GROUP_DIM, (g + 1) * SGU_GROUP_DIM)
        wg = jnp.where(causal, ws_ref[g], 0.0).astype(BF16)
        for c in range(tm // SGU_CHUNK):
            rs = slice(c * SGU_CHUNK, (c + 1) * SGU_CHUNK)
            s = _dot(wg, vnb[rs, gs]) + bs_ref[:, gs]
            sgu_ref[rs, gs] = (u[rs, gs] * s).astype(BF16)
    cq = _rms(_dot(h, w_ref[:, o_q:o_kv]), qg_ref[...]).astype(BF16)
    qn_ref[...] = _dot(cq, wqn_ref[...]).astype(BF16)
    cos = cos_ref[...]
    sin = sin_ref[...]
    for hd in range(MLA_HEADS):
        qp = _rope_block(_dot(cq, wqp_ref[:, hd * LANES:(hd + 1) * LANES]), cos, sin)
        qp_ref[hd] = qp[:, :ROPE_DIM].astype(BF16)
    ckv = _rms(_dot(h, w_ref[:, o_kv:o_pe]), kvg_ref[...])
    ckv_ref[...] = ckv
    ckvb_ref[...] = ckv.astype(BF16)
    kpe = _rope_block(_dot(h, w_ref[:, o_pe:]), cos, sin)[:, :ROPE_DIM]
    kpe_ref[...] = kpe
    kpeb_ref[...] = kpe.astype(BF16)


def _proj_odd(x, g, w_in, ln_g, ln_b, w_s, b_s, qg, kvg, wqn, wqp, cos, sin, *, tm, gate_len, emit_vn):
    n = x.shape[0]
    pos_tiles = cos.shape[0] // tm
    row_spec = lambda w: pl.BlockSpec((tm, w), lambda i: (i, 0))
    tab_spec = pl.BlockSpec((tm, LANES), lambda i: (i % pos_tiles, 0))
    consts = [g, w_in, ln_g, ln_b, w_s, b_s, qg, kvg, wqn, wqp]
    out_shape = [jax.ShapeDtypeStruct((n, SGU_DIM), BF16)]
    out_specs = [row_spec(SGU_DIM)]
    if emit_vn:
        out_shape.append(jax.ShapeDtypeStruct((n, SGU_DIM), F32))
        out_specs.append(row_spec(SGU_DIM))
    out_shape += [jax.ShapeDtypeStruct((n, MLA_HEADS * NOPE_DIM), BF16),
                  jax.ShapeDtypeStruct((MLA_HEADS, n, ROPE_DIM), BF16),
                  jax.ShapeDtypeStruct((n, KV_LORA), F32), jax.ShapeDtypeStruct((n, KV_LORA), BF16),
                  jax.ShapeDtypeStruct((n, ROPE_DIM), F32), jax.ShapeDtypeStruct((n, ROPE_DIM), BF16)]
    out_specs += [row_spec(MLA_HEADS * NOPE_DIM), pl.BlockSpec((MLA_HEADS, tm, ROPE_DIM), lambda i: (0, i, 0)),
                  row_spec(KV_LORA), row_spec(KV_LORA), row_spec(ROPE_DIM), row_spec(ROPE_DIM)]
    return pl.pallas_call(
        functools.partial(_proj_odd_body, tm=tm, gate_len=gate_len, emit_vn=emit_vn),
        grid=(n // tm,),
        in_specs=[row_spec(D_MODEL)] + [_const_spec(c.shape) for c in consts] + [tab_spec, tab_spec],
        out_specs=out_specs, out_shape=out_shape,
        compiler_params=_cparams(("arbitrary",)), name="proj_odd")(x, *consts, cos, sin)


def _mla_queries(qn, wuk_ref, qlat_ref, tq):
    lane = lax.broadcasted_iota(jnp.int32, (tq, LANES), 1)
    low = lane < NOPE_DIM
    for hd in range(MLA_HEADS):
        p = hd // 2
        pair = qn[:, p * LANES:(p + 1) * LANES]
        qm = jnp.where(low if hd % 2 == 0 else jnp.logical_not(low), pair, jnp.zeros_like(pair))
        qlat_ref[hd * tq:(hd + 1) * tq, :] = _dot(qm, wuk_ref[p]).astype(BF16)


def _mla_tile(qlat_ref, qpe, ck, kp, m_ref, l_ref, acc_ref, visible):
    s = (_dot_nt(qlat_ref[...], ck) + _dot_nt(qpe, kp)) * MLA_SCALE
    if visible is not None:
        s = jnp.where(visible, s, NEG_INF)
    m_prev = m_ref[...]
    m_new = jnp.maximum(m_prev, jnp.max(s, axis=-1, keepdims=True))
    alpha = jnp.exp(m_prev - m_new)
    p = jnp.exp(s - m_new)
    l_ref[...] = alpha * l_ref[...] + jnp.sum(p, axis=-1, keepdims=True)
    acc_ref[...] = alpha * acc_ref[...] + _dot(p.astype(BF16), ck)
    m_ref[...] = m_new


def _mla_finish(l_ref, acc_ref, wuv_ref, o_ref, tq):
    o_lat = (acc_ref[...] / l_ref[...]).astype(BF16)
    for p in range(MLA_HEADS // 2):
        h0, h1 = 2 * p, 2 * p + 1
        o_ref[:, p * LANES:(p + 1) * LANES] = (
            _dot(o_lat[h0 * tq:(h0 + 1) * tq], wuv_ref[h0]) + _dot(o_lat[h1 * tq:(h1 + 1) * tq], wuv_ref[h1])
        ).astype(BF16)


def _mla_init(m_ref, l_ref, acc_ref):
    m_ref[...] = jnp.full_like(m_ref, -jnp.inf)
    l_ref[...] = jnp.zeros_like(l_ref)
    acc_ref[...] = jnp.zeros_like(acc_ref)


def _mla_prompt_body(qn_ref, qp_ref, ckv_ref, kpe_ref, wuk_ref, wuv_ref, o_ref,
                     qlat_ref, m_ref, l_ref, acc_ref, *, tq, tk):
    qi = pl.program_id(1)
    rows = MLA_HEADS * tq
    _mla_queries(qn_ref[...], wuk_ref, qlat_ref, tq)
    qpe = qp_ref[...].reshape(rows, ROPE_DIM)
    _mla_init(m_ref, l_ref, acc_ref)

    def tile(kj, visible):
        start = pl.multiple_of(kj * tk, tk)
        _mla_tile(qlat_ref, qpe, ckv_ref[pl.ds(start, tk), :], kpe_ref[pl.ds(start, tk), :],
                  m_ref, l_ref, acc_ref, visible)

    n_full = (qi * tq) // tk

    def body(kj, c):
        tile(kj, None)
        return c

    lax.fori_loop(0, n_full, body, 0)
    q_pos = qi * tq + lax.broadcasted_iota(jnp.int32, (rows, tk), 0) % tq
    k_pos = n_full * tk + lax.broadcasted_iota(jnp.int32, (rows, tk), 1)
    tile(n_full, k_pos // CHUNK <= q_pos // CHUNK)
    _mla_finish(l_ref, acc_ref, wuv_ref, o_ref, tq)


def _mla_scratch(rows):
    return [pltpu.VMEM((rows, KV_LORA), BF16), pltpu.VMEM((rows, 1), F32), pltpu.VMEM((rows, 1), F32),
            pltpu.VMEM((rows, KV_LORA), F32)]


def _mla_prompt(qn, qp, ckvb, kpeb, wuk, wuv, *, batch, seq_len, tq, tk):
    n = qn.shape[0]
    nq = seq_len // tq
    width = MLA_HEADS * V_DIM
    return pl.pallas_call(
        functools.partial(_mla_prompt_body, tq=tq, tk=tk),
        grid=(batch, nq),
        in_specs=[pl.BlockSpec((tq, MLA_HEADS * NOPE_DIM), lambda b, i: (b * nq + i, 0)),
                  pl.BlockSpec((MLA_HEADS, tq, ROPE_DIM), lambda b, i: (0, b * nq + i, 0)),
                  pl.BlockSpec((seq_len, KV_LORA), lambda b, i: (b, 0)),
                  pl.BlockSpec((seq_len, ROPE_DIM), lambda b, i: (b, 0)),
                  _const_spec(wuk.shape), _const_spec(wuv.shape)],
        out_specs=pl.BlockSpec((tq, width), lambda b, i: (b * nq + i, 0)),
        out_shape=jax.ShapeDtypeStruct((n, width), BF16),
        scratch_shapes=_mla_scratch(MLA_HEADS * tq),
        compiler_params=_cparams(("arbitrary", "arbitrary")), name="mla_attn_prompt")(qn, qp, ckvb, kpeb, wuk, wuv)


def _mla_sample_body(qn_ref, qp_ref, cn_ref, pn_ref, cc_ref, pc_ref, wuk_ref, wuv_ref, o_ref,
                     qlat_ref, m_ref, l_ref, acc_ref, *, dec, past, kblk, tk):
    j = pl.program_id(1)
    rows = MLA_HEADS * dec
    qpe = qp_ref[...].reshape(rows, ROPE_DIM)

    @pl.when(j == 0)
    def _():
        _mla_queries(qn_ref[...], wuk_ref, qlat_ref, dec)
        _mla_init(m_ref, l_ref, acc_ref)
        ck = jnp.concatenate([cn_ref[...], jnp.zeros((tk - dec, KV_LORA), BF16)], axis=0)
        kp = jnp.concatenate([pn_ref[...], jnp.zeros((tk - dec, ROPE_DIM), BF16)], axis=0)
        col = lax.broadcasted_iota(jnp.int32, (rows, tk), 1)
        q_pos = past + lax.broadcasted_iota(jnp.int32, (rows, tk), 0) % dec
        visible = (col < dec) & ((past + col) // CHUNK <= q_pos // CHUNK)
        _mla_tile(qlat_ref, qpe, ck, kp, m_ref, l_ref, acc_ref, visible)

    def body(it, c):
        start = pl.multiple_of(it * tk, tk)
        _mla_tile(qlat_ref, qpe, cc_ref[0, pl.ds(start, tk), :].astype(BF16),
                  pc_ref[0, pl.ds(start, tk), :].astype(BF16), m_ref, l_ref, acc_ref, None)
        return c

    lax.fori_loop(0, kblk // tk, body, 0)

    @pl.when(j == pl.num_programs(1) - 1)
    def _():
        _mla_finish(l_ref, acc_ref, wuv_ref, o_ref, dec)


def _mla_sample(qn, qp, ckvb, kpeb, cache_ckv, cache_kpe, wuk, wuv, *, batch, dec, kblk=1024, tk=256):
    n = qn.shape[0]
    past = cache_ckv.shape[1]
    width = MLA_HEADS * V_DIM
    return pl.pallas_call(
        functools.partial(_mla_sample_body, dec=dec, past=past, kblk=kblk, tk=tk),
        grid=(batch, past // kblk),
        in_specs=[pl.BlockSpec((dec, MLA_HEADS * NOPE_DIM), lambda b, j: (b, 0)),
                  pl.BlockSpec((MLA_HEADS, dec, ROPE_DIM), lambda b, j: (0, b, 0)),
                  pl.BlockSpec((dec, KV_LORA), lambda b, j: (b, 0)),
                  pl.BlockSpec((dec, ROPE_DIM), lambda b, j: (b, 0)),
                  pl.BlockSpec((1, kblk, KV_LORA), lambda b, j: (b, j, 0)),
                  pl.BlockSpec((1, kblk, ROPE_DIM), lambda b, j: (b, j, 0)),
                  _const_spec(wuk.shape), _const_spec(wuv.shape)],
        out_specs=pl.BlockSpec((dec, width), lambda b, j: (b, 0)),
        out_shape=jax.ShapeDtypeStruct((n, width), BF16),
        scratch_shapes=_mla_scratch(MLA_HEADS * dec),
        compiler_params=_cparams(("arbitrary", "arbitrary")), name="mla_attn_sample")(
            qn, qp, ckvb, kpeb, cache_ckv, cache_kpe, wuk, wuv)


def _out_ffn_body(a_ref, b_ref, x_ref, woa_ref, wob_ref, gpost_ref, gpre_ref, wup_ref, wdn_ref, gfpost_ref, o_ref):
    mixed = _dot(a_ref[...], woa_ref[...]) + _dot(b_ref[...], wob_ref[...])
    x1 = x_ref[...] + _rms(mixed, gpost_ref[...])
    h = _rms(x1, gpre_ref[...]).astype(BF16)
    down = jnp.zeros_like(x1)
    for c in range(D_FF // FF_CHUNK):
        up = _dot(h, wup_ref[:, c * FF_CHUNK:(c + 1) * FF_CHUNK])
        act = jnp.square(jnp.maximum(up, 0.0)).astype(BF16)
        down = down + _dot(act, wdn_ref[c * FF_CHUNK:(c + 1) * FF_CHUNK, :])
    o_ref[...] = x1 + _rms(down, gfpost_ref[...])


def _out_ffn(a, b, x, woa, wob, g_post, g_pre, w_up, w_down, g_fpost, *, tm):
    n = x.shape[0]
    row_spec = lambda w: pl.BlockSpec((tm, w), lambda i: (i, 0))
    consts = [woa, wob, g_post, g_pre, w_up, w_down, g_fpost]
    return pl.pallas_call(
        _out_ffn_body,
        grid=(n // tm,),
        in_specs=[row_spec(a.shape[1]), row_spec(b.shape[1]), row_spec(D_MODEL)] + [_const_spec(c.shape) for c in consts],
        out_specs=row_spec(D_MODEL),
        out_shape=jax.ShapeDtypeStruct((n, D_MODEL), F32),
        compiler_params=_cparams(("arbitrary",)), name="out_ffn")(a, b, x, *consts)


def _rope_tables(pos, reps):
    half = ROPE_DIM // 2
    inv = ROPE_THETA ** (-jnp.arange(half, dtype=F32) / half)
    ang = pos.astype(F32)[:, None] * inv[None, :]
    zeros = jnp.zeros((pos.shape[0], LANES - ROPE_DIM), F32)
    cos = jnp.concatenate([jnp.cos(ang), jnp.cos(ang), zeros], axis=1)
    sin = jnp.concatenate([-jnp.sin(ang), jnp.sin(ang), zeros], axis=1)
    return jnp.tile(cos, (reps, 1)), jnp.tile(sin, (reps, 1))


def _prep_even(p, j):
    return dict(w_in=p["even_w_in"][j].astype(BF16), w_conv=p["even_w_conv"][j],
                woa=p["even_w_out"][j, :SB_WIDTH].astype(BF16), wob=p["even_w_out"][j, SB_WIDTH:].astype(BF16))


def _prep_odd(p, j, gate_len):
    w_in = p["odd_w_in"][j]
    w_in = jnp.pad(w_in, ((0, 0), (0, ODD_IN_PAD - w_in.shape[1]))).astype(BF16)
    reps = SGU_CHUNK // gate_len
    w_s = jnp.tile(p["sgu_w_s"][j, :, :gate_len, :gate_len], (1, reps, reps))
    b_s = jnp.tile(p["sgu_b_s"][j, :, :gate_len], (1, reps))
    b_s = jnp.repeat(b_s.T, SGU_GROUP_DIM, axis=1)
    w_uq = p["mla_w_uq"][j].reshape(Q_LORA, MLA_HEADS, NOPE_DIM + ROPE_DIM)
    wqn = w_uq[:, :, :NOPE_DIM].reshape(Q_LORA, MLA_HEADS * NOPE_DIM).astype(BF16)
    wqp = jnp.pad(w_uq[:, :, NOPE_DIM:], ((0, 0), (0, 0), (0, LANES - ROPE_DIM)))
    wqp = wqp.reshape(Q_LORA, MLA_HEADS * LANES).astype(BF16)
    wuk = p["mla_w_uk"][j].reshape(MLA_HEADS // 2, 2 * NOPE_DIM, KV_LORA).astype(BF16)
    w_uv = p["mla_w_uv"][j]
    wuv = jnp.stack([jnp.pad(w_uv[h], ((0, 0), ((h % 2) * V_DIM, (1 - h % 2) * V_DIM))) for h in range(MLA_HEADS)])
    return dict(w_in=w_in, ln_g=p["sgu_ln_g"][j][None], ln_b=p["sgu_ln_b"][j][None], w_s=w_s, b_s=b_s,
                qg=p["mla_q_norm_g"][j][None], kvg=p["mla_kv_norm_g"][j][None], wqn=wqn, wqp=wqp, wuk=wuk,
                wuv=wuv.astype(BF16),
                woa=p["odd_w_out"][j, :SGU_DIM].astype(BF16), wob=p["odd_w_out"][j, SGU_DIM:].astype(BF16))


def _run_trunk(x, pos, past, p, *, batch, seq_len, tm, tq):
    depth = p["mix_pre_g"].shape[0]
    n = batch * seq_len
    x = x.reshape(n, D_MODEL)
    is_sample = past is not None
    gate_len = min(seq_len, SGU_CHUNK)
    cos, sin = _rope_tables(pos, max(1, tm // seq_len))
    st = {k: [] for k in ("sb_k", "sb_v", "conv", "ckv", "kpe", "sgu_v")}
    for layer in range(depth):
        j = layer // 2
        g_pre = p["mix_pre_g"][layer][None]
        if layer % 2 == 0:
            w = _prep_even(p, j)
            conv_prev = past["conv"][j] if is_sample else None
            q, k, v, kb, vb, b_mix, conv_state = _proj_even(x, g_pre, w["w_in"], w["w_conv"], conv_prev,
                                                            batch=batch, seq_len=seq_len, tm=tm)
            if is_sample:
                a_mix = _sb_sample(q, kb, vb, past["sb_k"][j].reshape(batch, -1, SB_WIDTH),
                                   past["sb_v"][j].reshape(batch, -1, SB_WIDTH), batch=batch, dec=seq_len)
            else:
                a_mix = _sb_prompt(q, kb, vb, batch=batch, seq_len=seq_len, tq=tq)
            st["sb_k"].append(k.reshape(batch, seq_len, SB_HEADS, SB_HEAD_DIM))
            st["sb_v"].append(v.reshape(batch, seq_len, SB_HEADS, SB_HEAD_DIM))
            st["conv"].append(conv_state)
        else:
            w = _prep_odd(p, j, gate_len)
            outs = _proj_odd(x, g_pre, w["w_in"], w["ln_g"], w["ln_b"], w["w_s"], w["b_s"], w["qg"], w["kvg"],
                             w["wqn"], w["wqp"], cos, sin, tm=tm, gate_len=gate_len, emit_vn=is_sample)
            if is_sample:
                a_mix, vn, qn, qp, ckv, ckvb, kpe, kpeb = outs
                st["sgu_v"].append(vn.reshape(batch, seq_len, SGU_DIM))
                b_mix = _mla_sample(qn, qp, ckvb, kpeb, past["ckv"][j], past["kpe"][j], w["wuk"], w["wuv"],
                                    batch=batch, dec=seq_len)
            else:
                a_mix, qn, qp, ckv, ckvb, kpe, kpeb = outs
                b_mix = _mla_prompt(qn, qp, ckvb, kpeb, w["wuk"], w["wuv"], batch=batch, seq_len=seq_len,
                                    tq=tq, tk=2 * tq)
            st["ckv"].append(ckv.reshape(batch, seq_len, KV_LORA))
            st["kpe"].append(kpe.reshape(batch, seq_len, ROPE_DIM))
        x = _out_ffn(a_mix, b_mix, x, w["woa"], w["wob"], p["mix_post_g"][layer][None], p["ffn_pre_g"][layer][None],
                     p["ffn_w_up"][layer].astype(BF16), p["ffn_w_down"][layer].astype(BF16),
                     p["ffn_post_g"][layer][None], tm=tm)
    states = {k: jnp.stack(v) for k, v in st.items() if v}
    return x.reshape(batch, seq_len, D_MODEL), states


def kernel(x_prompt, x_sample, cache_sb_k, cache_sb_v, state_conv, cache_mla_ckv, cache_mla_kpe,
           mix_pre_g, mix_post_g, ffn_pre_g, ffn_post_g, even_w_in, even_w_conv, even_w_out,
           odd_w_in, sgu_ln_g, sgu_ln_b, sgu_w_s, sgu_b_s, mla_q_norm_g, mla_kv_norm_g,
           mla_w_uq, mla_w_uk, mla_w_uv, odd_w_out, ffn_w_up, ffn_w_down):
    params = {
        "mix_pre_g": mix_pre_g, "mix_post_g": mix_post_g, "ffn_pre_g": ffn_pre_g, "ffn_post_g": ffn_post_g,
        "even_w_in": even_w_in, "even_w_conv": even_w_conv, "even_w_out": even_w_out,
        "odd_w_in": odd_w_in, "sgu_ln_g": sgu_ln_g, "sgu_ln_b": sgu_ln_b, "sgu_w_s": sgu_w_s,
        "sgu_b_s": sgu_b_s, "mla_q_norm_g": mla_q_norm_g, "mla_kv_norm_g": mla_kv_norm_g,
        "mla_w_uq": mla_w_uq, "mla_w_uk": mla_w_uk, "mla_w_uv": mla_w_uv, "odd_w_out": odd_w_out,
        "ffn_w_up": ffn_w_up, "ffn_w_down": ffn_w_down,
    }
    batch, seq_len, _ = x_prompt.shape
    pos_p = jnp.arange(seq_len, dtype=jnp.int32)
    y_prompt, st_p = _run_trunk(x_prompt, pos_p, None, params, batch=batch, seq_len=seq_len,
                                tm=min(512, seq_len), tq=128)
    dec_batch, dec_seq, _ = x_sample.shape
    past_len = cache_sb_k.shape[2]
    pos_s = past_len + jnp.arange(dec_seq, dtype=jnp.int32)
    past = {"sb_k": cache_sb_k, "sb_v": cache_sb_v, "conv": state_conv, "ckv": cache_mla_ckv, "kpe": cache_mla_kpe}
    y_sample, st_s = _run_trunk(x_sample, pos_s, past, params, batch=dec_batch, seq_len=dec_seq,
                                tm=min(512, dec_batch * dec_seq), tq=128)
    return (y_prompt, y_sample,
            st_p["sb_k"], st_p["sb_v"], st_p["conv"], st_p["ckv"], st_p["kpe"],
            st_s["sb_k"], st_s["sb_v"], st_s["conv"], st_s["ckv"], st_s["kpe"], st_s["sgu_v"])
```

```python
import functools
import math

import numpy as np
import jax
import jax.numpy as jnp
from jax import lax
from jax.experimental import pallas as pl
from jax.experimental.pallas import tpu as pltpu

F32 = jnp.float32
BF16 = jnp.bfloat16

EPS = 1e-6
D_MODEL = 1024
CHUNK = 64
SB_HEADS = 8
SB_HEAD_DIM = 64
SB_WIDTH = SB_HEADS * SB_HEAD_DIM
SB_SCALE = 1.0 / math.sqrt(SB_HEAD_DIM)
CONV_DIM = D_MODEL // 2
CONV_W = 3
SGU_CHUNK = 128
SGU_GROUPS = 4
SGU_DIM = D_MODEL // 2
SGU_GROUP_DIM = SGU_DIM // SGU_GROUPS
MLA_HEADS = 8
Q_LORA = 384
KV_LORA = 256
NOPE_DIM = 64
ROPE_DIM = 32
V_DIM = 64
ROPE_THETA = 10000.0
MLA_SCALE = 1.0 / math.sqrt(NOPE_DIM + ROPE_DIM)
D_FF = 4 * D_MODEL
FF_CHUNK = 1024

LANES = 128
HEAD_PAIRS = SB_HEADS // 2
ODD_IN_PAD = 2 * SGU_DIM + Q_LORA + KV_LORA + LANES
VMEM_LIMIT = 56 * 1024 * 1024
NEG_INF = -1e30
LOG2_E = math.log2(math.e)

ROW_TILE = 512
SB_TQ, SB_TK = 256, 128
MLA_TQ, MLA_TK = 256, 256
MLA_ROW_CHUNK = 64


def _cparams(sem):
    return pltpu.CompilerParams(dimension_semantics=sem, vmem_limit_bytes=VMEM_LIMIT)


def _const_spec(shape):
    nd = len(shape)
    return pl.BlockSpec(shape, lambda *_: (0,) * nd, pipeline_mode=pl.Buffered(1))


def _rms(x, g):
    return x * lax.rsqrt(jnp.mean(x * x, axis=-1, keepdims=True) + EPS) * g


def _dot(a, b):
    return jnp.dot(a, b, preferred_element_type=F32)


def _dot_nt(a, b):
    return lax.dot_general(a, b, (((1,), (1,)), ((), ())), preferred_element_type=F32)


def _log_sigmoid(z):
    neg_abs = lax.bitcast_convert_type(lax.bitcast_convert_type(z, jnp.uint32) | jnp.uint32(0x80000000), F32)
    return jnp.minimum(z, 0.0) - jnp.log(1.0 + jnp.exp(neg_abs))


def _split_hi_lo(x):
    hi = x.astype(BF16)
    lo = (x - hi.astype(F32)).astype(BF16)
    return hi, lo


def _rope_block(x, cos, sin):
    half = ROPE_DIM // 2
    lane = lax.broadcasted_iota(jnp.int32, x.shape, 1)
    partner = jnp.where(lane < half, pltpu.roll(x, LANES - half, 1), pltpu.roll(x, half, 1))
    return x * cos + partner * sin


def _proj_even_body(*refs, tm, seq_len, whole_seqs):
    if whole_seqs:
        (x_ref, g_ref, w_ref, wc_ref, e1_ref, e2_ref,
         q_ref, k_ref, v_ref, kb_ref, vb_ref, gc_ref, ci_ref) = refs
    else:
        (x_ref, g_ref, w_ref, wc_ref,
         q_ref, k_ref, v_ref, kb_ref, vb_ref, gc_ref, cs_ref, tail_ref) = refs
    h = _rms(x_ref[...], g_ref[...]).astype(BF16)

    def proj(j):
        return _dot(h, w_ref[:, j * SB_WIDTH:(j + 1) * SB_WIDTH])

    q_ref[...] = (proj(0) * SB_SCALE).astype(BF16)
    k = proj(1)
    k_ref[...] = k
    kb_ref[...] = k.astype(BF16)
    v = proj(2)
    v_ref[...] = v
    vb_ref[...] = v.astype(BF16)
    g_post = proj(3)
    ci = proj(4) * proj(5)
    r1 = pltpu.roll(ci, 1, 0)
    r2 = pltpu.roll(ci, 2, 0)
    row = lax.broadcasted_iota(jnp.int32, (tm, 1), 0)
    if whole_seqs:
        tpos = row % seq_len
        s1 = jnp.where(tpos < 1, e1_ref[...], r1)
        s2 = jnp.where(tpos < 2, e2_ref[...], r2)
        ci_ref[...] = ci
    else:
        first = (pl.program_id(0) % (seq_len // tm)) == 0

        @pl.when(first)
        def _():
            tail_ref[...] = jnp.zeros_like(tail_ref)

        t1 = tail_ref[7:8, :]
        t2 = tail_ref[6:7, :]
        s1 = jnp.where(row == 0, t1, r1)
        s2 = jnp.where(row == 0, t2, jnp.where(row == 1, t1, r2))
        tail_ref[...] = ci[tm - 8:, :]
        cs_ref[0] = ci[tm - (CONV_W - 1):, :]
    conv = wc_ref[0:1, :] * s2 + wc_ref[1:2, :] * s1 + wc_ref[2:3, :] * ci
    gc_ref[...] = (g_post * conv).astype(BF16)


def _proj_even(x, g, w_in, w_conv, conv_prev, *, batch, seq_len, tm):
    n = x.shape[0]
    whole = tm % seq_len == 0
    grid = (n // tm,)
    row_spec = lambda w: pl.BlockSpec((tm, w), lambda i: (i, 0))
    in_specs = [row_spec(D_MODEL), _const_spec((1, D_MODEL)), _const_spec(w_in.shape), _const_spec(w_conv.shape)]
    args = [x, g, w_in, w_conv]
    out_shape = [jax.ShapeDtypeStruct((n, SB_WIDTH), BF16), jax.ShapeDtypeStruct((n, SB_WIDTH), F32),
                 jax.ShapeDtypeStruct((n, SB_WIDTH), F32), jax.ShapeDtypeStruct((n, SB_WIDTH), BF16),
                 jax.ShapeDtypeStruct((n, SB_WIDTH), BF16), jax.ShapeDtypeStruct((n, CONV_DIM), BF16)]
    out_specs = [row_spec(SB_WIDTH)] * 5 + [row_spec(CONV_DIM)]
    scratch = []
    if whole:
        if conv_prev is None:
            conv_prev = jnp.zeros((batch, CONV_W - 1, CONV_DIM), F32)
        e1 =jnp.zeros((batch, seq_len, CONV_DIM), F32).at[:, 0].set(conv_prev[:, 1])
        e2 = jnp.zeros((batch, seq_len, CONV_DIM), F32).at[:, 0].set(conv_prev[:, 0]).at[:, 1].set(conv_prev[:, 1])
        args += [e1.reshape(n, CONV_DIM), e2.reshape(n, CONV_DIM)]
        in_specs += [row_spec(CONV_DIM), row_spec(CONV_DIM)]
        out_shape.append(jax.ShapeDtypeStruct((n, CONV_DIM), F32))
        out_specs.append(row_spec(CONV_DIM))
    else:
        assert seq_len % tm == 0 and conv_prev is None
        seq_tiles = seq_len // tm
        out_shape.append(jax.ShapeDtypeStruct((batch, CONV_W - 1, CONV_DIM), F32))
        out_specs.append(pl.BlockSpec((1, CONV_W - 1, CONV_DIM), lambda i: (i // seq_tiles, 0, 0)))
        scratch.append(pltpu.VMEM((8, CONV_DIM), F32))
    outs = pl.pallas_call(
        functools.partial(_proj_even_body, tm=tm, seq_len=seq_len, whole_seqs=whole),
        grid=grid, in_specs=in_specs, out_specs=out_specs, out_shape=out_shape, scratch_shapes=scratch,
        compiler_params=_cparams(("arbitrary",)), name="proj_even")(*args)
    q, k, v, kb, vb, gc, last = outs
    if whole:
        last = last.reshape(batch, seq_len, CONV_DIM)[:, seq_len - (CONV_W - 1):]
    return q, k, v, kb, vb, gc, last


def _suffix_rhs(tk):
    j = np.arange(2 * tk)[:, None] % tk
    c = np.arange(2 * tk)[None, :]
    return jnp.asarray(np.where(c < tk, j > c, True), dtype=BF16)


def _sb_prompt_body(q_ref, k_ref, v_ref, r_ref, o_ref, carry_ref, acc_ref, *, tq, tk):
    qi = pl.program_id(1)
    ratio = tq // tk
    klow = lax.broadcasted_iota(jnp.int32, (tk, LANES), 1) < SB_HEAD_DIM
    rhs = r_ref[...]
    carry_ref[...] = jnp.zeros_like(carry_ref)
    acc_ref[...] = jnp.zeros_like(acc_ref)
    qs = [q_ref[:, p * LANES:(p + 1) * LANES] for p in range(HEAD_PAIRS)]

    def block_diag(blk):
        zero = jnp.zeros_like(blk)
        return jnp.concatenate([jnp.where(klow, blk, zero), jnp.where(klow, zero, blk)], axis=0)

    def tile(kj, masked):
        start = pl.multiple_of(kj * tk, tk)
        if masked:
            q_pos = qi * tq + lax.broadcasted_iota(jnp.int32, (tq, tk), 0)
            k_pos = kj * tk + lax.broadcasted_iota(jnp.int32, (tq, tk), 1)
            visible = k_pos < q_pos
        zs = [_dot_nt(qs[p], block_diag(k_ref[pl.ds(start, tk), p * LANES:(p + 1) * LANES]))
              for p in range(HEAD_PAIRS)]
        log_betas, splits = [], []
        for hd in range(SB_HEADS):
            z = zs[hd // 2][:, (hd % 2) * tk:(hd % 2 + 1) * tk]
            log_beta = _log_sigmoid(z)
            log_1m = log_beta - z
            if masked:
                log_1m = jnp.where(visible, log_1m, 0.0)
            hi, lo = _split_hi_lo(log_1m)
            log_betas.append(log_beta)
            splits.append(jnp.concatenate([hi, lo], axis=1))
        sums = [_dot(sp, rhs) for sp in splits]
        ws = []
        for hd in range(SB_HEADS):
            carry = carry_ref[hd]
            w = jnp.exp(log_betas[hd] + sums[hd][:, :tk] + carry)
            if masked:
                w = jnp.where(visible, w, 0.0)
            carry_ref[hd] = carry + sums[hd][:, tk:]
            ws.append(w.astype(BF16))
        for p in range(HEAD_PAIRS):
            vbd = block_diag(v_ref[pl.ds(start, tk), p * LANES:(p + 1) * LANES])
            acc_ref[p] += _dot(jnp.concatenate([ws[2 * p], ws[2 * p + 1]], axis=1), vbd)

    for d in range(ratio):
        tile(qi * ratio + ratio - 1 - d, True)

    def body(it, c):
        tile(qi * ratio - 1 - it, False)
        return c

    lax.fori_loop(0, qi * ratio, body, 0)
    for p in range(HEAD_PAIRS):
        o_ref[:, p * LANES:(p + 1) * LANES] = acc_ref[p].astype(BF16)


def _sb_prompt(q, kb, vb, *, batch, seq_len, tq, tk):
    n = q.shape[0]
    nq = seq_len // tq
    rhs = _suffix_rhs(tk)
    q_spec = pl.BlockSpec((tq, SB_WIDTH), lambda b, i: (b * nq + i, 0))
    kv_spec = pl.BlockSpec((seq_len, SB_WIDTH), lambda b, i: (b, 0))
    return pl.pallas_call(
        functools.partial(_sb_prompt_body, tq=tq, tk=tk),
        grid=(batch, nq),
        in_specs=[q_spec, kv_spec, kv_spec, _const_spec(rhs.shape)],
        out_specs=q_spec,
        out_shape=jax.ShapeDtypeStruct((n, SB_WIDTH), BF16),
        scratch_shapes=[pltpu.VMEM((SB_HEADS, tq, LANES), F32), pltpu.VMEM((HEAD_PAIRS, tq, LANES), F32)],
        compiler_params=_cparams(("arbitrary", "arbitrary")), name="sb_attn_prompt")(q, kb, vb, rhs)


def _suffix_lhs(tk):
    r = np.arange(2 * tk)[:, None]
    c = np.arange(2 * tk)[None, :] % tk
    return jnp.asarray(np.where(r < tk, c > r, True), dtype=BF16)


def _sb_sample_body(q_ref, kn_ref, vn_ref, kc_ref, vc_ref, l_ref, o_ref, qrows_ref, carry_ref, acc_ref,
                    *, dec, kblk, tk):
    j = pl.program_id(1)
    rows = SB_HEADS * dec
    lhs = l_ref[...]

    def tile(kb, vb, visible):
        zt = _dot_nt(kb, qrows_ref[...])
        log_beta = _log_sigmoid(zt)
        log_1m = log_beta - zt
        if visible is not None:
            log_1m = jnp.where(visible, log_1m, 0.0)
        hi, lo = _split_hi_lo(log_1m)
        sums = _dot(lhs, jnp.concatenate([hi, lo], axis=0))
        w = jnp.exp(log_beta + sums[:tk] + carry_ref[...])
        if visible is not None:
            w = jnp.where(visible, w, 0.0)
        acc_ref[...] += _dot(w.T.astype(BF16), vb)
        carry_ref[...] += sums[tk:]

    @pl.when(j == 0)
    def _():
        q = q_ref[...]
        qt = jnp.concatenate([q] * SB_HEADS, axis=0)
        rr = lax.broadcasted_iota(jnp.int32, (rows, SB_WIDTH), 0)
        cc = lax.broadcasted_iota(jnp.int32, (rows, SB_WIDTH), 1)
        qrows_ref[...] = jnp.where(rr // dec == cc // SB_HEAD_DIM, qt, jnp.zeros_like(qt))
        carry_ref[...] = jnp.zeros_like(carry_ref)
        acc_ref[...] = jnp.zeros_like(acc_ref)
        pad = jnp.zeros((tk - dec, SB_WIDTH), BF16)
        key_i = lax.broadcasted_iota(jnp.int32, (tk, rows), 0)
        query_t = lax.broadcasted_iota(jnp.int32, (tk, rows), 1) % dec
        tile(jnp.concatenate([kn_ref[...], pad], axis=0), jnp.concatenate([vn_ref[...], pad], axis=0),
             key_i < query_t)

    def body(it, c):
        start = pl.multiple_of(kblk - tk - it * tk, tk)
        tile(kc_ref[0, pl.ds(start, tk), :].astype(BF16), vc_ref[0, pl.ds(start, tk), :].astype(BF16), None)
        return c

    lax.fori_loop(0, kblk // tk, body, 0)

    @pl.when(j == pl.num_programs(1) - 1)
    def _():
        cc = lax.broadcasted_iota(jnp.int32, (dec, SB_WIDTH), 1)
        out = jnp.zeros((dec, SB_WIDTH), F32)
        for h in range(SB_HEADS):
            out = out + jnp.where(cc // SB_HEAD_DIM == h, acc_ref[h * dec:(h + 1) * dec, :], 0.0)
        o_ref[...] = out.astype(BF16)


def _sb_sample(q, kb, vb, cache_k, cache_v, *, batch, dec, kblk=1024, tk=128):
    n = q.shape[0]
    past = cache_k.shape[1]
    nkb = past // kblk
    rows = SB_HEADS * dec
    lhs = _suffix_lhs(tk)
    new_spec = pl.BlockSpec((dec, SB_WIDTH), lambda b, j: (b, 0))
    cache_spec = pl.BlockSpec((1, kblk, SB_WIDTH), lambda b, j: (b, nkb - 1 - j, 0))
    return pl.pallas_call(
        functools.partial(_sb_sample_body, dec=dec, kblk=kblk, tk=tk),
        grid=(batch, nkb),
        in_specs=[new_spec, new_spec, new_spec, cache_spec, cache_spec, _const_spec(lhs.shape)],
        out_specs=new_spec,
        out_shape=jax.ShapeDtypeStruct((n, SB_WIDTH), BF16),
        scratch_shapes=[pltpu.VMEM((rows, SB_WIDTH), BF16), pltpu.VMEM((tk, rows), F32),
                        pltpu.VMEM((rows, SB_WIDTH), F32)],
        compiler_params=_cparams(("arbitrary", "arbitrary")), name="sb_attn_sample")(
            q, kb, vb, cache_k, cache_v, lhs)


def _proj_odd_body(*refs, tm, gate_len, emit_vn):
    (x_ref, g_ref, w_ref, lng_ref, lnb_ref, ws_ref, bs_ref, qg_ref, kvg_ref, wqn_ref, wqp_ref,
     cos_ref, sin_ref) = refs[:13]
    outs = refs[13:]
    if emit_vn:
        sgu_ref, vn_ref, qn_ref, qp_ref, ckv_ref, ckvb_ref, kpe_ref, kpeb_ref = outs
    else:
        sgu_ref, qn_ref, qp_ref, ckv_ref, ckvb_ref, kpe_ref, kpeb_ref = outs
    h = _rms(x_ref[...], g_ref[...]).astype(BF16)
    o_v, o_q, o_kv, o_pe = SGU_DIM, 2 * SGU_DIM, 2 * SGU_DIM + Q_LORA, 2 * SGU_DIM + Q_LORA + KV_LORA
    u = _dot(h, w_ref[:, :o_v])
    v = _dot(h, w_ref[:, o_v:o_q])
    mu = jnp.mean(v, axis=-1, keepdims=True)
    vc = v - mu
    var = jnp.mean(vc * vc, axis=-1, keepdims=True)
    vn = vc * lax.rsqrt(var + EPS) * lng_ref[...] + lnb_ref[...]
    if emit_vn:
        vn_ref[...] = vn
    vnb = vn.astype(BF16)
    rr = lax.broadcasted_iota(jnp.int32, (SGU_CHUNK, SGU_CHUNK), 0)
    cc = lax.broadcasted_iota(jnp.int32, (SGU_CHUNK, SGU_CHUNK), 1)
    causal = (rr // gate_len == cc // gate_len) & (cc <= rr)
    for g in range(SGU_GROUPS):
        gs = slice(g * SGU_GROUP_DIM, (g + 1) * SGU_GROUP_DIM)
        wg = jnp.where(causal, ws_ref[g], 0.0).astype(BF16)
        for c in range(tm // SGU_CHUNK):
            rs = slice(c * SGU_CHUNK, (c + 1) * SGU_CHUNK)
            s = _dot(wg, vnb[rs, gs]) + bs_ref[:, gs]
            sgu_ref[rs, gs] = (u[rs, gs] * s).astype(BF16)
    cq = _rms(_dot(h, w_ref[:, o_q:o_kv]), qg_ref[...]).astype(BF16)
    qn_ref[...] = _dot(cq, wqn_ref[...]).astype(BF16)
    cos = cos_ref[...]
    sin = sin_ref[...]
    for hd in range(MLA_HEADS):
        qp = _rope_block(_dot(cq, wqp_ref[:, hd * LANES:(hd + 1) * LANES]), cos, sin)
        qp_ref[hd] = qp[:, :ROPE_DIM].astype(BF16)
    ckv = _rms(_dot(h, w_ref[:, o_kv:o_pe]), kvg_ref[...])
    ckv_ref[...] = ckv
    ckvb_ref[...] = ckv.astype(BF16)
    kpe = _rope_block(_dot(h, w_ref[:, o_pe:]), cos, sin)[:, :ROPE_DIM]
    kpe_ref[...] = kpe
    kpeb_ref[...] = kpe.astype(BF16)


def _proj_odd(x, g, w_in, ln_g, ln_b, w_s, b_s, qg, kvg, wqn, wqp, cos, sin, *, tm, gate_len, emit_vn):
    n = x.shape[0]
    pos_tiles = cos.shape[0] // tm
    row_spec = lambda w: pl.BlockSpec((tm, w), lambda i: (i, 0))
    tab_spec = pl.BlockSpec((tm, LANES), lambda i: (i % pos_tiles, 0))
    consts = [g, w_in, ln_g, ln_b, w_s, b_s, qg, kvg, wqn, wqp]
    out_shape = [jax.ShapeDtypeStruct((n, SGU_DIM), BF16)]
    out_specs = [row_spec(SGU_DIM)]
    if emit_vn:
        out_shape.append(jax.ShapeDtypeStruct((n, SGU_DIM), F32))
        out_specs.append(row_spec(SGU_DIM))
    out_shape += [jax.ShapeDtypeStruct((n, MLA_HEADS * NOPE_DIM), BF16),
                  jax.ShapeDtypeStruct((MLA_HEADS, n, ROPE_DIM), BF16),
                  jax.ShapeDtypeStruct((n, KV_LORA), F32), jax.ShapeDtypeStruct((n, KV_LORA), BF16),
                  jax.ShapeDtypeStruct((n, ROPE_DIM), F32), jax.ShapeDtypeStruct((n, ROPE_DIM), BF16)]
    out_specs += [row_spec(MLA_HEADS * NOPE_DIM), pl.BlockSpec((MLA_HEADS, tm, ROPE_DIM), lambda i: (0, i, 0)),
                  row_spec(KV_LORA), row_spec(KV_LORA), row_spec(ROPE_DIM), row_spec(ROPE_DIM)]
    return pl.pallas_call(
        functools.partial(_proj_odd_body, tm=tm, gate_len=gate_len, emit_vn=emit_vn),
        grid=(n // tm,),
        in_specs=[row_spec(D_MODEL)] + [_const_spec(c.shape) for c in consts] + [tab_spec, tab_spec],
        out_specs=out_specs, out_shape=out_shape,
        compiler_params=_cparams(("arbitrary",)), name="proj_odd")(x, *consts, cos, sin)


def _mla_queries(qn, wuk_ref, qlat_ref, tq):
    lane = lax.broadcasted_iota(jnp.int32, (tq, LANES), 1)
    low = lane < NOPE_DIM
    for hd in range(MLA_HEADS):
        p = hd // 2
        pair = qn[:, p * LANES:(p + 1) * LANES]
        qm = jnp.where(low if hd % 2 == 0 else jnp.logical_not(low), pair, jnp.zeros_like(pair))
        qlat_ref[hd * tq:(hd + 1) * tq, :] = _dot(qm, wuk_ref[p]).astype(BF16)


def _lane_tile(x, width):
    return jnp.concatenate([x] * (width // LANES), axis=1)


def _mla_tile(qlat_ref, qpe, ck, kp, sc, visible_fn):
    s_ref, p_ref, m_ref, l_ref, acc_ref = sc
    rows, tk = s_ref.shape
    s_ref[...] = _dot_nt(qlat_ref[...], ck) + _dot_nt(qpe, kp)
    for c in range(rows // MLA_ROW_CHUNK):
        rs = slice(c * MLA_ROW_CHUNK, (c + 1) * MLA_ROW_CHUNK)
        s = s_ref[rs, :] * (MLA_SCALE * LOG2_E)
        if visible_fn is not None:
            s = jnp.where(visible_fn(c), s, NEG_INF)
        m_prev = m_ref[rs, :]
        m_new = jnp.maximum(m_prev, jnp.max(s, axis=-1, keepdims=True))
        alpha = jnp.exp2(m_prev - m_new)
        p = jnp.exp2(s - _lane_tile(m_new, tk))
        l_ref[rs, :] = alpha * l_ref[rs, :] + jnp.sum(p, axis=-1, keepdims=True)
        m_ref[rs, :] = m_new
        p_ref[rs, :] = p.astype(BF16)
        acc_ref[rs, :] = acc_ref[rs, :] * _lane_tile(alpha, KV_LORA)
    acc_ref[...] += _dot(p_ref[...], ck)


def _mla_finish(sc, wuv_ref, o_ref, tq):
    _, _, _, l_ref, acc_ref = sc
    o_lat = (acc_ref[...] / _lane_tile(l_ref[...], KV_LORA)).astype(BF16)
    for p in range(MLA_HEADS // 2):
        h0, h1 = 2 * p, 2 * p + 1
        o_ref[:, p * LANES:(p + 1) * LANES] = (
            _dot(o_lat[h0 * tq:(h0 + 1) * tq], wuv_ref[h0]) + _dot(o_lat[h1 * tq:(h1 + 1) * tq], wuv_ref[h1])
        ).astype(BF16)


def _mla_init(sc):
    _, _, m_ref, l_ref, acc_ref = sc
    m_ref[...] = jnp.full_like(m_ref, -jnp.inf)
    l_ref[...] = jnp.zeros_like(l_ref)
    acc_ref[...] = jnp.zeros_like(acc_ref)


def _mla_prompt_body(qn_ref, qp_ref, ckv_ref, kpe_ref, wuk_ref, wuv_ref, o_ref, qlat_ref, *sc, tq, tk):
    qi = pl.program_id(1)
    rows = MLA_HEADS * tq
    _mla_queries(qn_ref[...], wuk_ref, qlat_ref, tq)
    qpe = qp_ref[...].reshape(rows, ROPE_DIM)
    _mla_init(sc)

    def tile(kj, visible_fn):
        start = pl.multiple_of(kj * tk, tk)
        _mla_tile(qlat_ref, qpe, ckv_ref[pl.ds(start, tk), :], kpe_ref[pl.ds(start, tk), :], sc, visible_fn)

    n_full = (qi * tq) // tk

    def body(kj, c):
        tile(kj, None)
        return c

    lax.fori_loop(0, n_full, body, 0)

    def visible(c):
        row = c * MLA_ROW_CHUNK + lax.broadcasted_iota(jnp.int32, (MLA_ROW_CHUNK, tk), 0)
        q_pos = qi * tq + row % tq
        k_pos = n_full * tk + lax.broadcasted_iota(jnp.int32, (MLA_ROW_CHUNK, tk), 1)
        return k_pos // CHUNK <= q_pos // CHUNK

    tile(n_full, visible)
    _mla_finish(sc, wuv_ref, o_ref, tq)


def _mla_scratch(rows, tk):
    return [pltpu.VMEM((rows, KV_LORA), BF16), pltpu.VMEM((rows, tk), F32), pltpu.VMEM((rows, tk), BF16),
            pltpu.VMEM((rows, LANES), F32), pltpu.VMEM((rows, LANES), F32), pltpu.VMEM((rows, KV_LORA), F32)]


def _mla_prompt(qn, qp, ckvb, kpeb, wuk, wuv, *, batch, seq_len, tq, tk):
    n = qn.shape[0]
    nq = seq_len // tq
    width = MLA_HEADS * V_DIM
    return pl.pallas_call(
        functools.partial(_mla_prompt_body, tq=tq, tk=tk),
        grid=(batch, nq),
        in_specs=[pl.BlockSpec((tq, MLA_HEADS * NOPE_DIM), lambda b, i: (b * nq + i, 0)),
                  pl.BlockSpec((MLA_HEADS, tq, ROPE_DIM), lambda b, i: (0, b * nq + i, 0)),
                  pl.BlockSpec((seq_len, KV_LORA), lambda b, i: (b, 0)),
                  pl.BlockSpec((seq_len, ROPE_DIM), lambda b, i: (b, 0)),
                  _const_spec(wuk.shape), _const_spec(wuv.shape)],
        out_specs=pl.BlockSpec((tq, width), lambda b, i: (b * nq + i, 0)),
        out_shape=jax.ShapeDtypeStruct((n, width), BF16),
        scratch_shapes=_mla_scratch(MLA_HEADS * tq, tk),
        compiler_params=_cparams(("arbitrary", "arbitrary")), name="mla_attn_prompt")(qn, qp, ckvb, kpeb, wuk, wuv)


def _mla_sample_body(qn_ref, qp_ref, cn_ref, pn_ref, cc_ref, pc_ref, wuk_ref, wuv_ref, o_ref, qlat_ref, *sc,
                     dec, past, kblk, tk):
    j = pl.program_id(1)
    rows = MLA_HEADS * dec
    qpe = qp_ref[...].reshape(rows, ROPE_DIM)

    @pl.when(j == 0)
    def _():
        _mla_queries(qn_ref[...], wuk_ref, qlat_ref, dec)
        _mla_init(sc)
        ck = jnp.concatenate([cn_ref[...], jnp.zeros((tk - dec, KV_LORA), BF16)], axis=0)
        kp = jnp.concatenate([pn_ref[...], jnp.zeros((tk - dec, ROPE_DIM), BF16)], axis=0)

        def visible(c):
            col = lax.broadcasted_iota(jnp.int32, (MLA_ROW_CHUNK, tk), 1)
            row = c * MLA_ROW_CHUNK + lax.broadcasted_iota(jnp.int32, (MLA_ROW_CHUNK, tk), 0)
            return (col < dec) & ((past + col) // CHUNK <= (past + row % dec) // CHUNK)

        _mla_tile(qlat_ref, qpe, ck, kp, sc, visible)

    def body(it, c):
        start = pl.multiple_of(it * tk, tk)
        _mla_tile(qlat_ref, qpe, cc_ref[0, pl.ds(start, tk), :].astype(BF16),
                  pc_ref[0, pl.ds(start, tk), :].astype(BF16), sc, None)
        return c

    lax.fori_loop(0, kblk // tk, body, 0)

    @pl.when(j == pl.num_programs(1) - 1)
    def _():
        _mla_finish(sc, wuv_ref, o_ref, dec)


def _mla_sample(qn, qp, ckvb, kpeb, cache_ckv, cache_kpe, wuk, wuv, *, batch, dec, kblk=1024, tk=256):
    n = qn.shape[0]
    past = cache_ckv.shape[1]
    width = MLA_HEADS * V_DIM
    return pl.pallas_call(
        functools.partial(_mla_sample_body, dec=dec, past=past, kblk=kblk, tk=tk),
        grid=(batch, past // kblk),
        in_specs=[pl.BlockSpec((dec, MLA_HEADS * NOPE_DIM), lambda b, j: (b, 0)),
                  pl.BlockSpec((MLA_HEADS, dec, ROPE_DIM), lambda b, j: (0, b, 0)),
                  pl.BlockSpec((dec, KV_LORA), lambda b, j: (b, 0)),
                  pl.BlockSpec((dec, ROPE_DIM), lambda b, j: (b, 0)),
                  pl.BlockSpec((1, kblk, KV_LORA), lambda b, j: (b, j, 0)),
                  pl.BlockSpec((1, kblk, ROPE_DIM), lambda b, j: (b, j, 0)),
                  _const_spec(wuk.shape), _const_spec(wuv.shape)],
        out_specs=pl.BlockSpec((dec, width), lambda b, j: (b, 0)),
        out_shape=jax.ShapeDtypeStruct((n, width), BF16),
        scratch_shapes=_mla_scratch(MLA_HEADS * dec, tk),
        compiler_params=_cparams(("arbitrary", "arbitrary")), name="mla_attn_sample")(
            qn, qp, ckvb, kpeb, cache_ckv, cache_kpe, wuk, wuv)


def _out_ffn_body(a_ref, b_ref, x_ref, woa_ref, wob_ref, gpost_ref, gpre_ref, wup_ref, wdn_ref, gfpost_ref, o_ref):
    mixed = _dot(a_ref[...], woa_ref[...]) + _dot(b_ref[...], wob_ref[...])
    x1 = x_ref[...] + _rms(mixed, gpost_ref[...])
    h = _rms(x1, gpre_ref[...]).astype(BF16)
    down = jnp.zeros_like(x1)
    for c in range(D_FF // FF_CHUNK):
        up = _dot(h, wup_ref[:, c * FF_CHUNK:(c + 1) * FF_CHUNK])
        act = jnp.square(jnp.maximum(up, 0.0)).astype(BF16)
        down = down + _dot(act, wdn_ref[c * FF_CHUNK:(c + 1) * FF_CHUNK, :])
    o_ref[...] = x1 + _rms(down, gfpost_ref[...])


def _out_ffn(a, b, x, woa, wob, g_post, g_pre, w_up, w_down, g_fpost, *, tm):
    n = x.shape[0]
    row_spec = lambda w: pl.BlockSpec((tm, w), lambda i: (i, 0))
    consts = [woa, wob, g_post, g_pre, w_up, w_down, g_fpost]
    return pl.pallas_call(
        _out_ffn_body,
        grid=(n // tm,),
        in_specs=[row_spec(a.shape[1]), row_spec(b.shape[1]), row_spec(D_MODEL)] + [_const_spec(c.shape) for c in consts],
        out_specs=row_spec(D_MODEL),
        out_shape=jax.ShapeDtypeStruct((n, D_MODEL), F32),
        compiler_params=_cparams(("arbitrary",)), name="out_ffn")(a, b, x, *consts)


def _rope_tables(pos, reps):
    half = ROPE_DIM // 2
    inv = ROPE_THETA ** (-jnp.arange(half, dtype=F32) / half)
    ang = pos.astype(F32)[:, None] * inv[None, :]
    zeros = jnp.zeros((pos.shape[0], LANES - ROPE_DIM), F32)
    cos = jnp.concatenate([jnp.cos(ang), jnp.cos(ang), zeros], axis=1)
    sin = jnp.concatenate([-jnp.sin(ang), jnp.sin(ang), zeros], axis=1)
    return jnp.tile(cos, (reps, 1)), jnp.tile(sin, (reps, 1))


def _prep_even(p, j):
    return dict(w_in=p["even_w_in"][j].astype(BF16), w_conv=p["even_w_conv"][j],
                woa=p["even_w_out"][j, :SB_WIDTH].astype(BF16), wob=p["even_w_out"][j, SB_WIDTH:].astype(BF16))


def _prep_odd(p, j, gate_len):
    w_in = p["odd_w_in"][j]
    w_in = jnp.pad(w_in, ((0, 0), (0, ODD_IN_PAD - w_in.shape[1]))).astype(BF16)
    reps = SGU_CHUNK // gate_len
    w_s = jnp.tile(p["sgu_w_s"][j, :, :gate_len, :gate_len], (1, reps, reps))
    b_s = jnp.tile(p["sgu_b_s"][j, :, :gate_len], (1, reps))
    b_s = jnp.repeat(b_s.T, SGU_GROUP_DIM, axis=1)
    w_uq = p["mla_w_uq"][j].reshape(Q_LORA, MLA_HEADS, NOPE_DIM + ROPE_DIM)
    wqn = w_uq[:, :, :NOPE_DIM].reshape(Q_LORA, MLA_HEADS * NOPE_DIM).astype(BF16)
    wqp = jnp.pad(w_uq[:, :, NOPE_DIM:], ((0, 0), (0, 0), (0, LANES - ROPE_DIM)))
    wqp = wqp.reshape(Q_LORA, MLA_HEADS * LANES).astype(BF16)
    wuk = p["mla_w_uk"][j].reshape(MLA_HEADS // 2, 2 * NOPE_DIM, KV_LORA).astype(BF16)
    w_uv = p["mla_w_uv"][j]
    wuv = jnp.stack([jnp.pad(w_uv[h], ((0, 0), ((h % 2) * V_DIM, (1 - h % 2) * V_DIM))) for h in range(MLA_HEADS)])
    return dict(w_in=w_in, ln_g=p["sgu_ln_g"][j][None], ln_b=p["sgu_ln_b"][j][None], w_s=w_s, b_s=b_s,
                qg=p["mla_q_norm_g"][j][None], kvg=p["mla_kv_norm_g"][j][None], wqn=wqn, wqp=wqp, wuk=wuk,
                wuv=wuv.astype(BF16),
                woa=p["odd_w_out"][j, :SGU_DIM].astype(BF16), wob=p["odd_w_out"][j, SGU_DIM:].astype(BF16))


def _run_trunk(x, pos, past, p, *, batch, seq_len):
    depth = p["mix_pre_g"].shape[0]
    n = batch * seq_len
    x = x.reshape(n, D_MODEL)
    is_sample = past is not None
    tm = min(ROW_TILE, n)
    assert n % tm == 0 and (tm % seq_len == 0 or seq_len % tm == 0) and tm % SGU_CHUNK == 0
    gate_len = min(seq_len, SGU_CHUNK)
    cos, sin = _rope_tables(pos, max(1, tm // seq_len))
    st = {k: [] for k in ("sb_k", "sb_v", "conv", "ckv", "kpe", "sgu_v")}
    for layer in range(depth):
        j = layer // 2
        g_pre = p["mix_pre_g"][layer][None]
        if layer % 2 == 0:
            w = _prep_even(p, j)
            conv_prev = past["conv"][j] if is_sample else None
            q, k, v, kb, vb, b_mix, conv_state = _proj_even(x, g_pre, w["w_in"], w["w_conv"], conv_prev,
                                                            batch=batch, seq_len=seq_len, tm=tm)
            if is_sample:
                a_mix = _sb_sample(q, kb, vb, past["sb_k"][j].reshape(batch, -1, SB_WIDTH),
                                   past["sb_v"][j].reshape(batch, -1, SB_WIDTH), batch=batch, dec=seq_len)
            else:
                a_mix = _sb_prompt(q, kb, vb, batch=batch, seq_len=seq_len, tq=SB_TQ, tk=SB_TK)
            st["sb_k"].append(k.reshape(batch, seq_len, SB_HEADS, SB_HEAD_DIM))
            st["sb_v"].append(v.reshape(batch, seq_len, SB_HEADS, SB_HEAD_DIM))
            st["conv"].append(conv_state)
        else:
            w = _prep_odd(p, j, gate_len)
            outs = _proj_odd(x, g_pre, w["w_in"], w["ln_g"], w["ln_b"], w["w_s"], w["b_s"], w["qg"], w["kvg"],
                             w["wqn"], w["wqp"], cos, sin, tm=tm, gate_len=gate_len, emit_vn=is_sample)
            if is_sample:
                a_mix, vn, qn, qp, ckv, ckvb, kpe, kpeb = outs
                st["sgu_v"].append(vn.reshape(batch, seq_len, SGU_DIM))
                b_mix = _mla_sample(qn, qp, ckvb, kpeb, past["ckv"][j], past["kpe"][j], w["wuk"], w["wuv"],
                                    batch=batch, dec=seq_len)
            else:
                a_mix, qn, qp, ckv, ckvb, kpe, kpeb = outs
                b_mix = _mla_prompt(qn, qp, ckvb, kpeb, w["wuk"], w["wuv"], batch=batch, seq_len=seq_len,
                                    tq=MLA_TQ, tk=MLA_TK)
            st["ckv"].append(ckv.reshape(batch, seq_len, KV_LORA))
            st["kpe"].append(kpe.reshape(batch, seq_len, ROPE_DIM))
        x = _out_ffn(a_mix, b_mix, x, w["woa"], w["wob"], p["mix_post_g"][layer][None], p["ffn_pre_g"][layer][None],
                     p["ffn_w_up"][layer].astype(BF16), p["ffn_w_down"][layer].astype(BF16),
                     p["ffn_post_g"][layer][None], tm=tm)
    states = {k: jnp.stack(v) for k, v in st.items() if v}
    return x.reshape(batch, seq_len, D_MODEL), states


def kernel(x_prompt, x_sample, cache_sb_k, cache_sb_v, state_conv, cache_mla_ckv, cache_mla_kpe,
           mix_pre_g, mix_post_g, ffn_pre_g, ffn_post_g, even_w_in, even_w_conv, even_w_out,
           odd_w_in, sgu_ln_g, sgu_ln_b, sgu_w_s, sgu_b_s, mla_q_norm_g, mla_kv_norm_g,
           mla_w_uq, mla_w_uk, mla_w_uv, odd_w_out, ffn_w_up, ffn_w_down):
    params = {
        "mix_pre_g": mix_pre_g, "mix_post_g": mix_post_g, "ffn_pre_g": ffn_pre_g, "ffn_post_g": ffn_post_g,
        "even_w_in": even_w_in, "even_w_conv": even_w_conv, "even_w_out": even_w_out,
        "odd_w_in": odd_w_in, "sgu_ln_g": sgu_ln_g, "sgu_ln_b": sgu_ln_b, "sgu_w_s": sgu_w_s,
        "sgu_b_s": sgu_b_s, "mla_q_norm_g": mla_q_norm_g, "mla_kv_norm_g": mla_kv_norm_g,
        "mla_w_uq": mla_w_uq, "mla_w_uk": mla_w_uk, "mla_w_uv": mla_w_uv, "odd_w_out": odd_w_out,
        "ffn_w_up": ffn_w_up, "ffn_w_down": ffn_w_down,
    }
    batch, seq_len, _ = x_prompt.shape
    pos_p = jnp.arange(seq_len, dtype=jnp.int32)
    y_prompt, st_p = _run_trunk(x_prompt, pos_p, None, params, batch=batch, seq_len=seq_len)
    dec_batch, dec_seq, _ = x_sample.shape
    past_len = cache_sb_k.shape[2]
    pos_s = past_len + jnp.arange(dec_seq, dtype=jnp.int32)
    past = {"sb_k": cache_sb_k, "sb_v": cache_sb_v, "conv": state_conv, "ckv": cache_mla_ckv, "kpe": cache_mla_kpe}
    y_sample, st_s = _run_trunk(x_sample, pos_s, past, params, batch=dec_batch, seq_len=dec_seq)
    return (y_prompt, y_sample,
            st_p["sb_k"], st_p["sb_v"], st_p["conv"], st_p["ckv"], st_p["kpe"],
            st_s["sb_k"], st_s["sb_v"], st_s["conv"], st_s["ckv"], st_s["kpe"], st_s["sgu_v"])
```

```python
import functools
import math

import numpy as np
import jax
import jax.numpy as jnp
from jax import lax
from jax.experimental import pallas as pl
from jax.experimental.pallas import tpu as pltpu

F32 = jnp.float32
BF16 = jnp.bfloat16

EPS = 1e-6
D_MODEL = 1024
CHUNK = 64
SB_HEADS = 8
SB_HEAD_DIM = 64
SB_WIDTH = SB_HEADS * SB_HEAD_DIM
SB_SCALE = 1.0 / math.sqrt(SB_HEAD_DIM)
CONV_DIM = D_MODEL // 2
CONV_W = 3
SGU_CHUNK = 128
SGU_GROUPS = 4
SGU_DIM = D_MODEL // 2
SGU_GROUP_DIM = SGU_DIM // SGU_GROUPS
MLA_HEADS = 8
Q_LORA = 384
KV_LORA = 256
NOPE_DIM = 64
ROPE_DIM = 32
V_DIM = 64
ROPE_THETA = 10000.0
MLA_SCALE = 1.0 / math.sqrt(NOPE_DIM + ROPE_DIM)
D_FF = 4 * D_MODEL
FF_CHUNK = 1024

LANES = 128
HEAD_PAIRS = SB_HEADS // 2
MLA_QK = 2 * KV_LORA
ODD_IN_PAD =2 * SGU_DIM + Q_LORA + KV_LORA + LANES
VMEM_LIMIT = 56 * 1024 * 1024
NEG_INF = -1e30
LOG2_E = math.log2(math.e)

ROW_TILE = 512
SB_TQ, SB_TK = 256, 128
MLA_TQ, MLA_TK = 256, 256
MLA_ROW_CHUNK = 64
MLA_ROW_GROUPS = 1


def _cparams(sem):
    return pltpu.CompilerParams(dimension_semantics=sem, vmem_limit_bytes=VMEM_LIMIT)


def _const_spec(shape):
    nd = len(shape)
    return pl.BlockSpec(shape, lambda *_: (0,) * nd, pipeline_mode=pl.Buffered(1))


def _rms(x, g):
    return x * lax.rsqrt(jnp.mean(x * x, axis=-1, keepdims=True) + EPS) * g


def _dot(a, b):
    return jnp.dot(a, b, preferred_element_type=F32)


def _dot_nt(a, b):
    return lax.dot_general(a, b, (((1,), (1,)), ((), ())), preferred_element_type=F32)


def _log_sigmoid(z):
    neg_abs = lax.bitcast_convert_type(lax.bitcast_convert_type(z, jnp.uint32) | jnp.uint32(0x80000000), F32)
    return jnp.minimum(z, 0.0) - jnp.log(1.0 + jnp.exp(neg_abs))


def _split_hi_lo(x):
    hi = x.astype(BF16)
    lo = (x - hi.astype(F32)).astype(BF16)
    return hi, lo


def _rope_block(x, cos, sin):
    half = ROPE_DIM // 2
    lane = lax.broadcasted_iota(jnp.int32, x.shape, 1)
    partner = jnp.where(lane < half, pltpu.roll(x, LANES - half, 1), pltpu.roll(x, half, 1))
    return x * cos + partner * sin


def _proj_even_body(*refs, tm, seq_len, whole_seqs):
    if whole_seqs:
        (x_ref, g_ref, w_ref, wc_ref, e1_ref, e2_ref,
         q_ref, k_ref, v_ref, kb_ref, vb_ref, gc_ref, ci_ref) = refs
    else:
        (x_ref, g_ref, w_ref, wc_ref,
         q_ref, k_ref, v_ref, kb_ref, vb_ref, gc_ref, cs_ref, tail_ref) = refs
    h = _rms(x_ref[...], g_ref[...]).astype(BF16)

    def proj(j):
        return _dot(h, w_ref[:, j * SB_WIDTH:(j + 1) * SB_WIDTH])

    q_ref[...] = (proj(0) * SB_SCALE).astype(BF16)
    k = proj(1)
    k_ref[...] = k
    kb_ref[...] = k.astype(BF16)
    v = proj(2)
    v_ref[...] = v
    vb_ref[...] = v.astype(BF16)
    g_post = proj(3)
    ci = proj(4) * proj(5)
    r1 = pltpu.roll(ci, 1, 0)
    r2 = pltpu.roll(ci, 2, 0)
    row = lax.broadcasted_iota(jnp.int32, (tm, 1), 0)
    if whole_seqs:
        tpos = row % seq_len
        s1 = jnp.where(tpos < 1, e1_ref[...], r1)
        s2 = jnp.where(tpos < 2, e2_ref[...], r2)
        ci_ref[...] = ci
    else:
        first = (pl.program_id(0) % (seq_len // tm)) == 0

        @pl.when(first)
        def _():
            tail_ref[...] = jnp.zeros_like(tail_ref)

        t1 = tail_ref[7:8, :]
        t2 = tail_ref[6:7, :]
        s1 = jnp.where(row == 0, t1, r1)
        s2 = jnp.where(row == 0, t2, jnp.where(row == 1, t1, r2))
        tail_ref[...] = ci[tm - 8:, :]
        cs_ref[0] = ci[tm - (CONV_W - 1):, :]
    conv = wc_ref[0:1, :] * s2 + wc_ref[1:2, :] * s1 + wc_ref[2:3, :] * ci
    gc_ref[...] = (g_post * conv).astype(BF16)


def _proj_even(x, g, w_in, w_conv, conv_prev, *, batch, seq_len, tm):
    n = x.shape[0]
    whole = tm % seq_len == 0
    grid = (n // tm,)
    row_spec = lambda w: pl.BlockSpec((tm, w), lambda i: (i, 0))
    in_specs = [row_spec(D_MODEL), _const_spec((1, D_MODEL)), _const_spec(w_in.shape), _const_spec(w_conv.shape)]
    args = [x, g, w_in, w_conv]
    out_shape = [jax.ShapeDtypeStruct((n, SB_WIDTH), BF16), jax.ShapeDtypeStruct((n, SB_WIDTH), F32),
                 jax.ShapeDtypeStruct((n, SB_WIDTH), F32), jax.ShapeDtypeStruct((n, SB_WIDTH), BF16),
                 jax.ShapeDtypeStruct((n, SB_WIDTH), BF16), jax.ShapeDtypeStruct((n, CONV_DIM), BF16)]
    out_specs = [row_spec(SB_WIDTH)] * 5 + [row_spec(CONV_DIM)]
    scratch = []
    if whole:
        if conv_prev is None:
            conv_prev = jnp.zeros((batch, CONV_W - 1, CONV_DIM), F32)
        e1 =jnp.zeros((batch, seq_len, CONV_DIM), F32).at[:, 0].set(conv_prev[:, 1])
        e2 = jnp.zeros((batch, seq_len, CONV_DIM), F32).at[:, 0].set(conv_prev[:, 0]).at[:, 1].set(conv_prev[:, 1])
        args += [e1.reshape(n, CONV_DIM), e2.reshape(n, CONV_DIM)]
        in_specs += [row_spec(CONV_DIM), row_spec(CONV_DIM)]
        out_shape.append(jax.ShapeDtypeStruct((n, CONV_DIM), F32))
        out_specs.append(row_spec(CONV_DIM))
    else:
        assert seq_len % tm == 0 and conv_prev is None
        seq_tiles = seq_len // tm
        out_shape.append(jax.ShapeDtypeStruct((batch, CONV_W - 1, CONV_DIM), F32))
        out_specs.append(pl.BlockSpec((1, CONV_W - 1, CONV_DIM), lambda i: (i // seq_tiles, 0, 0)))
        scratch.append(pltpu.VMEM((8, CONV_DIM), F32))
    outs = pl.pallas_call(
        functools.partial(_proj_even_body, tm=tm, seq_len=seq_len, whole_seqs=whole),
        grid=grid, in_specs=in_specs, out_specs=out_specs, out_shape=out_shape, scratch_shapes=scratch,
        compiler_params=_cparams(("arbitrary",)), name="proj_even")(*args)
    q, k, v, kb, vb, gc, last = outs
    if whole:
        last = last.reshape(batch, seq_len, CONV_DIM)[:, seq_len - (CONV_W - 1):]
    return q, k, v, kb, vb, gc, last


def _suffix_rhs(tk):
    j = np.arange(2 * tk)[:, None] % tk
    c = np.arange(2 * tk)[None, :]
    return jnp.asarray(np.where(c < tk, j > c, True), dtype=BF16)


def _sb_prompt_body(q_ref, k_ref, v_ref, r_ref, o_ref, carry_ref, acc_ref, *, tq, tk):
    qi = pl.program_id(1)
    ratio = tq // tk
    klow = lax.broadcasted_iota(jnp.int32, (tk, LANES), 1) < SB_HEAD_DIM
    rhs = r_ref[...]
    carry_ref[...] = jnp.zeros_like(carry_ref)
    acc_ref[...] = jnp.zeros_like(acc_ref)
    qs = [q_ref[:, p * LANES:(p + 1) * LANES] for p in range(HEAD_PAIRS)]

    def block_diag(blk):
        zero = jnp.zeros_like(blk)
        return jnp.concatenate([jnp.where(klow, blk, zero), jnp.where(klow, zero, blk)], axis=0)

    def tile(kj, masked):
        start = pl.multiple_of(kj * tk, tk)
        if masked:
            q_pos = qi * tq + lax.broadcasted_iota(jnp.int32, (tq, tk), 0)
            k_pos = kj * tk + lax.broadcasted_iota(jnp.int32, (tq, tk), 1)
            visible = k_pos < q_pos
        zs = [_dot_nt(qs[p], block_diag(k_ref[pl.ds(start, tk), p * LANES:(p + 1) * LANES]))
              for p in range(HEAD_PAIRS)]
        log_betas, splits = [], []
        for hd in range(SB_HEADS):
            z = zs[hd // 2][:, (hd % 2) * tk:(hd % 2 + 1) * tk]
            log_beta = _log_sigmoid(z)
            log_1m = log_beta - z
            if masked:
                log_1m = jnp.where(visible, log_1m, 0.0)
            hi, lo = _split_hi_lo(log_1m)
            log_betas.append(log_beta)
            splits.append(jnp.concatenate([hi, lo], axis=1))
        sums = [_dot(sp, rhs) for sp in splits]
        ws = []
        for hd in range(SB_HEADS):
            carry = carry_ref[hd]
            w = jnp.exp(log_betas[hd] + sums[hd][:, :tk] + carry)
            if masked:
                w = jnp.where(visible, w, 0.0)
            carry_ref[hd] = carry + sums[hd][:, tk:]
            ws.append(w.astype(BF16))
        for p in range(HEAD_PAIRS):
            vbd = block_diag(v_ref[pl.ds(start, tk), p * LANES:(p + 1) * LANES])
            acc_ref[p] += _dot(jnp.concatenate([ws[2 * p], ws[2 * p + 1]], axis=1), vbd)

    for d in range(ratio):
        tile(qi * ratio + ratio - 1 - d, True)

    def body(it, c):
        tile(qi * ratio - 1 - 2 * it, False)
        tile(qi * ratio - 2 - 2 * it, False)
        return c

    lax.fori_loop(0, qi * (ratio // 2), body, 0)
    for p in range(HEAD_PAIRS):
        o_ref[:, p * LANES:(p + 1) * LANES] = acc_ref[p].astype(BF16)


def _sb_prompt(q, kb, vb, *, batch, seq_len, tq, tk):
    assert (tq // tk) % 2 == 0 and tq % tk == 0 and seq_len % tq == 0
    n = q.shape[0]
    nq = seq_len // tq
    rhs = _suffix_rhs(tk)
    q_spec = pl.BlockSpec((tq, SB_WIDTH), lambda b, i: (b * nq + i, 0))
    kv_spec = pl.BlockSpec((seq_len, SB_WIDTH), lambda b, i: (b, 0))
    return pl.pallas_call(
        functools.partial(_sb_prompt_body, tq=tq, tk=tk),
        grid=(batch, nq),
        in_specs=[q_spec, kv_spec, kv_spec, _const_spec(rhs.shape)],
        out_specs=q_spec,
        out_shape=jax.ShapeDtypeStruct((n, SB_WIDTH), BF16),
        scratch_shapes=[pltpu.VMEM((SB_HEADS, tq, LANES), F32), pltpu.VMEM((HEAD_PAIRS, tq, LANES), F32)],
        compiler_params=_cparams(("arbitrary", "arbitrary")), name="sb_attn_prompt")(q, kb, vb, rhs)


def _suffix_lhs(tk):
    r = np.arange(2 * tk)[:, None]
    c = np.arange(2 * tk)[None, :] % tk
    return jnp.asarray(np.where(r < tk, c > r, True), dtype=BF16)


def _sb_sample_body(q_ref, kn_ref, vn_ref, kc_ref, vc_ref, l_ref, o_ref, qh_ref, carry_ref, acc_ref,
                    *, dec, kblk, tk):
    j = pl.program_id(1)
    rows = SB_HEADS * dec
    lhs = l_ref[...]

    def tile(k_heads, v_heads, visible):
        zt = _dot_nt(k_heads[0], qh_ref[0])
        for h in range(1, SB_HEADS):
            zt = zt + _dot_nt(k_heads[h], qh_ref[h])
        log_beta = _log_sigmoid(zt)
        log_1m = log_beta - zt
        if visible is not None:
            log_1m = jnp.where(visible, log_1m, 0.0)
        hi, lo = _split_hi_lo(log_1m)
        sums = _dot(lhs, jnp.concatenate([hi, lo], axis=0))
        w = jnp.exp(log_beta + sums[:tk] + carry_ref[...])
        if visible is not None:
            w = jnp.where(visible, w, 0.0)
        wt = w.T.astype(BF16)
        for h in range(SB_HEADS):
            acc_ref[h] += _dot(wt, v_heads[h])
        carry_ref[...] += sums[tk:]

    def head_slabs(x):
        x = x.astype(F32)
        pad = jnp.zeros((tk - dec, SB_HEAD_DIM), F32)
        return [jnp.concatenate([x[:, h * SB_HEAD_DIM:(h + 1) * SB_HEAD_DIM], pad], axis=0).astype(BF16)
                for h in range(SB_HEADS)]

    @pl.when(j == 0)
    def _():
        q = q_ref[...].astype(F32)
        for h in range(SB_HEADS):
            parts = [q[:, h * SB_HEAD_DIM:(h + 1) * SB_HEAD_DIM]]
            if h > 0:
                parts.insert(0, jnp.zeros((h * dec, SB_HEAD_DIM), F32))
            if h < SB_HEADS - 1:
                parts.append(jnp.zeros((rows - (h + 1) * dec, SB_HEAD_DIM), F32))
            qh_ref[h] = jnp.concatenate(parts, axis=0).astype(BF16)
        carry_ref[...] = jnp.zeros_like(carry_ref)
        acc_ref[...] = jnp.zeros_like(acc_ref)
        key_i = lax.broadcasted_iota(jnp.int32, (tk, rows), 0)
        query_t = lax.broadcasted_iota(jnp.int32, (tk, rows), 1) % dec
        tile(head_slabs(kn_ref[...]), head_slabs(vn_ref[...]),
             key_i < query_t)

    def body(it, c):
        base = pl.multiple_of((kblk - tk - it * tk) * SB_HEADS, tk * SB_HEADS)
        tile([kc_ref[0, pl.ds(base + h, tk, stride=SB_HEADS), :].astype(BF16) for h in range(SB_HEADS)],
             [vc_ref[0, pl.ds(base + h, tk, stride=SB_HEADS), :].astype(BF16) for h in range(SB_HEADS)], None)
        return c

    lax.fori_loop(0, kblk // tk, body, 0)

    @pl.when(j == pl.num_programs(1) - 1)
    def _():
        o_ref[...] = jnp.concatenate([acc_ref[h, h * dec:(h + 1) * dec, :] for h in range(SB_HEADS)],
                                     axis=1).astype(BF16)


def _sb_sample(q, kb, vb, cache_k, cache_v, *, batch, dec, kblk=1024, tk=256):
    n = q.shape[0]
    past = cache_k.shape[1] // SB_HEADS
    nkb = past // kblk
    rows = SB_HEADS * dec
    lhs = _suffix_lhs(tk)
    new_spec = pl.BlockSpec((dec, SB_WIDTH), lambda b, j: (b, 0))
    cache_spec = pl.BlockSpec((1, kblk * SB_HEADS, SB_HEAD_DIM), lambda b, j: (b, nkb - 1 - j, 0))
    return pl.pallas_call(
        functools.partial(_sb_sample_body, dec=dec, kblk=kblk, tk=tk),
        grid=(batch, nkb),
        in_specs=[new_spec, new_spec, new_spec, cache_spec, cache_spec, _const_spec(lhs.shape)],
        out_specs=new_spec,
        out_shape=jax.ShapeDtypeStruct((n, SB_WIDTH), BF16),
        scratch_shapes=[pltpu.VMEM((SB_HEADS, rows, SB_HEAD_DIM), BF16), pltpu.VMEM((tk, rows), F32),
                        pltpu.VMEM((SB_HEADS, rows, SB_HEAD_DIM), F32)],
        compiler_params=_cparams(("arbitrary", "arbitrary")), name="sb_attn_sample")(
            q, kb, vb, cache_k, cache_v, lhs)


def _proj_odd_body(*refs, tm, gate_len, emit_vn):
    (x_ref, g_ref, w_ref, lng_ref, lnb_ref, ws_ref, bs_ref, qg_ref, kvg_ref, wqn_ref, wqp_ref,
     cos_ref, sin_ref) = refs[:13]
    outs = refs[13:]
    if emit_vn:
        sgu_ref, vn_ref, qn_ref, qp_ref, ckv_ref, ckvb_ref, kpe_ref, kpeb_ref = outs
    else:
        sgu_ref, qn_ref, qp_ref, ckv_ref, ckvb_ref, kpe_ref, kpeb_ref = outs
    h = _rms(x_ref[...], g_ref[...]).astype(BF16)
    o_v, o_q, o_kv, o_pe = SGU_DIM, 2 * SGU_DIM, 2 * SGU_DIM + Q_LORA, 2 * SGU_DIM + Q_LORA + KV_LORA
    u = _dot(h, w_ref[:, :o_v])
    v = _dot(h, w_ref[:, o_v:o_q])
    mu = jnp.mean(v, axis=-1, keepdims=True)
    vc = v - mu
    var = jnp.mean(vc * vc, axis=-1, keepdims=True)
    vn = vc * lax.rsqrt(var + EPS) * lng_ref[...] + lnb_ref[...]
    if emit_vn:
        vn_ref[...] = vn
    vnb = vn.astype(BF16)
    rr = lax.broadcasted_iota(jnp.int32, (SGU_CHUNK, SGU_CHUNK), 0)
    cc = lax.broadcasted_iota(jnp.int32, (SGU_CHUNK, SGU_CHUNK), 1)
    causal = (rr // gate_len == cc // gate_len) & (cc <= rr)
    for g in range(SGU_GROUPS):
        gs = slice(g * SGU_GROUP_DIM, (g + 1) * SGU_GROUP_DIM)
        wg = jnp.where(causal, ws_ref[g], 0.0).astype(BF16)
        for c in range(tm // SGU_CHUNK):
            rs = slice(c * SGU_CHUNK, (c + 1) * SGU_CHUNK)
            s = _dot(wg, vnb[rs, gs]) + bs_ref[:, gs]
            sgu_ref[rs, gs] = (u[rs, gs] * s).astype(BF16)
    cq = _rms(_dot(h, w_ref[:, o_q:o_kv]), qg_ref[...]).astype(BF16)
    qn_ref[...] = _dot(cq, wqn_ref[...]).astype(BF16)
    cos = cos_ref[...]
    sin = sin_ref[...]
    for hd in range(MLA_HEADS):
        qp = _rope_block(_dot(cq, wqp_ref[:, hd * LANES:(hd + 1) * LANES]), cos, sin)
        qp_ref[hd] = qp[:, :ROPE_DIM].astype(BF16)
    ckv = _rms(_dot(h, w_ref[:, o_kv:o_pe]), kvg_ref[...])
    ckv_ref[...] = ckv
    ckvb_ref[...] = ckv.astype(BF16)
    kpe = _rope_block(_dot(h, w_ref[:, o_pe:]), cos, sin)
    kpe_ref[...] = kpe[:, :ROPE_DIM]
    kpeb_ref[...] = kpe.astype(BF16)


def _proj_odd(x, g, w_in, ln_g, ln_b, w_s, b_s, qg, kvg, wqn, wqp, cos, sin, *, tm, gate_len, emit_vn):
    n = x.shape[0]
    pos_tiles = cos.shape[0] // tm
    row_spec = lambda w: pl.BlockSpec((tm, w), lambda i: (i, 0))
    tab_spec = pl.BlockSpec((tm, LANES), lambda i: (i % pos_tiles, 0))
    consts = [g, w_in, ln_g, ln_b, w_s, b_s, qg, kvg, wqn, wqp]
    out_shape = [jax.ShapeDtypeStruct((n, SGU_DIM), BF16)]
    out_specs = [row_spec(SGU_DIM)]
    if emit_vn:
        out_shape.append(jax.ShapeDtypeStruct((n, SGU_DIM), F32))
        out_specs.append(row_spec(SGU_DIM))
    out_shape += [jax.ShapeDtypeStruct((n, MLA_HEADS * NOPE_DIM), BF16),
                  jax.ShapeDtypeStruct((MLA_HEADS, n, ROPE_DIM), BF16),
                  jax.ShapeDtypeStruct((n, KV_LORA), F32), jax.ShapeDtypeStruct((n, KV_LORA), BF16),
                  jax.ShapeDtypeStruct((n, ROPE_DIM), F32), jax.ShapeDtypeStruct((n, LANES), BF16)]
    out_specs += [row_spec(MLA_HEADS * NOPE_DIM), pl.BlockSpec((MLA_HEADS, tm, ROPE_DIM), lambda i: (0, i, 0)),
                  row_spec(KV_LORA), row_spec(KV_LORA), row_spec(ROPE_DIM), row_spec(LANES)]
    return pl.pallas_call(
        functools.partial(_proj_odd_body, tm=tm, gate_len=gate_len, emit_vn=emit_vn),
        grid=(n // tm,),
        in_specs=[row_spec(D_MODEL)] + [_const_spec(c.shape) for c in consts] + [tab_spec, tab_spec],
        out_specs=out_specs, out_shape=out_shape,
        compiler_params=_cparams(("arbitrary",)), name="proj_odd")(x, *consts, cos, sin)


def _mla_queries(qn, qpe, wuk_ref, qcat_ref, tq):
    lane = lax.broadcasted_iota(jnp.int32, (tq, LANES), 1)
    low = lane < NOPE_DIM
    for hd in range(MLA_HEADS):
        p = hd // 2
        pair = qn[:, p * LANES:(p + 1) * LANES]
        qm = jnp.where(low if hd % 2 == 0 else jnp.logical_not(low), pair, jnp.zeros_like(pair))
        qcat_ref[hd * tq:(hd + 1) * tq, :KV_LORA] = _dot(qm, wuk_ref[p]).astype(BF16)
    qcat_ref[:, KV_LORA:] = jnp.zeros((MLA_HEADS * tq, MLA_QK - KV_LORA), BF16)
    qcat_ref[:, KV_LORA:KV_LORA + ROPE_DIM] = qpe


def _lane_tile(x, width):
    return jnp.concatenate([x] * (width // LANES), axis=1)


def _mla_tile(qcat_ref, kcat, sc, visible_fn):
    s_ref, p_ref, m_ref, l_ref, acc_ref = sc
    rows, tk = s_ref.shape
    ck = kcat[:, :KV_LORA]
    grp = rows // MLA_ROW_GROUPS
    for g in range(MLA_ROW_GROUPS):
        gs = slice(g * grp, (g + 1) * grp)
        s_ref[gs, :] = _dot_nt(qcat_ref[gs, :], kcat)
    for g in range(MLA_ROW_GROUPS):
        for c in range(g * grp // MLA_ROW_CHUNK, (g + 1) * grp // MLA_ROW_CHUNK):
            rs = slice(c * MLA_ROW_CHUNK, (c + 1) * MLA_ROW_CHUNK)
            s = s_ref[rs, :] * (MLA_SCALE * LOG2_E)
            if visible_fn is not None:
                s = jnp.where(visible_fn(c), s, NEG_INF)
            m_prev = m_ref[rs, :]
            m_new = jnp.maximum(m_prev, jnp.max(s, axis=-1, keepdims=True))
            alpha = jnp.exp2(m_prev - m_new)
            p = jnp.exp2(s - _lane_tile(m_new, tk))
            l_ref[rs, :] = alpha * l_ref[rs, :] + jnp.sum(p, axis=-1, keepdims=True)
            m_ref[rs, :] = m_new
            p_ref[rs, :] = p.astype(BF16)
            acc_ref[rs, :] = acc_ref[rs, :] * _lane_tile(alpha, KV_LORA)
        gs = slice(g * grp, (g + 1) * grp)
        acc_ref[gs, :] += _dot(p_ref[gs, :], ck)


def _mla_finish(sc, wuv_ref, o_ref, tq):
    l_ref, acc_ref = sc[3], sc[4]
    o_lat = (acc_ref[...] / _lane_tile(l_ref[...], KV_LORA)).astype(BF16)
    for p in range(MLA_HEADS // 2):
        h0, h1 = 2 * p, 2 * p + 1
        o_ref[:, p * LANES:(p + 1) * LANES] = (
            _dot(o_lat[h0 * tq:(h0 + 1) * tq], wuv_ref[h0]) + _dot(o_lat[h1 * tq:(h1 + 1) * tq], wuv_ref[h1])
        ).astype(BF16)


def _mla_init(sc):
    m_ref, l_ref, acc_ref = sc[2], sc[3], sc[4]
    m_ref[...] = jnp.full_like(m_ref, -jnp.inf)
    l_ref[...] = jnp.zeros_like(l_ref)
    acc_ref[...] = jnp.zeros_like(acc_ref)


def _mla_prompt_body(qn_ref, qp_ref, ckv_ref, kpe_ref, wuk_ref, wuv_ref, o_ref, qcat_ref, *sc, tq, tk):
    qi = pl.program_id(1)
    rows = MLA_HEADS * tq
    _mla_queries(qn_ref[...], qp_ref[...].reshape(rows, ROPE_DIM), wuk_ref, qcat_ref, tq)
    _mla_init(sc)

    n_full = (qi * tq) // tk
    col = lax.broadcasted_iota(jnp.int32, (MLA_ROW_CHUNK, tk), 1)

    def visible(c):
        q_chunk_end = qi * tq + ((c * MLA_ROW_CHUNK) % tq) // CHUNK * CHUNK + CHUNK
        return col < q_chunk_end - n_full * tk

    def tile(kj, visible_fn):
        start = pl.multiple_of(kj * tk, tk)
        kcat = jnp.concatenate([ckv_ref[pl.ds(start, tk), :], kpe_ref[pl.ds(start, tk), :],
                                jnp.zeros((tk, MLA_QK - KV_LORA - LANES), BF16)], axis=1)
        _mla_tile(qcat_ref, kcat, sc, visible_fn)

    def body(kj, c):
        tile(kj, None)
        return c

    lax.fori_loop(0, n_full, body, 0)
    tile(n_full, visible)
    _mla_finish(sc, wuv_ref, o_ref, tq)


def _mla_scratch(rows, tk):
    return [pltpu.VMEM((rows, MLA_QK), BF16), pltpu.VMEM((rows, tk), F32), pltpu.VMEM((rows, tk), BF16),
            pltpu.VMEM((rows, LANES), F32), pltpu.VMEM((rows, LANES), F32), pltpu.VMEM((rows, KV_LORA), F32)]


def _mla_prompt(qn, qp, ckvb, kpeb, wuk, wuv, *, batch, seq_len, tq, tk):
    assert tq % CHUNK == 0 and CHUNK % MLA_ROW_CHUNK == 0 and seq_len % tq == 0 and seq_len % tk == 0
    n = qn.shape[0]
    nq = seq_len // tq
    width = MLA_HEADS * V_DIM
    return pl.pallas_call(
        functools.partial(_mla_prompt_body, tq=tq, tk=tk),
        grid=(batch, nq),
        in_specs=[pl.BlockSpec((tq, MLA_HEADS * NOPE_DIM), lambda b, i: (b * nq + i, 0)),
                  pl.BlockSpec((MLA_HEADS, tq, ROPE_DIM), lambda b, i: (0, b * nq + i, 0)),
                  pl.BlockSpec((seq_len, KV_LORA), lambda b, i: (b, 0)),
                  pl.BlockSpec((seq_len, LANES), lambda b, i: (b, 0)),
                  _const_spec(wuk.shape), _const_spec(wuv.shape)],
        out_specs=pl.BlockSpec((tq, width), lambda b, i: (b * nq + i, 0)),
        out_shape=jax.ShapeDtypeStruct((n, width), BF16),
        scratch_shapes=_mla_scratch(MLA_HEADS * tq, tk),
        compiler_params=_cparams(("arbitrary", "arbitrary")), name="mla_attn_prompt")(qn, qp, ckvb, kpeb, wuk, wuv)


def _mla_sample_body(qn_ref, qp_ref, cn_ref, pn_ref, cc_ref, pc_ref, wuk_ref, wuv_ref, o_ref, kcat_ref, qcat_ref,
                     *sc, dec, past, kblk, tk):
    j = pl.program_id(1)
    rows = MLA_HEADS * dec

    @pl.when(j == 0)
    def _():
        _mla_queries(qn_ref[...], qp_ref[...].reshape(rows, ROPE_DIM), wuk_ref, qcat_ref, dec)
        _mla_init(sc)
        kcat_ref[:, KV_LORA:] = jnp.zeros((tk, MLA_QK - KV_LORA), BF16)
        new = jnp.concatenate([cn_ref[...], pn_ref[...], jnp.zeros((dec, MLA_QK - KV_LORA - LANES), BF16)], axis=1)
        kcat = jnp.concatenate([new, jnp.zeros((tk - dec, MLA_QK), BF16)], axis=0)

        def visible(c):
            col = lax.broadcasted_iota(jnp.int32, (MLA_ROW_CHUNK, tk), 1)
            row = c * MLA_ROW_CHUNK + lax.broadcasted_iota(jnp.int32, (MLA_ROW_CHUNK, tk), 0)
            return (col < dec) & ((past + col) // CHUNK <= (past + row % dec) // CHUNK)

        _mla_tile(qcat_ref, kcat, sc, visible)

    def body(it, c):
        start = pl.multiple_of(it * tk, tk)
        kcat_ref[:, :KV_LORA] = cc_ref[0, pl.ds(start, tk), :].astype(BF16)
        kcat_ref[:, KV_LORA:KV_LORA + ROPE_DIM] = pc_ref[0, pl.ds(start, tk), :].astype(BF16)
        _mla_tile(qcat_ref, kcat_ref[...], sc, None)
        return c

    lax.fori_loop(0, kblk // tk, body, 0)

    @pl.when(j == pl.num_programs(1) - 1)
    def _():
        _mla_finish(sc, wuv_ref, o_ref, dec)


def _mla_sample(qn, qp, ckvb, kpeb, cache_ckv, cache_kpe, wuk, wuv, *, batch, dec, kblk=1024, tk=512):
    n = qn.shape[0]
    past = cache_ckv.shape[1]
    width = MLA_HEADS * V_DIM
    return pl.pallas_call(
        functools.partial(_mla_sample_body, dec=dec, past=past, kblk=kblk, tk=tk),
        grid=(batch, past // kblk),
        in_specs=[pl.BlockSpec((dec, MLA_HEADS * NOPE_DIM), lambda b, j: (b, 0)),
                  pl.BlockSpec((MLA_HEADS, dec, ROPE_DIM), lambda b, j: (0, b, 0)),
                  pl.BlockSpec((dec, KV_LORA), lambda b, j: (b, 0)),
                  pl.BlockSpec((dec, LANES), lambda b, j: (b, 0)),
                  pl.BlockSpec((1, kblk, KV_LORA), lambda b, j: (b, j, 0)),
                  pl.BlockSpec((1, kblk, ROPE_DIM), lambda b, j: (b, j, 0)),
                  _const_spec(wuk.shape), _const_spec(wuv.shape)],
        out_specs=pl.BlockSpec((dec, width), lambda b, j: (b, 0)),
        out_shape=jax.ShapeDtypeStruct((n, width), BF16),
        scratch_shapes=[pltpu.VMEM((tk, MLA_QK), BF16)] + _mla_scratch(MLA_HEADS * dec, tk),
        compiler_params=_cparams(("arbitrary", "arbitrary")), name="mla_attn_sample")(
            qn, qp, ckvb, kpeb, cache_ckv, cache_kpe, wuk, wuv)


def _out_ffn_body(a_ref, b_ref, x_ref, woa_ref, wob_ref, gpost_ref, gpre_ref, wup_ref, wdn_ref, gfpost_ref, o_ref):
    mixed = _dot(a_ref[...], woa_ref[...]) + _dot(b_ref[...], wob_ref[...])
    x1 = x_ref[...] + _rms(mixed, gpost_ref[...])
    h = _rms(x1, gpre_ref[...]).astype(BF16)
    down = jnp.zeros_like(x1)
    for c in range(D_FF // FF_CHUNK):
        up = _dot(h, wup_ref[:, c * FF_CHUNK:(c + 1) * FF_CHUNK])
        act = jnp.square(jnp.maximum(up, 0.0)).astype(BF16)
        down = down + _dot(act, wdn_ref[c * FF_CHUNK:(c + 1) * FF_CHUNK, :])
    o_ref[...] = x1 + _rms(down, gfpost_ref[...])


def _out_ffn(a, b, x, woa, wob, g_post, g_pre, w_up, w_down, g_fpost, *, tm):
    n = x.shape[0]
    row_spec = lambda w: pl.BlockSpec((tm, w), lambda i: (i, 0))
    consts = [woa, wob, g_post, g_pre, w_up, w_down, g_fpost]
    return pl.pallas_call(
        _out_ffn_body,
        grid=(n // tm,),
        in_specs=[row_spec(a.shape[1]), row_spec(b.shape[1]), row_spec(D_MODEL)] + [_const_spec(c.shape) for c in consts],
        out_specs=row_spec(D_MODEL),
        out_shape=jax.ShapeDtypeStruct((n, D_MODEL), F32),
        compiler_params=_cparams(("arbitrary",)), name="out_ffn")(a, b, x, *consts)


def _rope_tables(pos, reps):
    half = ROPE_DIM // 2
    inv = ROPE_THETA ** (-jnp.arange(half, dtype=F32) / half)
    ang = pos.astype(F32)[:, None] * inv[None, :]
    zeros = jnp.zeros((pos.shape[0], LANES - ROPE_DIM), F32)
    cos = jnp.concatenate([jnp.cos(ang), jnp.cos(ang), zeros], axis=1)
    sin = jnp.concatenate([-jnp.sin(ang), jnp.sin(ang), zeros], axis=1)
    return jnp.tile(cos, (reps, 1)), jnp.tile(sin, (reps, 1))


def _prep_even(p, j):
    return dict(w_in=p["even_w_in"][j].astype(BF16), w_conv=p["even_w_conv"][j],
                woa=p["even_w_out"][j, :SB_WIDTH].astype(BF16), wob=p["even_w_out"][j, SB_WIDTH:].astype(BF16))


def _prep_odd(p, j, gate_len):
    w_in = p["odd_w_in"][j]
    w_in = jnp.pad(w_in, ((0, 0), (0, ODD_IN_PAD - w_in.shape[1]))).astype(BF16)
    reps = SGU_CHUNK // gate_len
    w_s = jnp.tile(p["sgu_w_s"][j, :, :gate_len, :gate_len], (1, reps, reps))
    b_s = jnp.tile(p["sgu_b_s"][j, :, :gate_len], (1, reps))
    b_s = jnp.repeat(b_s.T, SGU_GROUP_DIM, axis=1)
    w_uq = p["mla_w_uq"][j].reshape(Q_LORA, MLA_HEADS, NOPE_DIM + ROPE_DIM)
    wqn = w_uq[:, :, :NOPE_DIM].reshape(Q_LORA, MLA_HEADS * NOPE_DIM).astype(BF16)
    wqp = jnp.pad(w_uq[:, :, NOPE_DIM:], ((0, 0), (0, 0), (0, LANES - ROPE_DIM)))
    wqp = wqp.reshape(Q_LORA, MLA_HEADS * LANES).astype(BF16)
    wuk = p["mla_w_uk"][j].reshape(MLA_HEADS // 2, 2 * NOPE_DIM, KV_LORA).astype(BF16)
    w_uv = p["mla_w_uv"][j]
    wuv = jnp.stack([jnp.pad(w_uv[h], ((0, 0), ((h % 2) * V_DIM, (1 - h % 2) * V_DIM))) for h in range(MLA_HEADS)])
    return dict(w_in=w_in, ln_g=p["sgu_ln_g"][j][None], ln_b=p["sgu_ln_b"][j][None], w_s=w_s, b_s=b_s,
                qg=p["mla_q_norm_g"][j][None], kvg=p["mla_kv_norm_g"][j][None], wqn=wqn, wqp=wqp, wuk=wuk,
                wuv=wuv.astype(BF16),
                woa=p["odd_w_out"][j, :SGU_DIM].astype(BF16), wob=p["odd_w_out"][j, SGU_DIM:].astype(BF16))


def _run_trunk(x, pos, past, p, *, batch, seq_len):
    depth = p["mix_pre_g"].shape[0]
    n = batch * seq_len
    x = x.reshape(n, D_MODEL)
    is_sample = past is not None
    tm = min(ROW_TILE, n)
    assert n % tm == 0 and (tm % seq_len == 0 or seq_len % tm == 0) and tm % SGU_CHUNK == 0
    gate_len = min(seq_len, SGU_CHUNK)
    cos, sin = _rope_tables(pos, max(1, tm // seq_len))
    st = {k: [] for k in ("sb_k", "sb_v", "conv", "ckv", "kpe", "sgu_v")}
    for layer in range(depth):
        j = layer // 2
        g_pre = p["mix_pre_g"][layer][None]
        if layer % 2 == 0:
            w = _prep_even(p, j)
            conv_prev = past["conv"][j] if is_sample else None
            q, k, v, kb, vb, b_mix, conv_state = _proj_even(x, g_pre, w["w_in"], w["w_conv"], conv_prev,
                                                            batch=batch, seq_len=seq_len, tm=tm)
            if is_sample:
                a_mix = _sb_sample(q, kb, vb, past["sb_k"][j].reshape(batch, -1, SB_HEAD_DIM),
                                   past["sb_v"][j].reshape(batch, -1, SB_HEAD_DIM), batch=batch, dec=seq_len)
            else:
                a_mix = _sb_prompt(q, kb, vb, batch=batch, seq_len=seq_len, tq=SB_TQ, tk=SB_TK)
            st["sb_k"].append(k.reshape(batch, seq_len, SB_HEADS, SB_HEAD_DIM))
            st["sb_v"].append(v.reshape(batch, seq_len, SB_HEADS, SB_HEAD_DIM))
            st["conv"].append(conv_state)
        else:
            w = _prep_odd(p, j, gate_len)
            outs = _proj_odd(x, g_pre, w["w_in"], w["ln_g"], w["ln_b"], w["w_s"], w["b_s"], w["qg"], w["kvg"],
                             w["wqn"], w["wqp"], cos, sin, tm=tm, gate_len=gate_len, emit_vn=is_sample)
            if is_sample:
                a_mix, vn, qn, qp, ckv, ckvb, kpe, kpeb = outs
                st["sgu_v"].append(vn.reshape(batch, seq_len, SGU_DIM))
                b_mix = _mla_sample(qn, qp, ckvb, kpeb, past["ckv"][j], past["kpe"][j], w["wuk"], w["wuv"],
                                    batch=batch, dec=seq_len)
            else:
                a_mix, qn, qp, ckv, ckvb, kpe, kpeb = outs
                b_mix = _mla_prompt(qn, qp, ckvb, kpeb, w["wuk"], w["wuv"], batch=batch, seq_len=seq_len,
                                    tq=MLA_TQ, tk=MLA_TK)
            st["ckv"].append(ckv.reshape(batch, seq_len, KV_LORA))
            st["kpe"].append(kpe.reshape(batch, seq_len, ROPE_DIM))
        x = _out_ffn(a_mix, b_mix, x, w["woa"], w["wob"], p["mix_post_g"][layer][None], p["ffn_pre_g"][layer][None],
                     p["ffn_w_up"][layer].astype(BF16), p["ffn_w_down"][layer].astype(BF16),
                     p["ffn_post_g"][layer][None], tm=tm)
    states = {k: jnp.stack(v) for k, v in st.items() if v}
    return x.reshape(batch, seq_len, D_MODEL), states


def kernel(x_prompt, x_sample, cache_sb_k, cache_sb_v, state_conv, cache_mla_ckv, cache_mla_kpe,
           mix_pre_g, mix_post_g, ffn_pre_g, ffn_post_g, even_w_in, even_w_conv, even_w_out,
           odd_w_in, sgu_ln_g, sgu_ln_b, sgu_w_s, sgu_b_s, mla_q_norm_g, mla_kv_norm_g,
           mla_w_uq, mla_w_uk, mla_w_uv, odd_w_out, ffn_w_up, ffn_w_down):
    params = {
        "mix_pre_g": mix_pre_g, "mix_post_g": mix_post_g, "ffn_pre_g": ffn_pre_g, "ffn_post_g": ffn_post_g,
        "even_w_in": even_w_in, "even_w_conv": even_w_conv, "even_w_out": even_w_out,
        "odd_w_in": odd_w_in, "sgu_ln_g": sgu_ln_g, "sgu_ln_b": sgu_ln_b, "sgu_w_s": sgu_w_s,
        "sgu_b_s": sgu_b_s, "mla_q_norm_g": mla_q_norm_g, "mla_kv_norm_g": mla_kv_norm_g,
        "mla_w_uq": mla_w_uq, "mla_w_uk": mla_w_uk, "mla_w_uv": mla_w_uv, "odd_w_out": odd_w_out,
        "ffn_w_up": ffn_w_up, "ffn_w_down": ffn_w_down,
    }
    batch, seq_len, _ = x_prompt.shape
    pos_p = jnp.arange(seq_len, dtype=jnp.int32)
    y_prompt, st_p = _run_trunk(x_prompt, pos_p, None, params, batch=batch, seq_len=seq_len)
    dec_batch, dec_seq, _ = x_sample.shape
    past_len = cache_sb_k.shape[2]
    pos_s = past_len + jnp.arange(dec_seq, dtype=jnp.int32)
    past = {"sb_k": cache_sb_k, "sb_v": cache_sb_v, "conv": state_conv, "ckv": cache_mla_ckv, "kpe": cache_mla_kpe}
    y_sample, st_s = _run_trunk(x_sample, pos_s, past, params, batch=dec_batch, seq_len=dec_seq)
    return (y_prompt, y_sample,
            st_p["sb_k"], st_p["sb_v"], st_p["conv"], st_p["ckv"], st_p["kpe"],
            st_s["sb_k"], st_s["sb_v"], st_s["conv"], st_s["ckv"], st_s["kpe"], st_s["sgu_v"])
```

```python
import functools
import math

import numpy as np
import jax
import jax.numpy as jnp
from jax import lax
from jax.experimental import pallas as pl
from jax.experimental.pallas import tpu as pltpu

F32 = jnp.float32
BF16 = jnp.bfloat16

EPS = 1e-6
D_MODEL = 1024
CHUNK = 64
SB_HEADS = 8
SB_HEAD_DIM = 64
SB_WIDTH = SB_HEADS * SB_HEAD_DIM
SB_SCALE = 1.0 / math.sqrt(SB_HEAD_DIM)
CONV_DIM = D_MODEL // 2
CONV_W = 3
SGU_CHUNK = 128
SGU_GROUPS = 4
SGU_DIM = D_MODEL // 2
SGU_GROUP_DIM = SGU_DIM // SGU_GROUPS
MLA_HEADS = 8
Q_LORA = 384
KV_LORA = 256
NOPE_DIM = 64
ROPE_DIM = 32
V_DIM = 64
ROPE_THETA = 10000.0
MLA_SCALE = 1.0 / math.sqrt(NOPE_DIM + ROPE_DIM)
D_FF = 4 * D_MODEL
FF_CHUNK = 1024

LANES = 128
HEAD_PAIRS = SB_HEADS // 2
MLA_QK = 2 * KV_LORA
ODD_IN_PAD =2 * SGU_DIM + Q_LORA + KV_LORA + LANES
VMEM_LIMIT = 56 * 1024 * 1024
NEG_INF = -1e30
LOG2_E = math.log2(math.e)

ROW_TILE = 512
SB_TQ, SB_TK = 256, 128
MLA_TQ, MLA_TK = 256, 256
MLA_ROW_CHUNK = 64
MLA_ROW_GROUPS = 1


def _cparams(sem):
    return pltpu.CompilerParams(dimension_semantics=sem, vmem_limit_bytes=VMEM_LIMIT)


def _const_spec(shape):
    nd = len(shape)
    return pl.BlockSpec(shape, lambda *_: (0,) * nd, pipeline_mode=pl.Buffered(1))


def _rms(x, g):
    return x * lax.rsqrt(jnp.mean(x * x, axis=-1, keepdims=True) + EPS) * g


def _dot(a, b):
    return jnp.dot(a, b, preferred_element_type=F32)


def _dot_nt(a, b):
    return lax.dot_general(a, b, (((1,), (1,)), ((), ())), preferred_element_type=F32)


def _log_sigmoid(z):
    neg_abs = lax.bitcast_convert_type(lax.bitcast_convert_type(z, jnp.uint32) | jnp.uint32(0x80000000), F32)
    return jnp.minimum(z, 0.0) - jnp.log(1.0 + jnp.exp(neg_abs))


def _split_hi_lo(x):
    hi = x.astype(BF16)
    lo = (x - hi.astype(F32)).astype(BF16)
    return hi, lo


def _rope_block(x, cos, sin):
    half = ROPE_DIM // 2
    lane = lax.broadcasted_iota(jnp.int32, x.shape, 1)
    partner = jnp.where(lane < half, pltpu.roll(x, LANES - half, 1), pltpu.roll(x, half, 1))
    return x * cos + partner * sin


def _proj_even_body(*refs, tm, seq_len, whole_seqs):
    if whole_seqs:
        (x_ref, g_ref, w_ref, wc_ref, e1_ref, e2_ref,
         q_ref, k_ref, v_ref, kb_ref, vb_ref, gc_ref, ci_ref) = refs
    else:
        (x_ref, g_ref, w_ref, wc_ref, wkv_ref,
         q_ref, k_ref, v_ref, kb_ref, vb_ref, gc_ref, cs_ref, tail_ref) = refs
    h = _rms(x_ref[...], g_ref[...]).astype(BF16)

    def proj(j):
        return _dot(h, w_ref[:, j * SB_WIDTH:(j + 1) * SB_WIDTH])

    q_ref[...] = (proj(0) * SB_SCALE).astype(BF16)
    if whole_seqs:
        k = proj(1)
        v = proj(2)
        k_ref[...] = k
        v_ref[...] = v
        kb_ref[...] = k.astype(BF16)
        vb_ref[...] = v.astype(BF16)
    else:
        k = _dot_nt(wkv_ref[:SB_WIDTH, :], h)
        v = _dot_nt(wkv_ref[SB_WIDTH:, :], h)
        k_ref[0] = k
        v_ref[0] = v
        kb_ref[0] = k.astype(BF16)
        vb_ref[0] = v.astype(BF16)
    g_post = proj(3)
    ci = proj(4) * proj(5)
    r1 = pltpu.roll(ci, 1, 0)
    r2 = pltpu.roll(ci, 2, 0)
    row = lax.broadcasted_iota(jnp.int32, (tm, 1), 0)
    if whole_seqs:
        tpos = row % seq_len
        s1 = jnp.where(tpos < 1, e1_ref[...], r1)
        s2 = jnp.where(tpos < 2, e2_ref[...], r2)
        ci_ref[...] = ci
    else:
        first = (pl.program_id(0) % (seq_len // tm)) == 0

        @pl.when(first)
        def _():
            tail_ref[...] = jnp.zeros_like(tail_ref)

        t1 = tail_ref[7:8, :]
        t2 = tail_ref[6:7, :]
        s1 = jnp.where(row == 0, t1, r1)
        s2 = jnp.where(row == 0, t2, jnp.where(row == 1, t1, r2))
        tail_ref[...] = ci[tm - 8:, :]
        cs_ref[0] = ci[tm - (CONV_W - 1):, :]
    conv = wc_ref[0:1, :] * s2 + wc_ref[1:2, :] * s1 + wc_ref[2:3, :] * ci
    gc_ref[...] = (g_post * conv).astype(BF16)


def _proj_even(x, g, w_in, w_kv_t, w_conv, conv_prev, *, batch, seq_len, tm):
    n = x.shape[0]
    whole = tm % seq_len == 0
    grid = (n // tm,)
    row_spec = lambda w: pl.BlockSpec((tm, w), lambda i: (i, 0))
    in_specs = [row_spec(D_MODEL), _const_spec((1, D_MODEL)), _const_spec(w_in.shape), _const_spec(w_conv.shape)]
    args = [x, g, w_in, w_conv]
    if whole:
        kv_shape, kv_spec = (n, SB_WIDTH), row_spec(SB_WIDTH)
    else:
        seq_tiles = seq_len // tm
        kv_shape = (batch, SB_WIDTH, seq_len)
        kv_spec = pl.BlockSpec((1, SB_WIDTH, tm), lambda i: (i // seq_tiles, 0, i % seq_tiles))
    out_shape = [jax.ShapeDtypeStruct((n, SB_WIDTH), BF16), jax.ShapeDtypeStruct(kv_shape, F32),
                 jax.ShapeDtypeStruct(kv_shape, F32), jax.ShapeDtypeStruct(kv_shape, BF16),
                 jax.ShapeDtypeStruct(kv_shape, BF16), jax.ShapeDtypeStruct((n, CONV_DIM), BF16)]
    out_specs = [row_spec(SB_WIDTH)] + [kv_spec] * 4 + [row_spec(CONV_DIM)]
    scratch = []
    if whole:
        if conv_prev is None:
            conv_prev = jnp.zeros((batch, CONV_W - 1, CONV_DIM), F32)
        e1 =jnp.zeros((batch, seq_len, CONV_DIM), F32).at[:, 0].set(conv_prev[:, 1])
        e2 = jnp.zeros((batch, seq_len, CONV_DIM), F32).at[:, 0].set(conv_prev[:, 0]).at[:, 1].set(conv_prev[:, 1])
        args += [e1.reshape(n, CONV_DIM), e2.reshape(n, CONV_DIM)]
        in_specs += [row_spec(CONV_DIM), row_spec(CONV_DIM)]
        out_shape.append(jax.ShapeDtypeStruct((n, CONV_DIM), F32))
        out_specs.append(row_spec(CONV_DIM))
    else:
        assert seq_len % tm == 0 and conv_prev is None
        args.append(w_kv_t)
        in_specs.append(_const_spec(w_kv_t.shape))
        out_shape.append(jax.ShapeDtypeStruct((batch, CONV_W - 1, CONV_DIM), F32))
        out_specs.append(pl.BlockSpec((1, CONV_W - 1, CONV_DIM), lambda i: (i // seq_tiles, 0, 0)))
        scratch.append(pltpu.VMEM((8, CONV_DIM), F32))
    outs = pl.pallas_call(
        functools.partial(_proj_even_body, tm=tm, seq_len=seq_len, whole_seqs=whole),
        grid=grid, in_specs=in_specs, out_specs=out_specs, out_shape=out_shape, scratch_shapes=scratch,
        compiler_params=_cparams(("arbitrary",)), name="proj_even")(*args)
    q, k, v, kb, vb, gc, last = outs
    if whole:
        last = last.reshape(batch, seq_len, CONV_DIM)[:, seq_len - (CONV_W - 1):]
    return q, k, v, kb, vb, gc, last


def _suffix_rhs(tk):
    j = np.arange(2 * tk)[:, None] % tk
    c = np.arange(2 * tk)[None, :]
    return jnp.asarray(np.where(c < tk, j > c, True), dtype=BF16)


def _sb_prompt_body(q_ref, k_ref, v_ref, r_ref, o_ref, carry_ref, acc_ref, *, tq, tk):
    qi = pl.program_id(1)
    ratio = tq // tk
    even = lax.broadcasted_iota(jnp.int32, (LANES, tk), 0) < SB_HEAD_DIM
    rhs = r_ref[...]
    carry_ref[...] = jnp.zeros_like(carry_ref)
    acc_ref[...] = jnp.zeros_like(acc_ref)
    qs = [q_ref[:, p * LANES:(p + 1) * LANES] for p in range(HEAD_PAIRS)]

    def block_diag(blk):
        zero = jnp.zeros_like(blk)
        return jnp.concatenate([jnp.where(even, blk, zero), jnp.where(even, zero, blk)], axis=1)

    def tile(kj, masked):
        start = pl.multiple_of(kj * tk, tk)
        if masked:
            q_pos = qi * tq + lax.broadcasted_iota(jnp.int32, (tq, tk), 0)
            k_pos = kj * tk + lax.broadcasted_iota(jnp.int32, (tq, tk), 1)
            visible = k_pos < q_pos
        zs = [_dot(qs[p], block_diag(k_ref[0, p * LANES:(p + 1) * LANES, pl.ds(start, tk)]))
              for p in range(HEAD_PAIRS)]
        log_betas, splits = [], []
        for hd in range(SB_HEADS):
            z = zs[hd // 2][:, (hd % 2) * tk:(hd % 2 + 1) * tk]
            log_beta = _log_sigmoid(z)
            log_1m = log_beta - z
            if masked:
                log_1m = jnp.where(visible, log_1m, 0.0)
            hi, lo = _split_hi_lo(log_1m)
            log_betas.append(log_beta)
            splits.append(jnp.concatenate([hi, lo], axis=1))
        sums = [_dot(sp, rhs) for sp in splits]
        ws = []
        for hd in range(SB_HEADS):
            carry = carry_ref[hd]
            w = jnp.exp(log_betas[hd] + sums[hd][:, :tk] + carry)
            if masked:
                w = jnp.where(visible, w, 0.0)
            carry_ref[hd] = carry + sums[hd][:, tk:]
            ws.append(w.astype(BF16))
        for p in range(HEAD_PAIRS):
            vbd = block_diag(v_ref[0, p * LANES:(p + 1) * LANES, pl.ds(start, tk)])
            acc_ref[p] += _dot_nt(jnp.concatenate([ws[2 * p], ws[2 * p + 1]], axis=1), vbd)

    for d in range(ratio):
        tile(qi * ratio + ratio - 1 - d, True)

    def body(it, c):
        tile(qi * ratio - 1 - 2 * it, False)
        tile(qi * ratio - 2 - 2 * it, False)
        return c

    lax.fori_loop(0, qi * (ratio // 2), body, 0)
    for p in range(HEAD_PAIRS):
        o_ref[:, p * LANES:(p + 1) * LANES] = acc_ref[p].astype(BF16)


def _sb_prompt(q, kb, vb, *, batch, seq_len, tq, tk):
    assert (tq // tk) % 2 == 0 and tq % tk == 0 and seq_len % tq == 0
    n = q.shape[0]
    nq = seq_len // tq
    rhs = _suffix_rhs(tk)
    q_spec = pl.BlockSpec((tq, SB_WIDTH), lambda b, i: (b * nq + i, 0))
    kv_spec = pl.BlockSpec((1, SB_WIDTH, seq_len), lambda b, i: (b, 0, 0))
    return pl.pallas_call(
        functools.partial(_sb_prompt_body, tq=tq, tk=tk),
        grid=(batch, nq),
        in_specs=[q_spec, kv_spec, kv_spec, _const_spec(rhs.shape)],
        out_specs=q_spec,
        out_shape=jax.ShapeDtypeStruct((n, SB_WIDTH), BF16),
        scratch_shapes=[pltpu.VMEM((SB_HEADS, tq, LANES), F32), pltpu.VMEM((HEAD_PAIRS, tq, LANES), F32)],
        compiler_params=_cparams(("arbitrary", "arbitrary")), name="sb_attn_prompt")(q, kb, vb, rhs)


def _sb_sample_body(q_ref, kn_ref, vn_ref, kc_ref, vc_ref, l_ref, o_ref, qh_ref, carry_ref, acc_ref,
                    *, dec, kblk, tk):
    j = pl.program_id(1)
    rows = SB_HEADS * dec
    rhs = l_ref[...]

    def tile(kt, vt, visible):
        z = _dot(qh_ref[...], kt)
        log_beta = _log_sigmoid(z)
        log_1m = log_beta - z
        if visible is not None:
            log_1m = jnp.where(visible, log_1m, 0.0)
        hi, lo = _split_hi_lo(log_1m)
        sums = _dot(jnp.concatenate([hi, lo], axis=1), rhs)
        w = jnp.exp(log_beta + sums[:, :tk] + carry_ref[...])
        if visible is not None:
            w = jnp.where(visible, w, 0.0)
        acc_ref[...] += _dot_nt(w.astype(BF16), vt)
        carry_ref[...] += sums[:, tk:]

    def new_keys_t(x):
        x = jnp.concatenate([x.astype(F32), jnp.zeros((tk - dec, SB_WIDTH), F32)], axis=0)
        return x.T.astype(BF16)

    @pl.when(j == 0)
    def _():
        q = q_ref[...]
        qt = jnp.concatenate([q] * SB_HEADS, axis=0)
        rr = lax.broadcasted_iota(jnp.int32, (rows, SB_WIDTH), 0)
        cc = lax.broadcasted_iota(jnp.int32, (rows, SB_WIDTH), 1)
        qh_ref[...] = jnp.where(rr // dec == cc // SB_HEAD_DIM, qt, jnp.zeros_like(qt))
        carry_ref[...] = jnp.zeros_like(carry_ref)
        acc_ref[...] = jnp.zeros_like(acc_ref)
        query_t = lax.broadcasted_iota(jnp.int32, (rows, tk), 0) % dec
        key_i = lax.broadcasted_iota(jnp.int32, (rows, tk), 1)
        tile(new_keys_t(kn_ref[...]), new_keys_t(vn_ref[...]),
             key_i < query_t)

    for s in reversed(range(kblk // tk)):
        tile(kc_ref[0, :, s * tk:(s + 1) * tk].astype(BF16), vc_ref[0, :, s * tk:(s + 1) * tk].astype(BF16), None)

    @pl.when(j == pl.num_programs(1) - 1)
    def _():
        cc = lax.broadcasted_iota(jnp.int32, (dec, SB_WIDTH), 1)
        out = jnp.zeros((dec, SB_WIDTH), F32)
        for h in range(SB_HEADS):
            out = out + jnp.where(cc // SB_HEAD_DIM == h, acc_ref[h * dec:(h + 1) * dec, :], 0.0)
        o_ref[...] = out.astype(BF16)


def _sb_sample(q, kb, vb, cache_kt, cache_vt, *, batch, dec, kblk=1024, tk=256):
    n = q.shape[0]
    past = cache_kt.shape[2]
    nkb = past // kblk
    rows = SB_HEADS * dec
    rhs = _suffix_rhs(tk)
    new_spec = pl.BlockSpec((dec, SB_WIDTH), lambda b, j: (b, 0))
    cache_spec = pl.BlockSpec((1, SB_WIDTH, kblk), lambda b, j: (b, 0, nkb - 1 - j))
    return pl.pallas_call(
        functools.partial(_sb_sample_body, dec=dec, kblk=kblk, tk=tk),
        grid=(batch, nkb),
        in_specs=[new_spec, new_spec, new_spec, cache_spec, cache_spec, _const_spec(rhs.shape)],
        out_specs=new_spec,
        out_shape=jax.ShapeDtypeStruct((n, SB_WIDTH), BF16),
        scratch_shapes=[pltpu.VMEM((rows, SB_WIDTH), BF16), pltpu.VMEM((rows, tk), F32),
                        pltpu.VMEM((rows, SB_WIDTH), F32)],
        compiler_params=_cparams(("arbitrary", "arbitrary")), name="sb_attn_sample")(
            q, kb, vb, cache_kt, cache_vt, rhs)


def _proj_odd_body(*refs, tm, gate_len, emit_vn, kpe_transposed):
    (x_ref, g_ref, w_ref, lng_ref, lnb_ref, ws_ref, bs_ref, qg_ref, kvg_ref, wqn_ref, wqp_ref,
     cos_ref, sin_ref) = refs[:13]
    outs = refs[13:]
    if emit_vn:
        sgu_ref, vn_ref, qn_ref, qp_ref, ckv_ref, ckvb_ref, kpe_ref, kpeb_ref = outs
    else:
        sgu_ref, qn_ref, qp_ref, ckv_ref, ckvb_ref, kpe_ref, kpeb_ref = outs
    h = _rms(x_ref[...], g_ref[...]).astype(BF16)
    o_v, o_q, o_kv, o_pe = SGU_DIM, 2 * SGU_DIM, 2 * SGU_DIM + Q_LORA, 2 * SGU_DIM + Q_LORA + KV_LORA
    u = _dot(h, w_ref[:, :o_v])
    v = _dot(h, w_ref[:, o_v:o_q])
    mu = jnp.mean(v, axis=-1, keepdims=True)
    vc = v - mu
    var = jnp.mean(vc * vc, axis=-1, keepdims=True)
    vn = vc * lax.rsqrt(var + EPS) * lng_ref[...] + lnb_ref[...]
    if emit_vn:
        vn_ref[...] = vn
    vnb = vn.astype(BF16)
    rr = lax.broadcasted_iota(jnp.int32, (SGU_CHUNK, SGU_CHUNK), 0)
    cc = lax.broadcasted_iota(jnp.int32, (SGU_CHUNK, SGU_CHUNK), 1)
    causal = (rr // gate_len == cc // gate_len) & (cc <= rr)
    for g in range(SGU_GROUPS):
        gs = slice(g * SGU_GROUP_DIM, (g + 1) * SGU_GROUP_DIM)
        wg = jnp.where(causal, ws_ref[g], 0.0).astype(BF16)
        for c in range(tm // SGU_CHUNK):
            rs = slice(c * SGU_CHUNK, (c + 1) * SGU_CHUNK)
            s = _dot(wg, vnb[rs, gs]) + bs_ref[:, gs]
            sgu_ref[rs, gs] = (u[rs, gs] * s).astype(BF16)
    cq = _rms(_dot(h, w_ref[:, o_q:o_kv]), qg_ref[...]).astype(BF16)
    qn_ref[...] = _dot(cq, wqn_ref[...]).astype(BF16)
    cos = cos_ref[...]
    sin = sin_ref[...]
    for hd in range(MLA_HEADS):
        qp = _rope_block(_dot(cq, wqp_ref[:, hd * LANES:(hd + 1) * LANES]), cos, sin)
        qp_ref[hd] = qp[:, :ROPE_DIM].astype(BF16)
    ckv = _rms(_dot(h, w_ref[:, o_kv:o_pe]), kvg_ref[...])
    ckv_ref[...] = ckv
    ckvb_ref[...] = ckv.astype(BF16)
    kpe = _rope_block(_dot(h, w_ref[:, o_pe:]), cos, sin)
    if kpe_transposed:
        kpe_ref[0] = kpe.T[:ROPE_DIM, :]
    else:
        kpe_ref[...] = kpe[:, :ROPE_DIM]
    kpeb_ref[...] = kpe.astype(BF16)


def _proj_odd(x, g, w_in, ln_g, ln_b, w_s, b_s, qg, kvg, wqn, wqp, cos, sin, *, batch, seq_len, tm, gate_len,
              emit_vn):
    n = x.shape[0]
    pos_tiles = cos.shape[0] // tm
    kpe_transposed = tm % seq_len != 0
    if kpe_transposed:
        seq_tiles = seq_len // tm
        kpe_shape = (batch, ROPE_DIM, seq_len)
        kpe_spec = pl.BlockSpec((1, ROPE_DIM, tm), lambda i: (i // seq_tiles, 0, i % seq_tiles))
    else:
        kpe_shape, kpe_spec = (n, ROPE_DIM), pl.BlockSpec((tm, ROPE_DIM), lambda i: (i, 0))
    row_spec = lambda w: pl.BlockSpec((tm, w), lambda i: (i, 0))
    tab_spec = pl.BlockSpec((tm, LANES), lambda i: (i % pos_tiles, 0))
    consts = [g, w_in, ln_g, ln_b, w_s, b_s, qg, kvg, wqn, wqp]
    out_shape = [jax.ShapeDtypeStruct((n, SGU_DIM), BF16)]
    out_specs = [row_spec(SGU_DIM)]
    if emit_vn:
        out_shape.append(jax.ShapeDtypeStruct((n, SGU_DIM), F32))
        out_specs.append(row_spec(SGU_DIM))
    out_shape += [jax.ShapeDtypeStruct((n, MLA_HEADS * NOPE_DIM), BF16),
                  jax.ShapeDtypeStruct((MLA_HEADS, n, ROPE_DIM), BF16),
                  jax.ShapeDtypeStruct((n, KV_LORA), F32), jax.ShapeDtypeStruct((n, KV_LORA), BF16),
                  jax.ShapeDtypeStruct(kpe_shape, F32), jax.ShapeDtypeStruct((n, LANES), BF16)]
    out_specs += [row_spec(MLA_HEADS * NOPE_DIM), pl.BlockSpec((MLA_HEADS, tm, ROPE_DIM), lambda i: (0, i, 0)),
                  row_spec(KV_LORA), row_spec(KV_LORA), kpe_spec, row_spec(LANES)]
    return pl.pallas_call(
        functools.partial(_proj_odd_body, tm=tm, gate_len=gate_len, emit_vn=emit_vn, kpe_transposed=kpe_transposed),
        grid=(n // tm,),
        in_specs=[row_spec(D_MODEL)] + [_const_spec(c.shape) for c in consts] + [tab_spec, tab_spec],
        out_specs=out_specs, out_shape=out_shape,
        compiler_params=_cparams(("arbitrary",)), name="proj_odd")(x, *consts, cos, sin)


def _mla_queries(qn, qpe, wuk_ref, qcat_ref, tq):
    lane = lax.broadcasted_iota(jnp.int32, (tq, LANES), 1)
    low = lane < NOPE_DIM
    for hd in range(MLA_HEADS):
        p = hd // 2
        pair = qn[:, p * LANES:(p + 1) * LANES]
        qm = jnp.where(low if hd % 2 == 0 else jnp.logical_not(low), pair, jnp.zeros_like(pair))
        qcat_ref[hd * tq:(hd + 1) * tq, :KV_LORA] = _dot(qm, wuk_ref[p]).astype(BF16)
    qcat_ref[:, KV_LORA:] = jnp.zeros((MLA_HEADS * tq, MLA_QK - KV_LORA), BF16)
    qcat_ref[:, KV_LORA:KV_LORA + ROPE_DIM] = qpe


def _lane_tile(x, width):
    return jnp.concatenate([x] * (width // LANES), axis=1)


def _mla_tile(qcat_ref, kcat, sc, visible_fn):
    s_ref, p_ref, m_ref, l_ref, acc_ref = sc
    rows, tk = s_ref.shape
    ck = kcat[:, :KV_LORA]
    grp = rows // MLA_ROW_GROUPS
    for g in range(MLA_ROW_GROUPS):
        gs = slice(g * grp, (g + 1) * grp)
        s_ref[gs, :] = _dot_nt(qcat_ref[gs, :], kcat)
    for g in range(MLA_ROW_GROUPS):
        for c in range(g * grp // MLA_ROW_CHUNK, (g + 1) * grp // MLA_ROW_CHUNK):
            rs = slice(c * MLA_ROW_CHUNK, (c + 1) * MLA_ROW_CHUNK)
            s = s_ref[rs, :] * (MLA_SCALE * LOG2_E)
            if visible_fn is not None:
                s = jnp.where(visible_fn(c), s, NEG_INF)
            m_prev = m_ref[rs, :]
            m_new = jnp.maximum(m_prev, jnp.max(s, axis=-1, keepdims=True))
            alpha = jnp.exp2(m_prev - m_new)
            p = jnp.exp2(s - _lane_tile(m_new, tk))
            l_ref[rs, :] = alpha * l_ref[rs, :] + jnp.sum(p, axis=-1, keepdims=True)
            m_ref[rs, :] = m_new
            p_ref[rs, :] = p.astype(BF16)
            acc_ref[rs, :] = acc_ref[rs, :] * _lane_tile(alpha, KV_LORA)
        gs = slice(g * grp, (g + 1) * grp)
        acc_ref[gs, :] += _dot(p_ref[gs, :], ck)


def _mla_finish(sc, wuv_ref, o_ref, tq):
    l_ref, acc_ref = sc[3], sc[4]
    o_lat = (acc_ref[...] / _lane_tile(l_ref[...], KV_LORA)).astype(BF16)
    for p in range(MLA_HEADS // 2):
        h0, h1 = 2 * p, 2 * p + 1
        o_ref[:, p * LANES:(p + 1) * LANES] = (
            _dot(o_lat[h0 * tq:(h0 + 1) * tq], wuv_ref[h0]) + _dot(o_lat[h1 * tq:(h1 + 1) * tq], wuv_ref[h1])
        ).astype(BF16)


def _mla_init(sc):
    m_ref, l_ref, acc_ref = sc[2], sc[3], sc[4]
    m_ref[...] = jnp.full_like(m_ref, -jnp.inf)
    l_ref[...] = jnp.zeros_like(l_ref)
    acc_ref[...] = jnp.zeros_like(acc_ref)


def _mla_prompt_body(qn_ref, qp_ref, ckv_ref, kpe_ref, wuk_ref, wuv_ref, o_ref, qcat_ref, *sc, tq, tk):
    qi = pl.program_id(1)
    rows = MLA_HEADS * tq
    _mla_queries(qn_ref[...], qp_ref[...].reshape(rows, ROPE_DIM), wuk_ref, qcat_ref, tq)
    _mla_init(sc)

    n_full = (qi * tq) // tk
    col = lax.broadcasted_iota(jnp.int32, (MLA_ROW_CHUNK, tk), 1)

    def visible(c):
        q_chunk_end = qi * tq + ((c * MLA_ROW_CHUNK) % tq) // CHUNK * CHUNK + CHUNK
        return col < q_chunk_end - n_full * tk

    def tile(kj, visible_fn):
        start = pl.multiple_of(kj * tk, tk)
        kcat = jnp.concatenate([ckv_ref[pl.ds(start, tk), :], kpe_ref[pl.ds(start, tk), :],
                                jnp.zeros((tk, MLA_QK - KV_LORA - LANES), BF16)], axis=1)
        _mla_tile(qcat_ref, kcat, sc, visible_fn)

    def body(kj, c):
        tile(kj, None)
        return c

    lax.fori_loop(0, n_full, body, 0)
    tile(n_full, visible)
    _mla_finish(sc, wuv_ref, o_ref, tq)


def _mla_scratch(rows, tk):
    return [pltpu.VMEM((rows, MLA_QK), BF16), pltpu.VMEM((rows, tk), F32), pltpu.VMEM((rows, tk), BF16),
            pltpu.VMEM((rows, LANES), F32), pltpu.VMEM((rows, LANES), F32), pltpu.VMEM((rows, KV_LORA), F32)]


def _mla_prompt(qn, qp, ckvb, kpeb, wuk, wuv, *, batch, seq_len, tq, tk):
    assert tq % CHUNK == 0 and CHUNK % MLA_ROW_CHUNK == 0 and seq_len % tq == 0 and seq_len % tk == 0
    n = qn.shape[0]
    nq = seq_len // tq
    width = MLA_HEADS * V_DIM
    return pl.pallas_call(
        functools.partial(_mla_prompt_body, tq=tq, tk=tk),
        grid=(batch, nq),
        in_specs=[pl.BlockSpec((tq, MLA_HEADS * NOPE_DIM), lambda b, i: (b * nq + i, 0)),
                  pl.BlockSpec((MLA_HEADS, tq, ROPE_DIM), lambda b, i: (0, b * nq + i, 0)),
                  pl.BlockSpec((seq_len, KV_LORA), lambda b, i: (b, 0)),
                  pl.BlockSpec((seq_len, LANES), lambda b, i: (b, 0)),
                  _const_spec(wuk.shape), _const_spec(wuv.shape)],
        out_specs=pl.BlockSpec((tq, width), lambda b, i: (b * nq + i, 0)),
        out_shape=jax.ShapeDtypeStruct((n, width), BF16),
        scratch_shapes=_mla_scratch(MLA_HEADS * tq, tk),
        compiler_params=_cparams(("arbitrary", "arbitrary")), name="mla_attn_prompt")(qn, qp, ckvb, kpeb, wuk, wuv)


def _mla_sample_body(qn_ref, qp_ref, cn_ref, pn_ref, cc_ref, pc_ref, wuk_ref, wuv_ref, o_ref, kcat_ref, qcat_ref,
                     *sc, dec, past, kblk, tk):
    j = pl.program_id(1)
    rows = MLA_HEADS * dec

    @pl.when(j == 0)
    def _():
        _mla_queries(qn_ref[...], qp_ref[...].reshape(rows, ROPE_DIM), wuk_ref, qcat_ref, dec)
        _mla_init(sc)
        kcat_ref[:, KV_LORA:] = jnp.zeros((tk, MLA_QK - KV_LORA), BF16)
        new = jnp.concatenate([cn_ref[...], pn_ref[...], jnp.zeros((dec, MLA_QK - KV_LORA - LANES), BF16)], axis=1)
        kcat = jnp.concatenate([new, jnp.zeros((tk - dec, MLA_QK), BF16)], axis=0)

        def visible(c):
            col = lax.broadcasted_iota(jnp.int32, (MLA_ROW_CHUNK, tk), 1)
            row = c * MLA_ROW_CHUNK + lax.broadcasted_iota(jnp.int32, (MLA_ROW_CHUNK, tk), 0)
            return (col < dec) & ((past + col) // CHUNK <= (past + row % dec) // CHUNK)

        _mla_tile(qcat_ref, kcat, sc, visible)

    def body(it, c):
        start = pl.multiple_of(it * tk, tk)
        kcat_ref[:, :KV_LORA] = cc_ref[0, pl.ds(start, tk), :].astype(BF16)
        kcat_ref[:, KV_LORA:KV_LORA + ROPE_DIM] = pc_ref[0, pl.ds(start, tk), :].astype(BF16)
        _mla_tile(qcat_ref, kcat_ref[...], sc, None)
        return c

    lax.fori_loop(0, kblk // tk, body, 0)

    @pl.when(j == pl.num_programs(1) - 1)
    def _():
        _mla_finish(sc, wuv_ref, o_ref, dec)


def _mla_sample(qn, qp, ckvb, kpeb, cache_ckv, cache_kpe, wuk, wuv, *, batch, dec, kblk=1024, tk=512):
    n = qn.shape[0]
    past = cache_ckv.shape[1]
    width = MLA_HEADS * V_DIM
    return pl.pallas_call(
        functools.partial(_mla_sample_body, dec=dec, past=past, kblk=kblk, tk=tk),
        grid=(batch, past // kblk),
        in_specs=[pl.BlockSpec((dec, MLA_HEADS * NOPE_DIM), lambda b, j: (b, 0)),
                  pl.BlockSpec((MLA_HEADS, dec, ROPE_DIM), lambda b, j: (0, b, 0)),
                  pl.BlockSpec((dec, KV_LORA), lambda b, j: (b, 0)),
                  pl.BlockSpec((dec, LANES), lambda b, j: (b, 0)),
                  pl.BlockSpec((1, kblk, KV_LORA), lambda b, j: (b, j, 0)),
                  pl.BlockSpec((1, kblk, ROPE_DIM), lambda b, j: (b, j, 0)),
                  _const_spec(wuk.shape), _const_spec(wuv.shape)],
        out_specs=pl.BlockSpec((dec, width), lambda b, j: (b, 0)),
        out_shape=jax.ShapeDtypeStruct((n, width), BF16),
        scratch_shapes=[pltpu.VMEM((tk, MLA_QK), BF16)] + _mla_scratch(MLA_HEADS * dec, tk),
        compiler_params=_cparams(("arbitrary", "arbitrary")), name="mla_attn_sample")(
            qn, qp, ckvb, kpeb, cache_ckv, cache_kpe, wuk, wuv)


def _out_ffn_body(a_ref, b_ref, x_ref, woa_ref, wob_ref, gpost_ref, gpre_ref, wup_ref, wdn_ref, gfpost_ref, o_ref):
    mixed = _dot(a_ref[...], woa_ref[...]) + _dot(b_ref[...], wob_ref[...])
    x1 = x_ref[...] + _rms(mixed, gpost_ref[...])
    h = _rms(x1, gpre_ref[...]).astype(BF16)
    down = jnp.zeros_like(x1)
    for c in range(D_FF // FF_CHUNK):
        up = _dot(h, wup_ref[:, c * FF_CHUNK:(c + 1) * FF_CHUNK])
        act = jnp.square(jnp.maximum(up, 0.0)).astype(BF16)
        down = down + _dot(act, wdn_ref[c * FF_CHUNK:(c + 1) * FF_CHUNK, :])
    o_ref[...] = x1 + _rms(down, gfpost_ref[...])


def _out_ffn(a, b, x, woa, wob, g_post, g_pre, w_up, w_down, g_fpost, *, tm):
    n = x.shape[0]
    row_spec = lambda w: pl.BlockSpec((tm, w), lambda i: (i, 0))
    consts = [woa, wob, g_post, g_pre, w_up, w_down, g_fpost]
    return pl.pallas_call(
        _out_ffn_body,
        grid=(n // tm,),
        in_specs=[row_spec(a.shape[1]), row_spec(b.shape[1]), row_spec(D_MODEL)] + [_const_spec(c.shape) for c in consts],
        out_specs=row_spec(D_MODEL),
        out_shape=jax.ShapeDtypeStruct((n, D_MODEL), F32),
        compiler_params=_cparams(("arbitrary",)), name="out_ffn")(a, b, x, *consts)


def _rope_tables(pos, reps):
    half = ROPE_DIM // 2
    inv = ROPE_THETA ** (-jnp.arange(half, dtype=F32) / half)
    ang = pos.astype(F32)[:, None] * inv[None, :]
    zeros = jnp.zeros((pos.shape[0], LANES - ROPE_DIM), F32)
    cos = jnp.concatenate([jnp.cos(ang), jnp.cos(ang), zeros], axis=1)
    sin = jnp.concatenate([-jnp.sin(ang), jnp.sin(ang), zeros], axis=1)
    return jnp.tile(cos, (reps, 1)), jnp.tile(sin, (reps, 1))


def _prep_even(p, j):
    w_in = p["even_w_in"][j]
    return dict(w_in=w_in.astype(BF16), w_kv_t=w_in[:, SB_WIDTH:3 * SB_WIDTH].T.astype(BF16), w_conv=p["even_w_conv"][j],
                woa=p["even_w_out"][j, :SB_WIDTH].astype(BF16), wob=p["even_w_out"][j, SB_WIDTH:].astype(BF16))


def _prep_odd(p, j, gate_len):
    w_in = p["odd_w_in"][j]
    w_in = jnp.pad(w_in, ((0, 0), (0, ODD_IN_PAD - w_in.shape[1]))).astype(BF16)
    reps = SGU_CHUNK // gate_len
    w_s = jnp.tile(p["sgu_w_s"][j, :, :gate_len, :gate_len], (1, reps, reps))
    b_s = jnp.tile(p["sgu_b_s"][j, :, :gate_len], (1, reps))
    b_s = jnp.repeat(b_s.T, SGU_GROUP_DIM, axis=1)
    w_uq = p["mla_w_uq"][j].reshape(Q_LORA, MLA_HEADS, NOPE_DIM + ROPE_DIM)
    wqn = w_uq[:, :, :NOPE_DIM].reshape(Q_LORA, MLA_HEADS * NOPE_DIM).astype(BF16)
    wqp = jnp.pad(w_uq[:, :, NOPE_DIM:], ((0, 0), (0, 0), (0, LANES - ROPE_DIM)))
    wqp = wqp.reshape(Q_LORA, MLA_HEADS * LANES).astype(BF16)
    wuk = p["mla_w_uk"][j].reshape(MLA_HEADS // 2, 2 * NOPE_DIM, KV_LORA).astype(BF16)
    w_uv = p["mla_w_uv"][j]
    wuv = jnp.stack([jnp.pad(w_uv[h], ((0, 0), ((h % 2) * V_DIM, (1 - h % 2) * V_DIM))) for h in range(MLA_HEADS)])
    return dict(w_in=w_in, ln_g=p["sgu_ln_g"][j][None], ln_b=p["sgu_ln_b"][j][None], w_s=w_s, b_s=b_s,
                qg=p["mla_q_norm_g"][j][None], kvg=p["mla_kv_norm_g"][j][None], wqn=wqn, wqp=wqp, wuk=wuk,
                wuv=wuv.astype(BF16),
                woa=p["odd_w_out"][j, :SGU_DIM].astype(BF16), wob=p["odd_w_out"][j, SGU_DIM:].astype(BF16))


def _run_trunk(x, pos, past, p, *, batch, seq_len):
    depth = p["mix_pre_g"].shape[0]
    n = batch * seq_len
    x = x.reshape(n, D_MODEL)
    is_sample = past is not None
    tm = min(ROW_TILE, n)
    assert n % tm == 0 and (tm % seq_len == 0 or seq_len % tm == 0) and tm % SGU_CHUNK == 0
    gate_len = min(seq_len, SGU_CHUNK)
    cos, sin = _rope_tables(pos, max(1, tm // seq_len))
    st = {k: [] for k in ("sb_k", "sb_v", "conv", "ckv", "kpe", "sgu_v")}
    for layer in range(depth):
        j = layer // 2
        g_pre = p["mix_pre_g"][layer][None]
        if layer % 2 == 0:
            w = _prep_even(p, j)
            conv_prev = past["conv"][j] if is_sample else None
            q, k, v, kb, vb, b_mix, conv_state = _proj_even(x, g_pre, w["w_in"], w["w_kv_t"], w["w_conv"], conv_prev,
                                                            batch=batch, seq_len=seq_len, tm=tm)
            if is_sample:
                assert tm % seq_len == 0

                def cache_t(c):
                    return jnp.transpose(c, (0, 2, 3, 1)).reshape(batch, SB_WIDTH, -1)

                a_mix = _sb_sample(q, kb, vb, cache_t(past["sb_k"][j]), cache_t(past["sb_v"][j]),
                                   batch=batch, dec=seq_len)
                k, v = (t.reshape(batch, seq_len, SB_HEADS, SB_HEAD_DIM) for t in (k, v))
            else:
                assert seq_len % tm == 0
                a_mix = _sb_prompt(q, kb, vb, batch=batch, seq_len=seq_len, tq=SB_TQ, tk=SB_TK)
                k, v = (jnp.transpose(t.reshape(batch, SB_HEADS, SB_HEAD_DIM, seq_len), (0, 3, 1, 2)) for t in (k, v))
            st["sb_k"].append(k)
            st["sb_v"].append(v)
            st["conv"].append(conv_state)
        else:
            w = _prep_odd(p, j, gate_len)
            outs = _proj_odd(x, g_pre, w["w_in"], w["ln_g"], w["ln_b"], w["w_s"], w["b_s"], w["qg"], w["kvg"],
                             w["wqn"], w["wqp"], cos, sin, batch=batch, seq_len=seq_len, tm=tm, gate_len=gate_len,
                             emit_vn=is_sample)
            if is_sample:
                a_mix, vn, qn, qp, ckv, ckvb, kpe, kpeb = outs
                st["sgu_v"].append(vn.reshape(batch, seq_len, SGU_DIM))
                b_mix = _mla_sample(qn, qp, ckvb, kpeb, past["ckv"][j], past["kpe"][j], w["wuk"], w["wuv"],
                                    batch=batch, dec=seq_len)
            else:
                a_mix, qn, qp, ckv, ckvb, kpe, kpeb = outs
                b_mix = _mla_prompt(qn, qp, ckvb, kpeb, w["wuk"], w["wuv"], batch=batch, seq_len=seq_len,
                                    tq=MLA_TQ, tk=MLA_TK)
            st["ckv"].append(ckv.reshape(batch, seq_len, KV_LORA))
            st["kpe"].append(jnp.transpose(kpe, (0, 2, 1)) if kpe.ndim == 3 else kpe.reshape(batch, seq_len, ROPE_DIM))
        x = _out_ffn(a_mix, b_mix, x, w["woa"], w["wob"], p["mix_post_g"][layer][None], p["ffn_pre_g"][layer][None],
                     p["ffn_w_up"][layer].astype(BF16), p["ffn_w_down"][layer].astype(BF16),
                     p["ffn_post_g"][layer][None], tm=tm)
    states = {k: jnp.stack(v) for k, v in st.items() if v}
    return x.reshape(batch, seq_len, D_MODEL), states


def kernel(x_prompt, x_sample, cache_sb_k, cache_sb_v, state_conv, cache_mla_ckv, cache_mla_kpe,
           mix_pre_g, mix_post_g, ffn_pre_g, ffn_post_g, even_w_in, even_w_conv, even_w_out,
           odd_w_in, sgu_ln_g, sgu_ln_b, sgu_w_s, sgu_b_s, mla_q_norm_g, mla_kv_norm_g,
           mla_w_uq, mla_w_uk, mla_w_uv, odd_w_out, ffn_w_up, ffn_w_down):
    params = {
        "mix_pre_g": mix_pre_g, "mix_post_g": mix_post_g, "ffn_pre_g": ffn_pre_g, "ffn_post_g": ffn_post_g,
        "even_w_in": even_w_in, "even_w_conv": even_w_conv, "even_w_out": even_w_out,
        "odd_w_in": odd_w_in, "sgu_ln_g": sgu_ln_g, "sgu_ln_b": sgu_ln_b, "sgu_w_s": sgu_w_s,
        "sgu_b_s": sgu_b_s, "mla_q_norm_g": mla_q_norm_g, "mla_kv_norm_g": mla_kv_norm_g,
        "mla_w_uq": mla_w_uq, "mla_w_uk": mla_w_uk, "mla_w_uv": mla_w_uv, "odd_w_out": odd_w_out,
        "ffn_w_up": ffn_w_up, "ffn_w_down": ffn_w_down,
    }
    batch, seq_len, _ = x_prompt.shape
    pos_p = jnp.arange(seq_len, dtype=jnp.int32)
    y_prompt, st_p = _run_trunk(x_prompt, pos_p, None, params, batch=batch, seq_len=seq_len)
    dec_batch, dec_seq, _ = x_sample.shape
    past_len = cache_sb_k.shape[2]
    pos_s = past_len + jnp.arange(dec_seq, dtype=jnp.int32)
    past = {"sb_k": cache_sb_k, "sb_v": cache_sb_v, "conv": state_conv, "ckv": cache_mla_ckv, "kpe": cache_mla_kpe}
    y_sample, st_s = _run_trunk(x_sample, pos_s, past, params, batch=dec_batch, seq_len=dec_seq)
    return (y_prompt, y_sample,
            st_p["sb_k"], st_p["sb_v"], st_p["conv"], st_p["ckv"], st_p["kpe"],
            st_s["sb_k"], st_s["sb_v"], st_s["conv"], st_s["ckv"], st_s["kpe"], st_s["sgu_v"])
```

```python
import functools
import math

import numpy as np
import jax
import jax.numpy as jnp
from jax import lax
from jax.experimental import pallas as pl
from jax.experimental.pallas import tpu as pltpu

F32 = jnp.float32
BF16 = jnp.bfloat16

EPS = 1e-6
D_MODEL = 1024
CHUNK = 64
SB_HEADS = 8
SB_HEAD_DIM = 64
SB_WIDTH = SB_HEADS * SB_HEAD_DIM
SB_SCALE = 1.0 / math.sqrt(SB_HEAD_DIM)
CONV_DIM = D_MODEL // 2
CONV_W = 3
SGU_CHUNK = 128
SGU_GROUPS = 4
SGU_DIM = D_MODEL // 2
SGU_GROUP_DIM = SGU_DIM // SGU_GROUPS
MLA_HEADS = 8
Q_LORA = 384
KV_LORA = 256
NOPE_DIM = 64
ROPE_DIM = 32
V_DIM = 64
ROPE_THETA = 10000.0
MLA_SCALE = 1.0 / math.sqrt(NOPE_DIM + ROPE_DIM)
D_FF = 4 * D_MODEL
FF_CHUNK = 1024

LANES = 128
HEAD_PAIRS = SB_HEADS // 2
MLA_QK = 2 * KV_LORA
ODD_IN_PAD =2 * SGU_DIM + Q_LORA + KV_LORA + LANES
VMEM_LIMIT = 56 * 1024 * 1024
NEG_INF = -1e30
LOG2_E = math.log2(math.e)

ROW_TILE = 512
ODD_ROW_TILE = 1024
SB_TQ, SB_TK = 256, 128
MLA_TQ, MLA_TK = 256, 256
MLA_ROW_CHUNK = 64
MLA_STREAMS = 1


def _cparams(sem):
    return pltpu.CompilerParams(dimension_semantics=sem, vmem_limit_bytes=VMEM_LIMIT)


def _const_spec(shape):
    nd = len(shape)
    return pl.BlockSpec(shape, lambda *_: (0,) * nd, pipeline_mode=pl.Buffered(1))


def _rms(x, g):
    return x * lax.rsqrt(jnp.mean(x * x, axis=-1, keepdims=True) + EPS) * g


def _dot(a, b):
    return jnp.dot(a, b, preferred_element_type=F32)


def _dot_nt(a, b):
    return lax.dot_general(a, b, (((1,), (1,)), ((), ())), preferred_element_type=F32)


def _log_sigmoid(z):
    neg_abs = lax.bitcast_convert_type(lax.bitcast_convert_type(z, jnp.uint32) | jnp.uint32(0x80000000), F32)
    return jnp.minimum(z, 0.0) - jnp.log(1.0 + jnp.exp(neg_abs))


def _split_hi_lo(x):
    hi = x.astype(BF16)
    lo = (x - hi.astype(F32)).astype(BF16)
    return hi, lo


def _rope_block(x, cos, sin):
    half = ROPE_DIM // 2
    lane = lax.broadcasted_iota(jnp.int32, x.shape, 1)
    partner = jnp.where(lane % ROPE_DIM < half, pltpu.roll(x, LANES - half, 1), pltpu.roll(x, half, 1))
    return x * cos + partner * sin


def _proj_even_body(*refs, tm, seq_len, whole_seqs):
    if whole_seqs:
        (x_ref, g_ref, w_ref, wc_ref, e1_ref, e2_ref,
         q_ref, k_ref, v_ref, kb_ref, vb_ref, gc_ref, ci_ref) = refs
    else:
        (x_ref, g_ref, w_ref, wc_ref, wkv_ref,
         q_ref, k_ref, v_ref, kb_ref, vb_ref, gc_ref, cs_ref, tail_ref) = refs
    h = _rms(x_ref[...], g_ref[...]).astype(BF16)

    def proj(j):
        return _dot(h, w_ref[:, j * SB_WIDTH:(j + 1) * SB_WIDTH])

    q_ref[...] = (proj(0) * SB_SCALE).astype(BF16)
    if whole_seqs:
        k = proj(1)
        v = proj(2)
        k_ref[...] = k
        v_ref[...] = v
        kb_ref[...] = k.astype(BF16)
        vb_ref[...] = v.astype(BF16)
    else:
        k = _dot_nt(wkv_ref[:SB_WIDTH, :], h)
        v = _dot_nt(wkv_ref[SB_WIDTH:, :], h)
        k_ref[0] = k
        v_ref[0] = v
        kb_ref[0] = k.astype(BF16)
        vb_ref[0] = v.astype(BF16)
    g_post = proj(3)
    ci = proj(4) * proj(5)
    r1 = pltpu.roll(ci, 1, 0)
    r2 = pltpu.roll(ci, 2, 0)
    row = lax.broadcasted_iota(jnp.int32, (tm, 1), 0)
    if whole_seqs:
        tpos = row % seq_len
        s1 = jnp.where(tpos < 1, e1_ref[...], r1)
        s2 = jnp.where(tpos < 2, e2_ref[...], r2)
        ci_ref[...] = ci
    else:
        first = (pl.program_id(0) % (seq_len // tm)) == 0

        @pl.when(first)
        def _():
            tail_ref[...] = jnp.zeros_like(tail_ref)

        t1 = tail_ref[7:8, :]
        t2 = tail_ref[6:7, :]
        s1 = jnp.where(row == 0, t1, r1)
        s2 = jnp.where(row == 0, t2, jnp.where(row == 1, t1, r2))
        tail_ref[...] = ci[tm - 8:, :]
        cs_ref[0] = ci[tm - (CONV_W - 1):, :]
    conv = wc_ref[0:1, :] * s2 + wc_ref[1:2, :] * s1 + wc_ref[2:3, :] * ci
    gc_ref[...] = (g_post * conv).astype(BF16)


def _proj_even(x, g, w_in, w_kv_t, w_conv, conv_prev, *, batch, seq_len, tm):
    n = x.shape[0]
    whole = tm % seq_len == 0
    grid = (n // tm,)
    row_spec = lambda w: pl.BlockSpec((tm, w), lambda i: (i, 0))
    in_specs = [row_spec(D_MODEL), _const_spec((1, D_MODEL)), _const_spec(w_in.shape), _const_spec(w_conv.shape)]
    args = [x, g, w_in, w_conv]
    if whole:
        kv_shape, kv_spec = (n, SB_WIDTH), row_spec(SB_WIDTH)
    else:
        seq_tiles = seq_len // tm
        kv_shape = (batch, SB_WIDTH, seq_len)
        kv_spec = pl.BlockSpec((1, SB_WIDTH, tm), lambda i: (i // seq_tiles, 0, i % seq_tiles))
    out_shape = [jax.ShapeDtypeStruct((n, SB_WIDTH), BF16), jax.ShapeDtypeStruct(kv_shape, F32),
                 jax.ShapeDtypeStruct(kv_shape, F32), jax.ShapeDtypeStruct(kv_shape, BF16),
                 jax.ShapeDtypeStruct(kv_shape, BF16), jax.ShapeDtypeStruct((n, CONV_DIM), BF16)]
    out_specs = [row_spec(SB_WIDTH)] + [kv_spec] * 4 + [row_spec(CONV_DIM)]
    scratch = []
    if whole:
        if conv_prev is None:
            conv_prev = jnp.zeros((batch, CONV_W - 1, CONV_DIM), F32)
        e1 =jnp.zeros((batch, seq_len, CONV_DIM), F32).at[:, 0].set(conv_prev[:, 1])
        e2 = jnp.zeros((batch, seq_len, CONV_DIM), F32).at[:, 0].set(conv_prev[:, 0]).at[:, 1].set(conv_prev[:, 1])
        args += [e1.reshape(n, CONV_DIM), e2.reshape(n, CONV_DIM)]
        in_specs += [row_spec(CONV_DIM), row_spec(CONV_DIM)]
        out_shape.append(jax.ShapeDtypeStruct((n, CONV_DIM), F32))
        out_specs.append(row_spec(CONV_DIM))
    else:
        assert seq_len % tm == 0 and conv_prev is None
        args.append(w_kv_t)
        in_specs.append(_const_spec(w_kv_t.shape))
        out_shape.append(jax.ShapeDtypeStruct((batch, CONV_W - 1, CONV_DIM), F32))
        out_specs.append(pl.BlockSpec((1, CONV_W - 1, CONV_DIM), lambda i: (i // seq_tiles, 0, 0)))
        scratch.append(pltpu.VMEM((8, CONV_DIM), F32))
    outs = pl.pallas_call(
        functools.partial(_proj_even_body, tm=tm, seq_len=seq_len, whole_seqs=whole),
        grid=grid, in_specs=in_specs, out_specs=out_specs, out_shape=out_shape, scratch_shapes=scratch,
        compiler_params=_cparams(("arbitrary",)), name="proj_even")(*args)
    q, k, v, kb, vb, gc, last = outs
    if whole:
        last = last.reshape(batch, seq_len, CONV_DIM)[:, seq_len - (CONV_W - 1):]
    return q, k, v, kb, vb, gc, last


def _suffix_rhs(tk):
    j = np.arange(2 * tk)[:, None] % tk
    c = np.arange(2 * tk)[None, :]
    return jnp.asarray(np.where(c < tk, j > c, True), dtype=BF16)


def _sb_prompt_body(q_ref, k_ref, v_ref, r_ref, o_ref, carry_ref, acc_ref, *, tq, tk):
    qi = pl.program_id(1)
    ratio = tq // tk
    even = lax.broadcasted_iota(jnp.int32, (LANES, tk), 0) < SB_HEAD_DIM
    rhs = r_ref[...]
    carry_ref[...] = jnp.zeros_like(carry_ref)
    acc_ref[...] = jnp.zeros_like(acc_ref)
    qs = [q_ref[:, p * LANES:(p + 1) * LANES] for p in range(HEAD_PAIRS)]

    def block_diag(blk):
        zero = jnp.zeros_like(blk)
        return jnp.concatenate([jnp.where(even, blk, zero), jnp.where(even, zero, blk)], axis=1)

    def tile(kj, masked, r0=0):
        start = pl.multiple_of(kj * tk, tk)
        if masked:
            q_pos = qi * tq + r0 + lax.broadcasted_iota(jnp.int32, (tq - r0, tk), 0)
            k_pos = kj * tk + lax.broadcasted_iota(jnp.int32, (tq - r0, tk), 1)
            visible = k_pos < q_pos
        zs = [_dot(qs[p][r0:], block_diag(k_ref[0, p * LANES:(p + 1) * LANES, pl.ds(start, tk)]))
              for p in range(HEAD_PAIRS)]
        log_betas, splits = [], []
        for hd in range(SB_HEADS):
            z = zs[hd // 2][:, (hd % 2) * tk:(hd % 2 + 1) * tk]
            log_beta = _log_sigmoid(z)
            log_1m = log_beta - z
            if masked:
                log_1m = jnp.where(visible, log_1m, 0.0)
            hi, lo = _split_hi_lo(log_1m)
            log_betas.append(log_beta)
            splits.append(jnp.concatenate([hi, lo], axis=1))
        sums = [_dot(sp, rhs) for sp in splits]
        ws = []
        for hd in range(SB_HEADS):
            carry = carry_ref[hd, r0:, :]
            w = jnp.exp(log_betas[hd] + sums[hd][:, :tk] + carry)
            if masked:
                w = jnp.where(visible, w, 0.0)
            carry_ref[hd, r0:, :] = carry + sums[hd][:, tk:]
            ws.append(w.astype(BF16))
        for p in range(HEAD_PAIRS):
            vbd = block_diag(v_ref[0, p * LANES:(p + 1) * LANES, pl.ds(start, tk)])
            acc_ref[p, r0:, :] += _dot_nt(jnp.concatenate([ws[2 * p], ws[2 * p + 1]], axis=1), vbd)

    for t in reversed(range(ratio)):
        tile(qi * ratio + t, True, r0=t * tk)

    def body(it, c):
        tile(qi * ratio - 1 - 2 * it, False)
        tile(qi * ratio - 2 - 2 * it, False)
        return c

    lax.fori_loop(0, qi * (ratio // 2), body, 0)
    for p in range(HEAD_PAIRS):
        o_ref[:, p * LANES:(p + 1) * LANES] = acc_ref[p].astype(BF16)


def _sb_prompt(q, kb, vb, *, batch, seq_len, tq, tk):
    assert (tq // tk) % 2 == 0 and tq % tk == 0 and seq_len % tq == 0
    n = q.shape[0]
    nq = seq_len // tq
    rhs = _suffix_rhs(tk)
    q_spec = pl.BlockSpec((tq, SB_WIDTH), lambda b, i: (b * nq + i, 0))
    kv_spec = pl.BlockSpec((1, SB_WIDTH, seq_len), lambda b, i: (b, 0, 0))
    return pl.pallas_call(
        functools.partial(_sb_prompt_body, tq=tq, tk=tk),
        grid=(batch, nq),
        in_specs=[q_spec, kv_spec, kv_spec, _const_spec(rhs.shape)],
        out_specs=q_spec,
        out_shape=jax.ShapeDtypeStruct((n, SB_WIDTH), BF16),
        scratch_shapes=[pltpu.VMEM((SB_HEADS, tq, LANES), F32), pltpu.VMEM((HEAD_PAIRS, tq, LANES), F32)],
        compiler_params=_cparams(("arbitrary", "arbitrary")), name="sb_attn_prompt")(q, kb, vb, rhs)


def _sb_sample_body(q_ref, kn_ref, vn_ref, kc_ref, vc_ref, l_ref, o_ref, qh_ref, carry_ref, acc_ref,
                    *, dec, kblk, tk):
    j = pl.program_id(1)
    rows = SB_HEADS * dec
    rhs = l_ref[...]

    def tile(kt, vt, visible):
        z = _dot(qh_ref[...], kt)
        log_beta = _log_sigmoid(z)
        log_1m = log_beta - z
        if visible is not None:
            log_1m = jnp.where(visible, log_1m, 0.0)
        hi, lo = _split_hi_lo(log_1m)
        sums = _dot(jnp.concatenate([hi, lo], axis=1), rhs)
        w = jnp.exp(log_beta + sums[:, :tk] + carry_ref[...])
        if visible is not None:
            w = jnp.where(visible, w, 0.0)
        acc_ref[...] += _dot_nt(w.astype(BF16), vt)
        carry_ref[...] += sums[:, tk:]

    def new_keys_t(x):
        x = jnp.concatenate([x.astype(F32), jnp.zeros((tk - dec, SB_WIDTH), F32)], axis=0)
        return x.T.astype(BF16)

    @pl.when(j == 0)
    def _():
        q = q_ref[...]
        qt = jnp.concatenate([q] * SB_HEADS, axis=0)
        rr = lax.broadcasted_iota(jnp.int32, (rows, SB_WIDTH), 0)
        cc = lax.broadcasted_iota(jnp.int32, (rows, SB_WIDTH), 1)
        qh_ref[...] = jnp.where(rr // dec == cc // SB_HEAD_DIM, qt, jnp.zeros_like(qt))
        carry_ref[...] = jnp.zeros_like(carry_ref)
        acc_ref[...] = jnp.zeros_like(acc_ref)
        query_t = lax.broadcasted_iota(jnp.int32, (rows, tk), 0) % dec
        key_i = lax.broadcasted_iota(jnp.int32, (rows, tk), 1)
        tile(new_keys_t(kn_ref[...]), new_keys_t(vn_ref[...]),
             key_i < query_t)

    for s in reversed(range(kblk // tk)):
        tile(kc_ref[0, :, s * tk:(s + 1) * tk].astype(BF16), vc_ref[0, :, s * tk:(s + 1) * tk].astype(BF16), None)

    @pl.when(j == pl.num_programs(1) - 1)
    def _():
        cc = lax.broadcasted_iota(jnp.int32, (dec, SB_WIDTH), 1)
        out = jnp.zeros((dec, SB_WIDTH), F32)
        for h in range(SB_HEADS):
            out = out + jnp.where(cc // SB_HEAD_DIM == h, acc_ref[h * dec:(h + 1) * dec, :], 0.0)
        o_ref[...] = out.astype(BF16)


def _sb_sample(q, kb, vb, cache_kt, cache_vt, *, batch, dec, kblk=1024, tk=256):
    n = q.shape[0]
    past = cache_kt.shape[2]
    nkb = past // kblk
    rows = SB_HEADS * dec
    rhs = _suffix_rhs(tk)
    new_spec = pl.BlockSpec((dec, SB_WIDTH), lambda b, j: (b, 0))
    cache_spec = pl.BlockSpec((1, SB_WIDTH, kblk), lambda b, j: (b, 0, nkb - 1 - j))
    return pl.pallas_call(
        functools.partial(_sb_sample_body, dec=dec, kblk=kblk, tk=tk),
        grid=(batch, nkb),
        in_specs=[new_spec, new_spec, new_spec, cache_spec, cache_spec, _const_spec(rhs.shape)],
        out_specs=new_spec,
        out_shape=jax.ShapeDtypeStruct((n, SB_WIDTH), BF16),
        scratch_shapes=[pltpu.VMEM((rows, SB_WIDTH), BF16), pltpu.VMEM((rows, tk), F32),
                        pltpu.VMEM((rows, SB_WIDTH), F32)],
        compiler_params=_cparams(("arbitrary", "arbitrary")), name="sb_attn_sample")(
            q, kb, vb, cache_kt, cache_vt, rhs)


def _proj_odd_body(*refs, tm, gate_len, emit_vn, kpe_transposed):
    (x_ref, g_ref, w_ref, lng_ref, lnb_ref, ws_ref, bs_ref, qg_ref, kvg_ref, wqn_ref, wqp_ref,
     cos_ref, sin_ref) = refs[:13]
    outs = refs[13:]
    if emit_vn:
        sgu_ref, vn_ref, qn_ref, qp_ref, ckv_ref, ckvb_ref, kpe_ref, kpeb_ref = outs
    else:
        sgu_ref, qn_ref, qp_ref, ckv_ref, ckvb_ref, kpe_ref, kpeb_ref = outs
    h = _rms(x_ref[...], g_ref[...]).astype(BF16)
    o_v, o_q, o_kv, o_pe = SGU_DIM, 2 * SGU_DIM, 2 * SGU_DIM + Q_LORA, 2 * SGU_DIM + Q_LORA + KV_LORA
    u = _dot(h, w_ref[:, :o_v])
    v = _dot(h, w_ref[:, o_v:o_q])
    mu = jnp.mean(v, axis=-1, keepdims=True)
    vc = v - mu
    var = jnp.mean(vc * vc, axis=-1, keepdims=True)
    vn = vc * lax.rsqrt(var + EPS) * lng_ref[...] + lnb_ref[...]
    if emit_vn:
        vn_ref[...] = vn
    vnb = vn.astype(BF16)
    rr = lax.broadcasted_iota(jnp.int32, (SGU_CHUNK, SGU_CHUNK), 0)
    cc = lax.broadcasted_iota(jnp.int32, (SGU_CHUNK, SGU_CHUNK), 1)
    causal = (rr // gate_len == cc // gate_len) & (cc <= rr)
    for g in range(SGU_GROUPS):
        gs = slice(g * SGU_GROUP_DIM, (g + 1) * SGU_GROUP_DIM)
        wg = jnp.where(causal, ws_ref[g], 0.0).astype(BF16)
        for c in range(tm // SGU_CHUNK):
            rs = slice(c * SGU_CHUNK, (c + 1) * SGU_CHUNK)
            s = _dot(wg, vnb[rs, gs]) + bs_ref[:, gs]
            sgu_ref[rs, gs] = (u[rs, gs] * s).astype(BF16)
    cq = _rms(_dot(h, w_ref[:, o_q:o_kv]), qg_ref[...]).astype(BF16)
    qn_ref[...] = _dot(cq, wqn_ref[...]).astype(BF16)
    cos = cos_ref[...]
    sin = sin_ref[...]
    for blk in range(MLA_HEADS * ROPE_DIM // LANES):
        bs = slice(blk * LANES, (blk + 1) * LANES)
        qp_ref[:, bs] = _rope_block(_dot(cq, wqp_ref[:, bs]), cos, sin).astype(BF16)
    ckv = _rms(_dot(h, w_ref[:, o_kv:o_pe]), kvg_ref[...])
    ckv_ref[...] = ckv
    ckvb_ref[...] = ckv.astype(BF16)
    kpe = _rope_block(_dot(h, w_ref[:, o_pe:]), cos, sin)
    if kpe_transposed:
        kpe_ref[0] = kpe.T[:ROPE_DIM, :]
    else:
        kpe_ref[...] = kpe[:, :ROPE_DIM]
    kpeb_ref[...] = kpe.astype(BF16)


def _proj_odd(x, g, w_in, ln_g, ln_b, w_s, b_s, qg, kvg, wqn, wqp, cos, sin, *, batch, seq_len, tm, gate_len,
              emit_vn):
    n = x.shape[0]
    pos_tiles = cos.shape[0] // tm
    kpe_transposed = tm % seq_len != 0
    if kpe_transposed:
        seq_tiles = seq_len // tm
        kpe_shape = (batch, ROPE_DIM, seq_len)
        kpe_spec = pl.BlockSpec((1, ROPE_DIM, tm), lambda i: (i // seq_tiles, 0, i % seq_tiles))
    else:
        kpe_shape, kpe_spec = (n, ROPE_DIM), pl.BlockSpec((tm, ROPE_DIM), lambda i: (i, 0))
    row_spec = lambda w: pl.BlockSpec((tm, w), lambda i: (i, 0))
    tab_spec = pl.BlockSpec((tm, LANES), lambda i: (i % pos_tiles, 0))
    consts = [g, w_in, ln_g, ln_b, w_s, b_s, qg, kvg, wqn, wqp]
    out_shape = [jax.ShapeDtypeStruct((n, SGU_DIM), BF16)]
    out_specs = [row_spec(SGU_DIM)]
    if emit_vn:
        out_shape.append(jax.ShapeDtypeStruct((n, SGU_DIM), F32))
        out_specs.append(row_spec(SGU_DIM))
    out_shape += [jax.ShapeDtypeStruct((n, MLA_HEADS * NOPE_DIM), BF16),
                  jax.ShapeDtypeStruct((n, MLA_HEADS * ROPE_DIM), BF16),
                  jax.ShapeDtypeStruct((n, KV_LORA), F32), jax.ShapeDtypeStruct((n, KV_LORA), BF16),
                  jax.ShapeDtypeStruct(kpe_shape, F32), jax.ShapeDtypeStruct((n, LANES), BF16)]
    out_specs += [row_spec(MLA_HEADS * NOPE_DIM), row_spec(MLA_HEADS * ROPE_DIM),
                  row_spec(KV_LORA), row_spec(KV_LORA), kpe_spec, row_spec(LANES)]
    return pl.pallas_call(
        functools.partial(_proj_odd_body, tm=tm, gate_len=gate_len, emit_vn=emit_vn, kpe_transposed=kpe_transposed),
        grid=(n // tm,),
        in_specs=[row_spec(D_MODEL)] + [_const_spec(c.shape) for c in consts] + [tab_spec, tab_spec],
        out_specs=out_specs, out_shape=out_shape,
        compiler_params=_cparams(("arbitrary",)), name="proj_odd")(x, *consts, cos, sin)


def _mla_queries(qn, qpe, wuk_ref, qcat_ref, tq):
    lane = lax.broadcasted_iota(jnp.int32, (tq, LANES), 1)
    low = lane < NOPE_DIM
    qpe = qpe.astype(F32)
    qcat_ref[:, KV_LORA:] = jnp.zeros((MLA_HEADS * tq, MLA_QK - KV_LORA), BF16)
    for hd in range(MLA_HEADS):
        p = hd // 2
        rs = slice(hd * tq, (hd + 1) * tq)
        pair = qn[:, p * LANES:(p + 1) * LANES]
        qm = jnp.where(low if hd % 2 == 0 else jnp.logical_not(low), pair, jnp.zeros_like(pair))
        qcat_ref[rs, :KV_LORA] = _dot(qm, wuk_ref[p]).astype(BF16)
        qcat_ref[rs, KV_LORA:KV_LORA + ROPE_DIM] = qpe[:, hd * ROPE_DIM:(hd + 1) * ROPE_DIM].astype(BF16)


def _lane_tile(x, width):
    return jnp.concatenate([x] * (width // LANES), axis=1)


def _mla_scores(qcat_ref, kcat, sc):
    sc[0][...] = _dot_nt(qcat_ref[...], kcat)


def _mla_softmax(sc, visible_fn):
    s_ref, p_ref, m_ref, l_ref, acc_ref = sc
    rows, tk = s_ref.shape
    for c in range(rows // MLA_ROW_CHUNK):
        rs = slice(c * MLA_ROW_CHUNK, (c + 1) * MLA_ROW_CHUNK)
        s = s_ref[rs, :] * (MLA_SCALE * LOG2_E)
        if visible_fn is not None:
            s = jnp.where(visible_fn(c), s, NEG_INF)
        m_prev = m_ref[rs, :]
        m_new = jnp.maximum(m_prev, jnp.max(s, axis=-1, keepdims=True))
        alpha = jnp.exp2(m_prev - m_new)
        p = jnp.exp2(s - _lane_tile(m_new, tk))
        l_ref[rs, :] = alpha * l_ref[rs, :] + jnp.sum(p, axis=-1, keepdims=True)
        m_ref[rs, :] = m_new
        p_ref[rs, :] = p.astype(BF16)
        acc_ref[rs, :] = acc_ref[rs, :] * _lane_tile(alpha, KV_LORA)


def _mla_pv(ck, sc):
    sc[4][...] += _dot(sc[1][...], ck)


def _mla_tile(qcat_ref, kcat, sc, visible_fn):
    _mla_scores(qcat_ref, kcat, sc)
    _mla_softmax(sc, visible_fn)
    _mla_pv(kcat[:, :KV_LORA], sc)


def _mla_finish(sc, wuv_ref, o_ref, tq):
    l_ref, acc_ref = sc[3], sc[4]
    o_lat = (acc_ref[...] / _lane_tile(l_ref[...], KV_LORA)).astype(BF16)
    for p in range(MLA_HEADS // 2):
        h0, h1 = 2 * p, 2 * p + 1
        o_ref[:, p * LANES:(p + 1) * LANES] = (
            _dot(o_lat[h0 * tq:(h0 + 1) * tq], wuv_ref[h0]) + _dot(o_lat[h1 * tq:(h1 + 1) * tq], wuv_ref[h1])
        ).astype(BF16)


def _mla_init(sc):
    m_ref, l_ref, acc_ref = sc[2], sc[3], sc[4]
    m_ref[...] = jnp.full_like(m_ref, -jnp.inf)
    l_ref[...] = jnp.zeros_like(l_ref)
    acc_ref[...] = jnp.zeros_like(acc_ref)


def _mla_prompt_body(qn_ref, qp_ref, ckv_ref, kpe_ref, wuk_ref, wuv_ref, o_ref, *scratch, tq, tk, streams):
    qi = pl.program_id(1)
    rows = MLA_HEADS * tq
    per = len(scratch) // streams
    qcats = [scratch[st * per] for st in range(streams)]
    scs = [scratch[st * per + 1:(st + 1) * per] for st in range(streams)]
    for st in range(streams):
        _mla_queries(qn_ref[st], qp_ref[st], wuk_ref, qcats[st], tq)
        _mla_init(scs[st])

    n_full = (qi * tq) // tk
    col = lax.broadcasted_iota(jnp.int32, (MLA_ROW_CHUNK, tk), 1)

    def visible(c):
        q_chunk_end = qi * tq + ((c * MLA_ROW_CHUNK) % tq) // CHUNK * CHUNK + CHUNK
        return col < q_chunk_end - n_full * tk

    def tile(kj, visible_fn):
        start = pl.multiple_of(kj * tk, tk)
        kcats = [jnp.concatenate([ckv_ref[st, pl.ds(start, tk), :], kpe_ref[st, pl.ds(start, tk), :],
                                  jnp.zeros((tk, MLA_QK - KV_LORA - LANES), BF16)], axis=1) for st in range(streams)]
        for st in range(streams):
            _mla_scores(qcats[st], kcats[st], scs[st])
        for st in range(streams):
            _mla_softmax(scs[st], visible_fn)
        for st in range(streams):
            _mla_pv(kcats[st][:, :KV_LORA], scs[st])

    def body(kj, c):
        tile(kj, None)
        return c

    lax.fori_loop(0, n_full, body, 0)
    tile(n_full, visible)
    for st in range(streams):
        _mla_finish(scs[st], wuv_ref, o_ref.at[st], tq)


def _mla_scratch(rows, tk):
    return [pltpu.VMEM((rows, MLA_QK), BF16), pltpu.VMEM((rows, tk), F32), pltpu.VMEM((rows, tk), BF16),
            pltpu.VMEM((rows, LANES), F32), pltpu.VMEM((rows, LANES), F32), pltpu.VMEM((rows, KV_LORA), F32)]


def _mla_prompt(qn, qp, ckvb, kpeb, wuk, wuv, *, batch, seq_len, tq, tk):
    assert tq % CHUNK == 0 and CHUNK % MLA_ROW_CHUNK == 0 and seq_len % tq == 0 and seq_len % tk == 0
    n = qn.shape[0]
    nq = seq_len // tq
    width = MLA_HEADS * V_DIM
    streams = MLA_STREAMS if batch % MLA_STREAMS == 0 else 1
    seq3 = lambda a: a.reshape(batch, seq_len, a.shape[-1])
    out = pl.pallas_call(
        functools.partial(_mla_prompt_body, tq=tq, tk=tk, streams=streams),
        grid=(batch // streams, nq),
        in_specs=[pl.BlockSpec((streams, tq, MLA_HEADS * NOPE_DIM), lambda g, i: (g, i, 0)),
                  pl.BlockSpec((streams, tq, MLA_HEADS * ROPE_DIM), lambda g, i: (g, i, 0)),
                  pl.BlockSpec((streams, seq_len, KV_LORA), lambda g, i: (g, 0, 0)),
                  pl.BlockSpec((streams, seq_len, LANES), lambda g, i: (g, 0, 0)),
                  _const_spec(wuk.shape), _const_spec(wuv.shape)],
        out_specs=pl.BlockSpec((streams, tq, width), lambda g, i: (g, i, 0)),
        out_shape=jax.ShapeDtypeStruct((batch, seq_len, width), BF16),
        scratch_shapes=_mla_scratch(MLA_HEADS * tq, tk) * streams,
        compiler_params=_cparams(("arbitrary", "arbitrary")), name="mla_attn_prompt")(
            seq3(qn), seq3(qp), seq3(ckvb), seq3(kpeb), wuk, wuv)
    return out.reshape(n, width)


def _mla_sample_body(qn_ref, qp_ref, cn_ref, pn_ref, cc_ref, pc_ref, wuk_ref, wuv_ref, o_ref, kcat_ref, qcat_ref,
                     *sc, dec, past, kblk, tk):
    j = pl.program_id(1)
    rows = MLA_HEADS * dec

    @pl.when(j == 0)
    def _():
        _mla_queries(qn_ref[...], qp_ref[...], wuk_ref, qcat_ref, dec)
        _mla_init(sc)
        kcat_ref[:, KV_LORA:] = jnp.zeros((tk, MLA_QK - KV_LORA), BF16)
        new = jnp.concatenate([cn_ref[...], pn_ref[...], jnp.zeros((dec, MLA_QK - KV_LORA - LANES), BF16)], axis=1)
        kcat = jnp.concatenate([new, jnp.zeros((tk - dec, MLA_QK), BF16)], axis=0)

        def visible(c):
            col = lax.broadcasted_iota(jnp.int32, (MLA_ROW_CHUNK, tk), 1)
            row = c * MLA_ROW_CHUNK + lax.broadcasted_iota(jnp.int32, (MLA_ROW_CHUNK, tk), 0)
            return (col < dec) & ((past + col) // CHUNK <= (past + row % dec) // CHUNK)

        _mla_tile(qcat_ref, kcat, sc, visible)

    def body(it, c):
        start = pl.multiple_of(it * tk, tk)
        kcat_ref[:, :KV_LORA] = cc_ref[0, pl.ds(start, tk), :].astype(BF16)
        kp_t = jnp.concatenate([pc_ref[0, :, pl.ds(start, tk)], jnp.zeros((LANES - ROPE_DIM, tk), F32)], axis=0)
        kcat_ref[:, KV_LORA:KV_LORA + LANES] = kp_t.T.astype(BF16)
        _mla_tile(qcat_ref, kcat_ref[...], sc, None)
        return c

    lax.fori_loop(0, kblk // tk, body, 0)

    @pl.when(j == pl.num_programs(1) - 1)
    def _():
        _mla_finish(sc, wuv_ref, o_ref, dec)


def _mla_sample(qn, qp, ckvb, kpeb, cache_ckv, cache_kpe_t, wuk, wuv, *, batch, dec, kblk=1024, tk=512):
    n = qn.shape[0]
    past = cache_ckv.shape[1]
    width = MLA_HEADS * V_DIM
    return pl.pallas_call(
        functools.partial(_mla_sample_body, dec=dec, past=past, kblk=kblk, tk=tk),
        grid=(batch, past // kblk),
        in_specs=[pl.BlockSpec((dec, MLA_HEADS * NOPE_DIM), lambda b, j: (b, 0)),
                  pl.BlockSpec((dec, MLA_HEADS * ROPE_DIM), lambda b, j: (b, 0)),
                  pl.BlockSpec((dec, KV_LORA), lambda b, j: (b, 0)),
                  pl.BlockSpec((dec, LANES), lambda b, j: (b, 0)),
                  pl.BlockSpec((1, kblk, KV_LORA), lambda b, j: (b, j, 0)),
                  pl.BlockSpec((1, ROPE_DIM, kblk), lambda b, j: (b, 0, j)),
                  _const_spec(wuk.shape), _const_spec(wuv.shape)],
        out_specs=pl.BlockSpec((dec, width), lambda b, j: (b, 0)),
        out_shape=jax.ShapeDtypeStruct((n, width), BF16),
        scratch_shapes=[pltpu.VMEM((tk, MLA_QK), BF16)] + _mla_scratch(MLA_HEADS * dec, tk),
        compiler_params=_cparams(("arbitrary", "arbitrary")), name="mla_attn_sample")(
            qn, qp, ckvb, kpeb, cache_ckv, cache_kpe_t, wuk, wuv)


def _out_ffn_body(a_ref, b_ref, x_ref, woa_ref, wob_ref, gpost_ref, gpre_ref, wup_ref, wdn_ref, gfpost_ref, o_ref):
    mixed = _dot(a_ref[...], woa_ref[...]) + _dot(b_ref[...], wob_ref[...])
    x1 = x_ref[...] + _rms(mixed, gpost_ref[...])
    h = _rms(x1, gpre_ref[...]).astype(BF16)
    down = jnp.zeros_like(x1)
    for c in range(D_FF // FF_CHUNK):
        up = _dot(h, wup_ref[:, c * FF_CHUNK:(c + 1) * FF_CHUNK])
        act = jnp.square(jnp.maximum(up, 0.0)).astype(BF16)
        down = down + _dot(act, wdn_ref[c * FF_CHUNK:(c + 1) * FF_CHUNK, :])
    o_ref[...] = x1 + _rms(down, gfpost_ref[...])


def _out_ffn(a, b, x, woa, wob, g_post, g_pre, w_up, w_down, g_fpost, *, tm):
    n = x.shape[0]
    row_spec = lambda w: pl.BlockSpec((tm, w), lambda i: (i, 0))
    consts = [woa, wob, g_post, g_pre, w_up, w_down, g_fpost]
    return pl.pallas_call(
        _out_ffn_body,
        grid=(n // tm,),
        in_specs=[row_spec(a.shape[1]), row_spec(b.shape[1]), row_spec(D_MODEL)] + [_const_spec(c.shape) for c in consts],
        out_specs=row_spec(D_MODEL),
        out_shape=jax.ShapeDtypeStruct((n, D_MODEL), F32),
        compiler_params=_cparams(("arbitrary",)), name="out_ffn")(a, b, x, *consts)


def _rope_tables(pos, reps):
    half = ROPE_DIM // 2
    inv = ROPE_THETA ** (-jnp.arange(half, dtype=F32) / half)
    ang = pos.astype(F32)[:, None] * inv[None, :]
    cos = jnp.tile(jnp.concatenate([jnp.cos(ang), jnp.cos(ang)], axis=1), (reps, LANES // ROPE_DIM))
    sin = jnp.tile(jnp.concatenate([-jnp.sin(ang), jnp.sin(ang)], axis=1), (reps, LANES // ROPE_DIM))
    return cos, sin


def _prep_even(p, j):
    w_in = p["even_w_in"][j]
    return dict(w_in=w_in.astype(BF16), w_kv_t=w_in[:, SB_WIDTH:3 * SB_WIDTH].T.astype(BF16), w_conv=p["even_w_conv"][j],
                woa=p["even_w_out"][j, :SB_WIDTH].astype(BF16), wob=p["even_w_out"][j, SB_WIDTH:].astype(BF16))


def _prep_odd(p, j, gate_len):
    w_in = p["odd_w_in"][j]
    w_in = jnp.pad(w_in, ((0, 0), (0, ODD_IN_PAD - w_in.shape[1]))).astype(BF16)
    reps = SGU_CHUNK // gate_len
    w_s = jnp.tile(p["sgu_w_s"][j, :, :gate_len, :gate_len], (1, reps, reps))
    b_s = jnp.tile(p["sgu_b_s"][j, :, :gate_len], (1, reps))
    b_s = jnp.repeat(b_s.T, SGU_GROUP_DIM, axis=1)
    w_uq = p["mla_w_uq"][j].reshape(Q_LORA, MLA_HEADS, NOPE_DIM + ROPE_DIM)
    wqn = w_uq[:, :, :NOPE_DIM].reshape(Q_LORA, MLA_HEADS * NOPE_DIM).astype(BF16)
    wqp = w_uq[:, :, NOPE_DIM:].reshape(Q_LORA, MLA_HEADS * ROPE_DIM).astype(BF16)
    wuk = p["mla_w_uk"][j].reshape(MLA_HEADS // 2, 2 * NOPE_DIM, KV_LORA).astype(BF16)
    w_uv = p["mla_w_uv"][j]
    wuv = jnp.stack([jnp.pad(w_uv[h], ((0, 0), ((h % 2) * V_DIM, (1 - h % 2) * V_DIM))) for h in range(MLA_HEADS)])
    return dict(w_in=w_in, ln_g=p["sgu_ln_g"][j][None], ln_b=p["sgu_ln_b"][j][None], w_s=w_s, b_s=b_s,
                qg=p["mla_q_norm_g"][j][None], kvg=p["mla_kv_norm_g"][j][None], wqn=wqn, wqp=wqp, wuk=wuk,
                wuv=wuv.astype(BF16),
                woa=p["odd_w_out"][j, :SGU_DIM].astype(BF16), wob=p["odd_w_out"][j, SGU_DIM:].astype(BF16))


def _run_trunk(x, pos, past, p, *, batch, seq_len):
    depth = p["mix_pre_g"].shape[0]
    n = batch * seq_len
    x = x.reshape(n, D_MODEL)
    is_sample = past is not None
    tm = min(ROW_TILE, n)
    assert n % tm == 0 and (tm % seq_len == 0 or seq_len % tm == 0) and tm % SGU_CHUNK == 0
    gate_len = min(seq_len, SGU_CHUNK)
    tm_odd = ODD_ROW_TILE if seq_len % ODD_ROW_TILE == 0 else tm
    cos, sin = _rope_tables(pos, max(1, tm_odd // seq_len))
    st = {k: [] for k in ("sb_k", "sb_v", "conv", "ckv", "kpe", "sgu_v")}
    for layer in range(depth):
        j = layer // 2
        g_pre = p["mix_pre_g"][layer][None]
        if layer % 2 == 0:
            w = _prep_even(p, j)
            conv_prev = past["conv"][j] if is_sample else None
            q, k, v, kb, vb, b_mix, conv_state = _proj_even(x, g_pre, w["w_in"], w["w_kv_t"], w["w_conv"], conv_prev,
                                                            batch=batch, seq_len=seq_len, tm=tm)
            if is_sample:
                assert tm % seq_len == 0

                def cache_t(c):
                    return jnp.transpose(c, (0, 2, 3, 1)).reshape(batch, SB_WIDTH, -1)

                a_mix = _sb_sample(q, kb, vb, cache_t(past["sb_k"][j]), cache_t(past["sb_v"][j]),
                                   batch=batch, dec=seq_len)
                k, v = (t.reshape(batch, seq_len, SB_HEADS, SB_HEAD_DIM) for t in (k, v))
            else:
                assert seq_len % tm == 0
                a_mix = _sb_prompt(q, kb, vb, batch=batch, seq_len=seq_len, tq=SB_TQ, tk=SB_TK)
                k, v = (jnp.transpose(t.reshape(batch, SB_HEADS, SB_HEAD_DIM, seq_len), (0, 3, 1, 2)) for t in (k, v))
            st["sb_k"].append(k)
            st["sb_v"].append(v)
            st["conv"].append(conv_state)
        else:
            w = _prep_odd(p, j, gate_len)
            outs = _proj_odd(x, g_pre, w["w_in"], w["ln_g"], w["ln_b"], w["w_s"], w["b_s"], w["qg"], w["kvg"],
                             w["wqn"], w["wqp"], cos, sin, batch=batch, seq_len=seq_len, tm=tm_odd, gate_len=gate_len,
                             emit_vn=is_sample)
            if is_sample:
                a_mix, vn, qn, qp, ckv, ckvb, kpe, kpeb = outs
                st["sgu_v"].append(vn.reshape(batch, seq_len, SGU_DIM))
                b_mix = _mla_sample(qn, qp, ckvb, kpeb, past["ckv"][j], jnp.transpose(past["kpe"][j], (0, 2, 1)),
                                    w["wuk"], w["wuv"],
                                    batch=batch, dec=seq_len)
            else:
                a_mix, qn, qp, ckv, ckvb, kpe, kpeb = outs
                b_mix = _mla_prompt(qn, qp, ckvb, kpeb, w["wuk"], w["wuv"], batch=batch, seq_len=seq_len,
                                    tq=MLA_TQ, tk=MLA_TK)
            st["ckv"].append(ckv.reshape(batch, seq_len, KV_LORA))
            st["kpe"].append(jnp.transpose(kpe, (0, 2, 1)) if kpe.ndim == 3 else kpe.reshape(batch, seq_len, ROPE_DIM))
        x = _out_ffn(a_mix, b_mix, x, w["woa"], w["wob"], p["mix_post_g"][layer][None], p["ffn_pre_g"][layer][None],
                     p["ffn_w_up"][layer].astype(BF16), p["ffn_w_down"][layer].astype(BF16),
                     p["ffn_post_g"][layer][None], tm=tm)
    states = {k: jnp.stack(v) for k, v in st.items() if v}
    return x.reshape(batch, seq_len, D_MODEL), states


def kernel(x_prompt, x_sample, cache_sb_k, cache_sb_v, state_conv, cache_mla_ckv, cache_mla_kpe,
           mix_pre_g, mix_post_g, ffn_pre_g, ffn_post_g, even_w_in, even_w_conv, even_w_out,
           odd_w_in, sgu_ln_g, sgu_ln_b, sgu_w_s, sgu_b_s, mla_q_norm_g, mla_kv_norm_g,
           mla_w_uq, mla_w_uk, mla_w_uv, odd_w_out, ffn_w_up, ffn_w_down):
    params = {
        "mix_pre_g": mix_pre_g, "mix_post_g": mix_post_g, "ffn_pre_g": ffn_pre_g, "ffn_post_g": ffn_post_g,
        "even_w_in": even_w_in, "even_w_conv": even_w_conv, "even_w_out": even_w_out,
        "odd_w_in": odd_w_in, "sgu_ln_g": sgu_ln_g, "sgu_ln_b": sgu_ln_b, "sgu_w_s": sgu_w_s,
        "sgu_b_s": sgu_b_s, "mla_q_norm_g": mla_q_norm_g, "mla_kv_norm_g": mla_kv_norm_g,
        "mla_w_uq": mla_w_uq, "mla_w_uk": mla_w_uk, "mla_w_uv": mla_w_uv, "odd_w_out": odd_w_out,
        "ffn_w_up": ffn_w_up, "ffn_w_down": ffn_w_down,
    }
    batch, seq_len, _ = x_prompt.shape
    pos_p = jnp.arange(seq_len, dtype=jnp.int32)
    y_prompt, st_p = _run_trunk(x_prompt, pos_p, None, params, batch=batch, seq_len=seq_len)
    dec_batch, dec_seq, _ = x_sample.shape
    past_len = cache_sb_k.shape[2]
    pos_s = past_len + jnp.arange(dec_seq, dtype=jnp.int32)
    past = {"sb_k": cache_sb_k, "sb_v": cache_sb_v, "conv": state_conv, "ckv": cache_mla_ckv, "kpe": cache_mla_kpe}
    y_sample, st_s = _run_trunk(x_sample, pos_s, past, params, batch=dec_batch, seq_len=dec_seq)
    return (y_prompt, y_sample,
            st_p["sb_k"], st_p["sb_v"], st_p["conv"], st_p["ckv"], st_p["kpe"],
            st_s["sb_k"], st_s["sb_v"], st_s["conv"], st_s["ckv"], st_s["kpe"], st_s["sgu_v"])
```

```python
import functools
import math

import numpy as np
import jax
import jax.numpy as jnp
from jax import lax
from jax.experimental import pallas as pl
from jax.experimental.pallas import tpu as pltpu

F32 = jnp.float32
BF16 = jnp.bfloat16

EPS = 1e-6
D_MODEL = 1024
CHUNK = 64
SB_HEADS = 8
SB_HEAD_DIM = 64
SB_WIDTH = SB_HEADS * SB_HEAD_DIM
SB_SCALE = 1.0 / math.sqrt(SB_HEAD_DIM)
CONV_DIM = D_MODEL // 2
CONV_W = 3
SGU_CHUNK = 128
SGU_GROUPS = 4
SGU_DIM = D_MODEL // 2
SGU_GROUP_DIM = SGU_DIM // SGU_GROUPS
MLA_HEADS = 8
Q_LORA = 384
KV_LORA = 256
NOPE_DIM = 64
ROPE_DIM = 32
V_DIM = 64
ROPE_THETA = 10000.0
MLA_SCALE = 1.0 / math.sqrt(NOPE_DIM + ROPE_DIM)
D_FF = 4 * D_MODEL
FF_CHUNK = 1024
FFN_ROW_GROUPS = 2

LANES = 128
HEAD_PAIRS = SB_HEADS // 2
MLA_QK = 2 * KV_LORA
ODD_IN_PAD =2 * SGU_DIM + Q_LORA + KV_LORA + LANES
VMEM_LIMIT = 56 * 1024 * 1024
NEG_INF = -1e30
LOG2_E = math.log2(math.e)

ROW_TILE = 512
PROJ_ROW_TILE = 1024
SB_TQ, SB_TK = 512, 128
MLA_TQ, MLA_TK = 256, 256
MLA_ROW_CHUNK = 64
MLA_STREAMS = 1


def _cparams(sem):
    return pltpu.CompilerParams(dimension_semantics=sem, vmem_limit_bytes=VMEM_LIMIT)


def _const_spec(shape):
    nd = len(shape)
    return pl.BlockSpec(shape, lambda *_: (0,) * nd, pipeline_mode=pl.Buffered(1))


def _rms(x, g):
    return x * lax.rsqrt(jnp.mean(x * x, axis=-1, keepdims=True) + EPS) * g


def _dot(a, b):
    return jnp.dot(a, b, preferred_element_type=F32)


def _dot_nt(a, b):
    return lax.dot_general(a, b, (((1,), (1,)), ((), ())), preferred_element_type=F32)


def _log_sigmoid(z):
    neg_abs = lax.bitcast_convert_type(lax.bitcast_convert_type(z, jnp.uint32) | jnp.uint32(0x80000000), F32)
    return jnp.minimum(z, 0.0) - jnp.log(1.0 + jnp.exp(neg_abs))


def _split_hi_lo(x):
    hi = x.astype(BF16)
    lo = (x - hi.astype(F32)).astype(BF16)
    return hi, lo


def _rope_block(x, cos, sin):
    half = ROPE_DIM // 2
    lane = lax.broadcasted_iota(jnp.int32, x.shape, 1)
    partner = jnp.where(lane % ROPE_DIM < half, pltpu.roll(x, LANES - half, 1), pltpu.roll(x, half, 1))
    return x * cos + partner * sin


def _proj_even_body(*refs, tm, seq_len, whole_seqs):
    if whole_seqs:
        (x_ref, g_ref, w_ref, wc_ref, e1_ref, e2_ref,
         q_ref, k_ref, v_ref, kb_ref, vb_ref, gc_ref, ci_ref) = refs
    else:
        (x_ref, g_ref, w_ref, wc_ref, wkv_ref,
         q_ref, k_ref, v_ref, kb_ref, vb_ref, gc_ref, cs_ref, tail_ref) = refs
    h = _rms(x_ref[...], g_ref[...]).astype(BF16)

    def proj(j):
        return _dot(h, w_ref[:, j * SB_WIDTH:(j + 1) * SB_WIDTH])

    q_ref[...] = (proj(0) * SB_SCALE).astype(BF16)
    if whole_seqs:
        k = proj(1)
        v = proj(2)
        k_ref[...] = k
        v_ref[...] = v
        kb_ref[...] = k.astype(BF16)
        vb_ref[...] = v.astype(BF16)
    else:
        k = _dot_nt(wkv_ref[:SB_WIDTH, :], h)
        v = _dot_nt(wkv_ref[SB_WIDTH:, :], h)
        k_ref[0] = k
        v_ref[0] = v
        kb_ref[0] = k.astype(BF16)
        vb_ref[0] = v.astype(BF16)
    g_post = proj(3)
    ci = proj(4) * proj(5)
    r1 = pltpu.roll(ci, 1, 0)
    r2 = pltpu.roll(ci, 2, 0)
    row = lax.broadcasted_iota(jnp.int32, (tm, 1), 0)
    if whole_seqs:
        tpos = row % seq_len
        s1 = jnp.where(tpos < 1, e1_ref[...], r1)
        s2 = jnp.where(tpos < 2, e2_ref[...], r2)
        ci_ref[...] = ci
    else:
        first = (pl.program_id(0) % (seq_len // tm)) == 0

        @pl.when(first)
        def _():
            tail_ref[...] = jnp.zeros_like(tail_ref)

        t1 = tail_ref[7:8, :]
        t2 = tail_ref[6:7, :]
        s1 = jnp.where(row == 0, t1, r1)
        s2 = jnp.where(row == 0, t2, jnp.where(row == 1, t1, r2))
        tail_ref[...] = ci[tm - 8:, :]
        cs_ref[0] = ci[tm - (CONV_W - 1):, :]
    conv = wc_ref[0:1, :] * s2 + wc_ref[1:2, :] * s1 + wc_ref[2:3, :] * ci
    gc_ref[...] = (g_post * conv).astype(BF16)


def _proj_even(x, g, w_in, w_kv_t, w_conv, conv_prev, *, batch, seq_len, tm):
    n = x.shape[0]
    whole = tm % seq_len == 0
    grid = (n // tm,)
    row_spec = lambda w: pl.BlockSpec((tm, w), lambda i: (i, 0))
    in_specs = [row_spec(D_MODEL), _const_spec((1, D_MODEL)), _const_spec(w_in.shape), _const_spec(w_conv.shape)]
    args = [x, g, w_in, w_conv]
    if whole:
        kv_shape, kv_spec = (n, SB_WIDTH), row_spec(SB_WIDTH)
    else:
        seq_tiles = seq_len // tm
        kv_shape = (batch, SB_WIDTH, seq_len)
        kv_spec = pl.BlockSpec((1, SB_WIDTH, tm), lambda i: (i // seq_tiles, 0, i % seq_tiles))
    out_shape = [jax.ShapeDtypeStruct((n, SB_WIDTH), BF16), jax.ShapeDtypeStruct(kv_shape, F32),
                 jax.ShapeDtypeStruct(kv_shape, F32), jax.ShapeDtypeStruct(kv_shape, BF16),
                 jax.ShapeDtypeStruct(kv_shape, BF16), jax.ShapeDtypeStruct((n, CONV_DIM), BF16)]
    out_specs = [row_spec(SB_WIDTH)] + [kv_spec] * 4 + [row_spec(CONV_DIM)]
    scratch = []
    if whole:
        if conv_prev is None:
            conv_prev = jnp.zeros((batch, CONV_W - 1, CONV_DIM), F32)
        e1 =jnp.zeros((batch, seq_len, CONV_DIM), F32).at[:, 0].set(conv_prev[:, 1])
        e2 = jnp.zeros((batch, seq_len, CONV_DIM), F32).at[:, 0].set(conv_prev[:, 0]).at[:, 1].set(conv_prev[:, 1])
        args += [e1.reshape(n, CONV_DIM), e2.reshape(n, CONV_DIM)]
        in_specs += [row_spec(CONV_DIM), row_spec(CONV_DIM)]
        out_shape.append(jax.ShapeDtypeStruct((n, CONV_DIM), F32))
        out_specs.append(row_spec(CONV_DIM))
    else:
        assert seq_len % tm == 0 and conv_prev is None
        args.append(w_kv_t)
        in_specs.append(_const_spec(w_kv_t.shape))
        out_shape.append(jax.ShapeDtypeStruct((batch, CONV_W - 1, CONV_DIM), F32))
        out_specs.append(pl.BlockSpec((1, CONV_W - 1, CONV_DIM), lambda i: (i // seq_tiles, 0, 0)))
        scratch.append(pltpu.VMEM((8, CONV_DIM), F32))
    outs = pl.pallas_call(
        functools.partial(_proj_even_body, tm=tm, seq_len=seq_len, whole_seqs=whole),
        grid=grid, in_specs=in_specs, out_specs=out_specs, out_shape=out_shape, scratch_shapes=scratch,
        compiler_params=_cparams(("arbitrary",)), name="proj_even")(*args)
    q, k, v, kb, vb, gc, last = outs
    if whole:
        last = last.reshape(batch, seq_len, CONV_DIM)[:, seq_len - (CONV_W - 1):]
    return q, k, v, kb, vb, gc, last


def _suffix_rhs(tk):
    j = np.arange(2 * tk)[:, None] % tk
    c = np.arange(2 * tk)[None, :]
    return jnp.asarray(np.where(c < tk, j > c, True), dtype=BF16)


def _sb_prompt_body(q_ref, k_ref, v_ref, r_ref, o_ref, carry_ref, acc_ref, *, tq, tk):
    qi = pl.program_id(1)
    ratio = tq // tk
    even = lax.broadcasted_iota(jnp.int32, (LANES, tk), 0) < SB_HEAD_DIM
    rhs = r_ref[...]
    carry_ref[...] = jnp.zeros_like(carry_ref)
    acc_ref[...] = jnp.zeros_like(acc_ref)
    qs = [q_ref[:, p * LANES:(p + 1) * LANES] for p in range(HEAD_PAIRS)]

    def block_diag(blk):
        zero = jnp.zeros_like(blk)
        return jnp.concatenate([jnp.where(even, blk, zero), jnp.where(even, zero, blk)], axis=1)

    def tile(kj, masked, r0=0):
        start = pl.multiple_of(kj * tk, tk)
        if masked:
            q_pos = qi * tq + r0 + lax.broadcasted_iota(jnp.int32, (tq - r0, tk), 0)
            k_pos = kj * tk + lax.broadcasted_iota(jnp.int32, (tq - r0, tk), 1)
            visible = k_pos < q_pos
        zs = [_dot(qs[p][r0:], block_diag(k_ref[0, p * LANES:(p + 1) * LANES, pl.ds(start, tk)]))
              for p in range(HEAD_PAIRS)]
        log_betas, splits = [], []
        for hd in range(SB_HEADS):
            z = zs[hd // 2][:, (hd % 2) * tk:(hd % 2 + 1) * tk]
            log_beta = _log_sigmoid(z)
            log_1m = log_beta - z
            if masked:
                log_1m = jnp.where(visible, log_1m, 0.0)
            hi, lo = _split_hi_lo(log_1m)
            log_betas.append(log_beta)
            splits.append(jnp.concatenate([hi, lo], axis=1))
        sums = [_dot(sp, rhs) for sp in splits]
        ws = []
        for hd in range(SB_HEADS):
            carry = carry_ref[hd, r0:, :]
            w = jnp.exp(log_betas[hd] + sums[hd][:, :tk] + carry)
            if masked:
                w = jnp.where(visible, w, 0.0)
            carry_ref[hd, r0:, :] = carry + sums[hd][:, tk:]
            ws.append(w.astype(BF16))
        for p in range(HEAD_PAIRS):
            vbd = block_diag(v_ref[0, p * LANES:(p + 1) * LANES, pl.ds(start, tk)])
            acc_ref[p, r0:, :] += _dot_nt(jnp.concatenate([ws[2 * p], ws[2 * p + 1]], axis=1), vbd)

    for t in reversed(range(ratio)):
        tile(qi * ratio + t, True, r0=t * tk)

    def body(it, c):
        tile(qi * ratio - 1 - 2 * it, False)
        tile(qi * ratio - 2 - 2 * it, False)
        return c

    lax.fori_loop(0, qi * (ratio // 2), body, 0)
    for p in range(HEAD_PAIRS):
        o_ref[:, p * LANES:(p + 1) * LANES] = acc_ref[p].astype(BF16)


def _sb_prompt(q, kb, vb, *, batch, seq_len, tq, tk):
    assert (tq // tk) % 2 == 0 and tq % tk == 0 and seq_len % tq == 0
    n = q.shape[0]
    nq = seq_len // tq
    rhs = _suffix_rhs(tk)
    q_spec = pl.BlockSpec((tq, SB_WIDTH), lambda b, i: (b * nq + i, 0))
    kv_spec = pl.BlockSpec((1, SB_WIDTH, seq_len), lambda b, i: (b, 0, 0))
    return pl.pallas_call(
        functools.partial(_sb_prompt_body, tq=tq, tk=tk),
        grid=(batch, nq),
        in_specs=[q_spec, kv_spec, kv_spec, _const_spec(rhs.shape)],
        out_specs=q_spec,
        out_shape=jax.ShapeDtypeStruct((n, SB_WIDTH), BF16),
        scratch_shapes=[pltpu.VMEM((SB_HEADS, tq, LANES), F32), pltpu.VMEM((HEAD_PAIRS, tq, LANES), F32)],
        compiler_params=_cparams(("arbitrary", "arbitrary")), name="sb_attn_prompt")(q, kb, vb, rhs)


def _sb_sample_body(q_ref, kn_ref, vn_ref, kc_ref, vc_ref, l_ref, o_ref, qh_ref, carry_ref, acc_ref,
                    *, dec, kblk, tk):
    j = pl.program_id(1)
    rows = SB_HEADS * dec
    rhs = l_ref[...]

    def tile(kt, vt, visible):
        z = _dot(qh_ref[...], kt)
        log_beta = _log_sigmoid(z)
        log_1m = log_beta - z
        if visible is not None:
            log_1m = jnp.where(visible, log_1m, 0.0)
        hi, lo = _split_hi_lo(log_1m)
        sums = _dot(jnp.concatenate([hi, lo], axis=1), rhs)
        w = jnp.exp(log_beta + sums[:, :tk] + carry_ref[...])
        if visible is not None:
            w = jnp.where(visible, w, 0.0)
        acc_ref[...] += _dot_nt(w.astype(BF16), vt)
        carry_ref[...] += sums[:, tk:]

    def new_keys_t(x):
        x = jnp.concatenate([x.astype(F32), jnp.zeros((tk - dec, SB_WIDTH), F32)], axis=0)
        return x.T.astype(BF16)

    @pl.when(j == 0)
    def _():
        q = q_ref[...]
        qt = jnp.concatenate([q] * SB_HEADS, axis=0)
        rr = lax.broadcasted_iota(jnp.int32, (rows, SB_WIDTH), 0)
        cc = lax.broadcasted_iota(jnp.int32, (rows, SB_WIDTH), 1)
        qh_ref[...] = jnp.where(rr // dec == cc // SB_HEAD_DIM, qt, jnp.zeros_like(qt))
        carry_ref[...] = jnp.zeros_like(carry_ref)
        acc_ref[...] = jnp.zeros_like(acc_ref)
        query_t = lax.broadcasted_iota(jnp.int32, (rows, tk), 0) % dec
        key_i = lax.broadcasted_iota(jnp.int32, (rows, tk), 1)
        tile(new_keys_t(kn_ref[...]), new_keys_t(vn_ref[...]),
             key_i < query_t)

    for s in reversed(range(kblk // tk)):
        tile(kc_ref[0, :, s * tk:(s + 1) * tk].astype(BF16), vc_ref[0, :, s * tk:(s + 1) * tk].astype(BF16), None)

    @pl.when(j == pl.num_programs(1) - 1)
    def _():
        cc = lax.broadcasted_iota(jnp.int32, (dec, SB_WIDTH), 1)
        out = jnp.zeros((dec, SB_WIDTH), F32)
        for h in range(SB_HEADS):
            out = out + jnp.where(cc // SB_HEAD_DIM == h, acc_ref[h * dec:(h + 1) * dec, :], 0.0)
        o_ref[...] = out.astype(BF16)


def _sb_sample(q, kb, vb, cache_kt, cache_vt, *, batch, dec, kblk=1024, tk=256):
    n = q.shape[0]
    past = cache_kt.shape[2]
    nkb = past // kblk
    rows = SB_HEADS * dec
    rhs = _suffix_rhs(tk)
    new_spec = pl.BlockSpec((dec, SB_WIDTH), lambda b, j: (b, 0))
    cache_spec = pl.BlockSpec((1, SB_WIDTH, kblk), lambda b, j: (b, 0, nkb - 1 - j))
    return pl.pallas_call(
        functools.partial(_sb_sample_body, dec=dec, kblk=kblk, tk=tk),
        grid=(batch, nkb),
        in_specs=[new_spec, new_spec, new_spec, cache_spec, cache_spec, _const_spec(rhs.shape)],
        out_specs=new_spec,
        out_shape=jax.ShapeDtypeStruct((n, SB_WIDTH), BF16),
        scratch_shapes=[pltpu.VMEM((rows, SB_WIDTH), BF16), pltpu.VMEM((rows, tk), F32),
                        pltpu.VMEM((rows, SB_WIDTH), F32)],
        compiler_params=_cparams(("arbitrary", "arbitrary")), name="sb_attn_sample")(
            q, kb, vb, cache_kt, cache_vt, rhs)


def _proj_odd_body(*refs, tm, gate_len, emit_vn, kpe_transposed):
    (x_ref, g_ref, w_ref, lng_ref, lnb_ref, ws_ref, bs_ref, qg_ref, kvg_ref, wqn_ref, wqp_ref,
     cos_ref, sin_ref) = refs[:13]
    outs = refs[13:]
    if emit_vn:
        sgu_ref, vn_ref, qn_ref, qp_ref, ckv_ref, ckvb_ref, kpe_ref, kpeb_ref = outs
    else:
        sgu_ref, qn_ref, qp_ref, ckv_ref, ckvb_ref, kpe_ref, kpeb_ref = outs
    h = _rms(x_ref[...], g_ref[...]).astype(BF16)
    o_v, o_q, o_kv, o_pe = SGU_DIM, 2 * SGU_DIM, 2 * SGU_DIM + Q_LORA, 2 * SGU_DIM + Q_LORA + KV_LORA
    u = _dot(h, w_ref[:, :o_v])
    v = _dot(h, w_ref[:, o_v:o_q])
    mu = jnp.mean(v, axis=-1, keepdims=True)
    vc = v - mu
    var = jnp.mean(vc * vc, axis=-1, keepdims=True)
    vn = vc * lax.rsqrt(var + EPS) * lng_ref[...] + lnb_ref[...]
    if emit_vn:
        vn_ref[...] = vn
    vnb = vn.astype(BF16)
    rr = lax.broadcasted_iota(jnp.int32, (SGU_CHUNK, SGU_CHUNK), 0)
    cc = lax.broadcasted_iota(jnp.int32, (SGU_CHUNK, SGU_CHUNK), 1)
    causal = (rr // gate_len == cc // gate_len) & (cc <= rr)
    for g in range(SGU_GROUPS):
        gs = slice(g * SGU_GROUP_DIM, (g + 1) * SGU_GROUP_DIM)
        wg = jnp.where(causal, ws_ref[g], 0.0).astype(BF16)
        for c in range(tm // SGU_CHUNK):
            rs = slice(c * SGU_CHUNK, (c + 1) * SGU_CHUNK)
            s = _dot(wg, vnb[rs, gs]) + bs_ref[:, gs]
            sgu_ref[rs, gs] = (u[rs, gs] * s).astype(BF16)
    cq = _rms(_dot(h, w_ref[:, o_q:o_kv]), qg_ref[...]).astype(BF16)
    qn_ref[...] = _dot(cq, wqn_ref[...]).astype(BF16)
    cos = cos_ref[...]
    sin = sin_ref[...]
    for blk in range(MLA_HEADS * ROPE_DIM // LANES):
        bs = slice(blk * LANES, (blk + 1) * LANES)
        qp_ref[:, bs] = _rope_block(_dot(cq, wqp_ref[:, bs]), cos, sin).astype(BF16)
    ckv = _rms(_dot(h, w_ref[:, o_kv:o_pe]), kvg_ref[...])
    ckv_ref[...] = ckv
    ckvb_ref[...] = ckv.astype(BF16)
    kpe = _rope_block(_dot(h, w_ref[:, o_pe:]), cos, sin)
    if kpe_transposed:
        kpe_ref[0] = kpe.T[:ROPE_DIM, :]
    else:
        kpe_ref[...] = kpe[:, :ROPE_DIM]
    kpeb_ref[...] = kpe.astype(BF16)


def _proj_odd(x, g, w_in, ln_g, ln_b, w_s, b_s, qg, kvg, wqn, wqp, cos, sin, *, batch, seq_len, tm, gate_len,
              emit_vn):
    n = x.shape[0]
    pos_tiles = cos.shape[0] // tm
    kpe_transposed = tm % seq_len != 0
    if kpe_transposed:
        seq_tiles = seq_len // tm
        kpe_shape = (batch, ROPE_DIM, seq_len)
        kpe_spec = pl.BlockSpec((1, ROPE_DIM, tm), lambda i: (i // seq_tiles, 0, i % seq_tiles))
    else:
        kpe_shape, kpe_spec = (n, ROPE_DIM), pl.BlockSpec((tm, ROPE_DIM), lambda i: (i, 0))
    row_spec = lambda w: pl.BlockSpec((tm, w), lambda i: (i, 0))
    tab_spec = pl.BlockSpec((tm, LANES), lambda i: (i % pos_tiles, 0))
    consts = [g, w_in, ln_g, ln_b, w_s, b_s, qg, kvg, wqn, wqp]
    out_shape = [jax.ShapeDtypeStruct((n, SGU_DIM), BF16)]
    out_specs = [row_spec(SGU_DIM)]
    if emit_vn:
        out_shape.append(jax.ShapeDtypeStruct((n, SGU_DIM), F32))
        out_specs.append(row_spec(SGU_DIM))
    out_shape += [jax.ShapeDtypeStruct((n, MLA_HEADS * NOPE_DIM), BF16),
                  jax.ShapeDtypeStruct((n, MLA_HEADS * ROPE_DIM), BF16),
                  jax.ShapeDtypeStruct((n, KV_LORA), F32), jax.ShapeDtypeStruct((n, KV_LORA), BF16),
                  jax.ShapeDtypeStruct(kpe_shape, F32), jax.ShapeDtypeStruct((n, LANES), BF16)]
    out_specs += [row_spec(MLA_HEADS * NOPE_DIM), row_spec(MLA_HEADS * ROPE_DIM),
                  row_spec(KV_LORA), row_spec(KV_LORA), kpe_spec, row_spec(LANES)]
    return pl.pallas_call(
        functools.partial(_proj_odd_body, tm=tm, gate_len=gate_len, emit_vn=emit_vn, kpe_transposed=kpe_transposed),
        grid=(n // tm,),
        in_specs=[row_spec(D_MODEL)] + [_const_spec(c.shape) for c in consts] + [tab_spec, tab_spec],
        out_specs=out_specs, out_shape=out_shape,
        compiler_params=_cparams(("arbitrary",)), name="proj_odd")(x, *consts, cos, sin)


def _mla_queries(qn, qpe, wuk_ref, qcat_ref, tq):
    lane = lax.broadcasted_iota(jnp.int32, (tq, LANES), 1)
    low = lane < NOPE_DIM
    qpe = qpe.astype(F32)
    qcat_ref[:, KV_LORA:] = jnp.zeros((MLA_HEADS * tq, MLA_QK - KV_LORA), BF16)
    for hd in range(MLA_HEADS):
        p = hd // 2
        rs = slice(hd * tq, (hd + 1) * tq)
        pair = qn[:, p * LANES:(p + 1) * LANES]
        qm = jnp.where(low if hd % 2 == 0 else jnp.logical_not(low), pair, jnp.zeros_like(pair))
        qcat_ref[rs, :KV_LORA] = _dot(qm, wuk_ref[p]).astype(BF16)
        qcat_ref[rs, KV_LORA:KV_LORA + ROPE_DIM] = qpe[:, hd * ROPE_DIM:(hd + 1) * ROPE_DIM].astype(BF16)


def _lane_tile(x, width):
    return jnp.concatenate([x] * (width // LANES), axis=1)


def _mla_scores(qcat_ref, kcat, sc):
    sc[0][...] = _dot_nt(qcat_ref[...], kcat)


def _mla_softmax(sc, visible_fn):
    s_ref, p_ref, m_ref, l_ref, acc_ref = sc
    rows, tk = s_ref.shape
    for c in range(rows // MLA_ROW_CHUNK):
        rs = slice(c * MLA_ROW_CHUNK, (c + 1) * MLA_ROW_CHUNK)
        s = s_ref[rs, :] * (MLA_SCALE * LOG2_E)
        if visible_fn is not None:
            s = jnp.where(visible_fn(c), s, NEG_INF)
        m_prev = m_ref[rs, :]
        m_new = jnp.maximum(m_prev, jnp.max(s, axis=-1, keepdims=True))
        alpha = jnp.exp2(m_prev - m_new)
        p = jnp.exp2(s - _lane_tile(m_new, tk))
        l_ref[rs, :] = alpha * l_ref[rs, :] + jnp.sum(p, axis=-1, keepdims=True)
        m_ref[rs, :] = m_new
        p_ref[rs, :] = p.astype(BF16)
        acc_ref[rs, :] = acc_ref[rs, :] * _lane_tile(alpha, KV_LORA)


def _mla_pv(ck, sc):
    sc[4][...] += _dot(sc[1][...], ck)


def _mla_tile(qcat_ref, kcat, sc, visible_fn):
    _mla_scores(qcat_ref, kcat, sc)
    _mla_softmax(sc, visible_fn)
    _mla_pv(kcat[:, :KV_LORA], sc)


def _mla_finish(sc, wuv_ref, o_ref, tq):
    l_ref, acc_ref = sc[3], sc[4]
    o_lat = (acc_ref[...] / _lane_tile(l_ref[...], KV_LORA)).astype(BF16)
    for p in range(MLA_HEADS // 2):
        h0, h1 = 2 * p, 2 * p + 1
        o_ref[:, p * LANES:(p + 1) * LANES] = (
            _dot(o_lat[h0 * tq:(h0 + 1) * tq], wuv_ref[h0]) + _dot(o_lat[h1 * tq:(h1 + 1) * tq], wuv_ref[h1])
        ).astype(BF16)


def _mla_init(sc):
    m_ref, l_ref, acc_ref = sc[2], sc[3], sc[4]
    m_ref[...] = jnp.full_like(m_ref, -jnp.inf)
    l_ref[...] = jnp.zeros_like(l_ref)
    acc_ref[...] = jnp.zeros_like(acc_ref)


def _mla_prompt_body(qn_ref, qp_ref, ckv_ref, kpe_ref, wuk_ref, wuv_ref, o_ref, *scratch, tq, tk, streams):
    qi = pl.program_id(1)
    rows = MLA_HEADS * tq
    per = len(scratch) // streams
    qcats = [scratch[st * per] for st in range(streams)]
    scs = [scratch[st * per + 1:(st + 1) * per] for st in range(streams)]
    for st in range(streams):
        _mla_queries(qn_ref[st], qp_ref[st], wuk_ref, qcats[st], tq)
        _mla_init(scs[st])

    n_full = (qi * tq) // tk
    col = lax.broadcasted_iota(jnp.int32, (MLA_ROW_CHUNK, tk), 1)

    def visible(c):
        q_chunk_end = qi * tq + ((c * MLA_ROW_CHUNK) % tq) // CHUNK * CHUNK + CHUNK
        return col < q_chunk_end - n_full * tk

    def tile(kj, visible_fn):
        start = pl.multiple_of(kj * tk, tk)
        kcats = [jnp.concatenate([ckv_ref[st, pl.ds(start, tk), :], kpe_ref[st, pl.ds(start, tk), :],
                                  jnp.zeros((tk, MLA_QK - KV_LORA - LANES), BF16)], axis=1) for st in range(streams)]
        for st in range(streams):
            _mla_scores(qcats[st], kcats[st], scs[st])
        for st in range(streams):
            _mla_softmax(scs[st], visible_fn)
        for st in range(streams):
            _mla_pv(kcats[st][:, :KV_LORA], scs[st])

    def body(kj, c):
        tile(kj, None)
        return c

    lax.fori_loop(0, n_full, body, 0)
    tile(n_full, visible)
    for st in range(streams):
        _mla_finish(scs[st], wuv_ref, o_ref.at[st], tq)


def _mla_scratch(rows, tk):
    return [pltpu.VMEM((rows, MLA_QK), BF16), pltpu.VMEM((rows, tk), F32), pltpu.VMEM((rows, tk), BF16),
            pltpu.VMEM((rows, LANES), F32), pltpu.VMEM((rows, LANES), F32), pltpu.VMEM((rows, KV_LORA), F32)]


def _mla_prompt(qn, qp, ckvb, kpeb, wuk, wuv, *, batch, seq_len, tq, tk):
    assert tq % CHUNK == 0 and CHUNK % MLA_ROW_CHUNK == 0 and seq_len % tq == 0 and seq_len % tk == 0
    n = qn.shape[0]
    nq = seq_len // tq
    width = MLA_HEADS * V_DIM
    streams = MLA_STREAMS if batch % MLA_STREAMS == 0 else 1
    seq3 = lambda a: a.reshape(batch, seq_len, a.shape[-1])
    out = pl.pallas_call(
        functools.partial(_mla_prompt_body, tq=tq, tk=tk, streams=streams),
        grid=(batch // streams, nq),
        in_specs=[pl.BlockSpec((streams, tq, MLA_HEADS * NOPE_DIM), lambda g, i: (g, i, 0)),
                  pl.BlockSpec((streams, tq, MLA_HEADS * ROPE_DIM), lambda g, i: (g, i, 0)),
                  pl.BlockSpec((streams, seq_len, KV_LORA), lambda g, i: (g, 0, 0)),
                  pl.BlockSpec((streams, seq_len, LANES), lambda g, i: (g, 0, 0)),
                  _const_spec(wuk.shape), _const_spec(wuv.shape)],
        out_specs=pl.BlockSpec((streams, tq, width), lambda g, i: (g, i, 0)),
        out_shape=jax.ShapeDtypeStruct((batch, seq_len, width), BF16),
        scratch_shapes=_mla_scratch(MLA_HEADS * tq, tk) * streams,
        compiler_params=_cparams(("arbitrary", "arbitrary")), name="mla_attn_prompt")(
            seq3(qn), seq3(qp), seq3(ckvb), seq3(kpeb), wuk, wuv)
    return out.reshape(n, width)


def _mla_sample_body(qn_ref, qp_ref, cn_ref, pn_ref, cc_ref, pc_ref, wuk_ref, wuv_ref, o_ref, kcat_ref, qcat_ref,
                     *sc, dec, past, kblk, tk):
    j = pl.program_id(1)
    rows = MLA_HEADS * dec

    @pl.when(j == 0)
    def _():
        _mla_queries(qn_ref[...], qp_ref[...], wuk_ref, qcat_ref, dec)
        _mla_init(sc)
        kcat_ref[:, KV_LORA:] = jnp.zeros((tk, MLA_QK - KV_LORA), BF16)
        new = jnp.concatenate([cn_ref[...], pn_ref[...], jnp.zeros((dec, MLA_QK - KV_LORA - LANES), BF16)], axis=1)
        kcat = jnp.concatenate([new, jnp.zeros((tk - dec, MLA_QK), BF16)], axis=0)

        def visible(c):
            col = lax.broadcasted_iota(jnp.int32, (MLA_ROW_CHUNK, tk), 1)
            row = c * MLA_ROW_CHUNK + lax.broadcasted_iota(jnp.int32, (MLA_ROW_CHUNK, tk), 0)
            return (col < dec) & ((past + col) // CHUNK <= (past + row % dec) // CHUNK)

        _mla_tile(qcat_ref, kcat, sc, visible)

    def body(it, c):
        start = pl.multiple_of(it * tk, tk)
        kcat_ref[:, :KV_LORA] = cc_ref[0, pl.ds(start, tk), :].astype(BF16)
        kp_t = jnp.concatenate([pc_ref[0, :, pl.ds(start, tk)], jnp.zeros((LANES - ROPE_DIM, tk), F32)], axis=0)
        kcat_ref[:, KV_LORA:KV_LORA + LANES] = kp_t.T.astype(BF16)
        _mla_tile(qcat_ref, kcat_ref[...], sc, None)
        return c

    lax.fori_loop(0, kblk // tk, body, 0)

    @pl.when(j == pl.num_programs(1) - 1)
    def _():
        _mla_finish(sc, wuv_ref, o_ref, dec)


def _mla_sample(qn, qp, ckvb, kpeb, cache_ckv, cache_kpe_t, wuk, wuv, *, batch, dec, kblk=1024, tk=512):
    n = qn.shape[0]
    past = cache_ckv.shape[1]
    width = MLA_HEADS * V_DIM
    return pl.pallas_call(
        functools.partial(_mla_sample_body, dec=dec, past=past, kblk=kblk, tk=tk),
        grid=(batch, past // kblk),
        in_specs=[pl.BlockSpec((dec, MLA_HEADS * NOPE_DIM), lambda b, j: (b, 0)),
                  pl.BlockSpec((dec, MLA_HEADS * ROPE_DIM), lambda b, j: (b, 0)),
                  pl.BlockSpec((dec, KV_LORA), lambda b, j: (b, 0)),
                  pl.BlockSpec((dec, LANES), lambda b, j: (b, 0)),
                  pl.BlockSpec((1, kblk, KV_LORA), lambda b, j: (b, j, 0)),
                  pl.BlockSpec((1, ROPE_DIM, kblk), lambda b, j: (b, 0, j)),
                  _const_spec(wuk.shape), _const_spec(wuv.shape)],
        out_specs=pl.BlockSpec((dec, width), lambda b, j: (b, 0)),
        out_shape=jax.ShapeDtypeStruct((n, width), BF16),
        scratch_shapes=[pltpu.VMEM((tk, MLA_QK), BF16)] + _mla_scratch(MLA_HEADS * dec, tk),
        compiler_params=_cparams(("arbitrary", "arbitrary")), name="mla_attn_sample")(
            qn, qp, ckvb, kpeb, cache_ckv, cache_kpe_t, wuk, wuv)


def _out_ffn_body(a_ref, b_ref, x_ref, woa_ref, wob_ref, gpost_ref, gpre_ref, wup_ref, wdn_ref, gfpost_ref, o_ref):
    tm = x_ref.shape[0]
    groups = [slice(r * tm // FFN_ROW_GROUPS, (r + 1) * tm // FFN_ROW_GROUPS) for r in range(FFN_ROW_GROUPS)]
    mixed = [_dot(a_ref[rs, :], woa_ref[...]) + _dot(b_ref[rs, :], wob_ref[...]) for rs in groups]
    x1 = [x_ref[rs, :] + _rms(m, gpost_ref[...]) for rs, m in zip(groups, mixed)]
    h = [_rms(x, gpre_ref[...]).astype(BF16) for x in x1]
    down = [jnp.zeros_like(x) for x in x1]
    for c in range(D_FF // FF_CHUNK):
        for r in range(FFN_ROW_GROUPS):
            up = _dot(h[r], wup_ref[:, c * FF_CHUNK:(c + 1) * FF_CHUNK])
            act = jnp.square(jnp.maximum(up, 0.0)).astype(BF16)
            down[r] = down[r] + _dot(act, wdn_ref[c * FF_CHUNK:(c + 1) * FF_CHUNK, :])
    for r, rs in enumerate(groups):
        o_ref[rs, :] = x1[r] + _rms(down[r], gfpost_ref[...])


def _out_ffn(a, b, x, woa, wob, g_post, g_pre, w_up, w_down, g_fpost, *, tm):
    n = x.shape[0]
    row_spec = lambda w: pl.BlockSpec((tm, w), lambda i: (i, 0))
    consts = [woa, wob, g_post, g_pre, w_up, w_down, g_fpost]
    return pl.pallas_call(
        _out_ffn_body,
        grid=(n // tm,),
        in_specs=[row_spec(a.shape[1]), row_spec(b.shape[1]), row_spec(D_MODEL)] + [_const_spec(c.shape) for c in consts],
        out_specs=row_spec(D_MODEL),
        out_shape=jax.ShapeDtypeStruct((n, D_MODEL), F32),
        compiler_params=_cparams(("arbitrary",)), name="out_ffn")(a, b, x, *consts)


def _rope_tables(pos, reps):
    half = ROPE_DIM // 2
    inv = ROPE_THETA ** (-jnp.arange(half, dtype=F32) / half)
    ang = pos.astype(F32)[:, None] * inv[None, :]
    cos = jnp.tile(jnp.concatenate([jnp.cos(ang), jnp.cos(ang)], axis=1), (reps, LANES // ROPE_DIM))
    sin = jnp.tile(jnp.concatenate([-jnp.sin(ang), jnp.sin(ang)], axis=1), (reps, LANES // ROPE_DIM))
    return cos, sin


def _prep_even(p, j):
    w_in = p["even_w_in"][j]
    return dict(w_in=w_in.astype(BF16), w_kv_t=w_in[:, SB_WIDTH:3 * SB_WIDTH].T.astype(BF16), w_conv=p["even_w_conv"][j],
                woa=p["even_w_out"][j, :SB_WIDTH].astype(BF16), wob=p["even_w_out"][j, SB_WIDTH:].astype(BF16))


def _prep_odd(p, j, gate_len):
    w_in = p["odd_w_in"][j]
    w_in = jnp.pad(w_in, ((0, 0), (0, ODD_IN_PAD - w_in.shape[1]))).astype(BF16)
    reps = SGU_CHUNK // gate_len
    w_s = jnp.tile(p["sgu_w_s"][j, :, :gate_len, :gate_len], (1, reps, reps))
    b_s = jnp.tile(p["sgu_b_s"][j, :, :gate_len], (1, reps))
    b_s = jnp.repeat(b_s.T, SGU_GROUP_DIM, axis=1)
    w_uq = p["mla_w_uq"][j].reshape(Q_LORA, MLA_HEADS, NOPE_DIM + ROPE_DIM)
    wqn = w_uq[:, :, :NOPE_DIM].reshape(Q_LORA, MLA_HEADS * NOPE_DIM).astype(BF16)
    wqp = w_uq[:, :, NOPE_DIM:].reshape(Q_LORA, MLA_HEADS * ROPE_DIM).astype(BF16)
    wuk = p["mla_w_uk"][j].reshape(MLA_HEADS // 2, 2 * NOPE_DIM, KV_LORA).astype(BF16)
    w_uv = p["mla_w_uv"][j]
    wuv = jnp.stack([jnp.pad(w_uv[h], ((0, 0), ((h % 2) * V_DIM, (1 - h % 2) * V_DIM))) for h in range(MLA_HEADS)])
    return dict(w_in=w_in, ln_g=p["sgu_ln_g"][j][None], ln_b=p["sgu_ln_b"][j][None], w_s=w_s, b_s=b_s,
                qg=p["mla_q_norm_g"][j][None], kvg=p["mla_kv_norm_g"][j][None], wqn=wqn, wqp=wqp, wuk=wuk,
                wuv=wuv.astype(BF16),
                woa=p["odd_w_out"][j, :SGU_DIM].astype(BF16), wob=p["odd_w_out"][j, SGU_DIM:].astype(BF16))


def _run_trunk(x, pos, past, p, *, batch, seq_len):
    depth = p["mix_pre_g"].shape[0]
    n = batch * seq_len
    x = x.reshape(n, D_MODEL)
    is_sample = past is not None
    tm = min(ROW_TILE, n)
    assert n % tm == 0 and (tm % seq_len == 0 or seq_len % tm == 0) and tm % SGU_CHUNK == 0
    gate_len = min(seq_len, SGU_CHUNK)
    tm_proj = PROJ_ROW_TILE if (seq_len % PROJ_ROW_TILE == 0 and seq_len > PROJ_ROW_TILE) else tm
    cos, sin = _rope_tables(pos, max(1, tm_proj // seq_len))
    st = {k: [] for k in ("sb_k", "sb_v", "conv", "ckv", "kpe", "sgu_v")}
    for layer in range(depth):
        j = layer // 2
        g_pre = p["mix_pre_g"][layer][None]
        if layer % 2 == 0:
            w = _prep_even(p, j)
            conv_prev = past["conv"][j] if is_sample else None
            q, k, v, kb, vb, b_mix, conv_state = _proj_even(x, g_pre, w["w_in"], w["w_kv_t"], w["w_conv"], conv_prev,
                                                            batch=batch, seq_len=seq_len, tm=tm_proj)
            if is_sample:
                assert tm % seq_len == 0

                def cache_t(c):
                    return jnp.transpose(c, (0, 2, 3, 1)).reshape(batch, SB_WIDTH, -1)

                a_mix = _sb_sample(q, kb, vb, cache_t(past["sb_k"][j]), cache_t(past["sb_v"][j]),
                                   batch=batch, dec=seq_len)
                k, v = (t.reshape(batch, seq_len, SB_HEADS, SB_HEAD_DIM) for t in (k, v))
            else:
                assert seq_len % tm == 0
                a_mix = _sb_prompt(q, kb, vb, batch=batch, seq_len=seq_len, tq=SB_TQ, tk=SB_TK)
                k, v = (jnp.transpose(t.reshape(batch, SB_HEADS, SB_HEAD_DIM, seq_len), (0, 3, 1, 2)) for t in (k, v))
            st["sb_k"].append(k)
            st["sb_v"].append(v)
            st["conv"].append(conv_state)
        else:
            w = _prep_odd(p, j, gate_len)
            outs = _proj_odd(x, g_pre, w["w_in"], w["ln_g"], w["ln_b"], w["w_s"], w["b_s"], w["qg"], w["kvg"],
                             w["wqn"], w["wqp"], cos, sin, batch=batch, seq_len=seq_len, tm=tm_proj, gate_len=gate_len,
                             emit_vn=is_sample)
            if is_sample:
                a_mix, vn, qn, qp, ckv, ckvb, kpe, kpeb = outs
                st["sgu_v"].append(vn.reshape(batch, seq_len, SGU_DIM))
                b_mix = _mla_sample(qn, qp, ckvb, kpeb, past["ckv"][j], jnp.transpose(past["kpe"][j], (0, 2, 1)),
                                    w["wuk"], w["wuv"],
                                    batch=batch, dec=seq_len)
            else:
                a_mix, qn, qp, ckv, ckvb, kpe, kpeb = outs
                b_mix = _mla_prompt(qn, qp, ckvb, kpeb, w["wuk"], w["wuv"], batch=batch, seq_len=seq_len,
                                    tq=MLA_TQ, tk=MLA_TK)
            st["ckv"].append(ckv.reshape(batch, seq_len, KV_LORA))
            st["kpe"].append(jnp.transpose(kpe, (0, 2, 1)) if kpe.ndim == 3 else kpe.reshape(batch, seq_len, ROPE_DIM))
        x = _out_ffn(a_mix, b_mix, x, w["woa"], w["wob"], p["mix_post_g"][layer][None], p["ffn_pre_g"][layer][None],
                     p["ffn_w_up"][layer].astype(BF16), p["ffn_w_down"][layer].astype(BF16),
                     p["ffn_post_g"][layer][None], tm=tm)
    states = {k: jnp.stack(v) for k, v in st.items() if v}
    return x.reshape(batch, seq_len, D_MODEL), states


def kernel(x_prompt, x_sample, cache_sb_k, cache_sb_v, state_conv, cache_mla_ckv, cache_mla_kpe,
           mix_pre_g, mix_post_g, ffn_pre_g, ffn_post_g, even_w_in, even_w_conv, even_w_out,
           odd_w_in, sgu_ln_g, sgu_ln_b, sgu_w_s, sgu_b_s, mla_q_norm_g, mla_kv_norm_g,
           mla_w_uq, mla_w_uk, mla_w_uv, odd_w_out, ffn_w_up, ffn_w_down):
    params = {
        "mix_pre_g": mix_pre_g, "mix_post_g": mix_post_g, "ffn_pre_g": ffn_pre_g, "ffn_post_g": ffn_post_g,
        "even_w_in": even_w_in, "even_w_conv": even_w_conv, "even_w_out": even_w_out,
        "odd_w_in": odd_w_in, "sgu_ln_g": sgu_ln_g, "sgu_ln_b": sgu_ln_b, "sgu_w_s": sgu_w_s,
        "sgu_b_s": sgu_b_s, "mla_q_norm_g": mla_q_norm_g, "mla_kv_norm_g": mla_kv_norm_g,
        "mla_w_uq": mla_w_uq, "mla_w_uk": mla_w_uk, "mla_w_uv": mla_w_uv, "odd_w_out": odd_w_out,
        "ffn_w_up": ffn_w_up, "ffn_w_down": ffn_w_down,
    }
    batch, seq_len, _ = x_prompt.shape
    pos_p = jnp.arange(seq_len, dtype=jnp.int32)
    y_prompt, st_p = _run_trunk(x_prompt, pos_p, None, params, batch=batch, seq_len=seq_len)
    dec_batch, dec_seq, _ = x_sample.shape
    past_len = cache_sb_k.shape[2]
    pos_s = past_len + jnp.arange(dec_seq, dtype=jnp.int32)
    past = {"sb_k": cache_sb_k, "sb_v": cache_sb_v, "conv": state_conv, "ckv": cache_mla_ckv, "kpe": cache_mla_kpe}
    y_sample, st_s = _run_trunk(x_sample, pos_s, past, params, batch=dec_batch, seq_len=dec_seq)
    return (y_prompt, y_sample,
            st_p["sb_k"], st_p["sb_v"], st_p["conv"], st_p["ckv"], st_p["kpe"],
            st_s["sb_k"], st_s["sb_v"], st_s["conv"], st_s["ckv"], st_s["kpe"], st_s["sgu_v"])
```

```python
import functools
import math

import numpy as np
import jax
import jax.numpy as jnp
from jax import lax
from jax.experimental import pallas as pl
from jax.experimental.pallas import tpu as pltpu

F32 = jnp.float32
BF16 = jnp.bfloat16

EPS = 1e-6
D_MODEL = 1024
CHUNK = 64
SB_HEADS = 8
SB_HEAD_DIM = 64
SB_WIDTH = SB_HEADS * SB_HEAD_DIM
SB_SCALE = 1.0 / math.sqrt(SB_HEAD_DIM)
CONV_DIM = D_MODEL // 2
CONV_W = 3
SGU_CHUNK = 128
SGU_GROUPS = 4
SGU_DIM = D_MODEL // 2
SGU_GROUP_DIM = SGU_DIM // SGU_GROUPS
MLA_HEADS = 8
Q_LORA = 384
KV_LORA = 256
NOPE_DIM = 64
ROPE_DIM = 32
V_DIM = 64
ROPE_THETA = 10000.0
MLA_SCALE = 1.0 / math.sqrt(NOPE_DIM + ROPE_DIM)
D_FF = 4 * D_MODEL
FF_CHUNK = 1024
FFN_ROW_GROUPS = 2

LANES = 128
HEAD_PAIRS = SB_HEADS // 2
MLA_QK = 2 * KV_LORA
ODD_IN_PAD =2 * SGU_DIM + Q_LORA + KV_LORA + LANES
VMEM_LIMIT = 56 * 1024 * 1024
NEG_INF = -1e30
LOG2_E = math.log2(math.e)

ROW_TILE = 512
PROJ_ROW_TILE = 1024
SB_TQ, SB_TK = 512, 128
MLA_TQ, MLA_TK = 256, 256
MLA_ROW_CHUNK = 64
MLA_STREAMS = 1


def _cparams(sem):
    return pltpu.CompilerParams(dimension_semantics=sem, vmem_limit_bytes=VMEM_LIMIT)


def _const_spec(shape):
    nd = len(shape)
    return pl.BlockSpec(shape, lambda *_: (0,) * nd, pipeline_mode=pl.Buffered(1))


def _rms(x, g):
    return x * lax.rsqrt(jnp.mean(x * x, axis=-1, keepdims=True) + EPS) * g


def _dot(a, b):
    return jnp.dot(a, b, preferred_element_type=F32)


def _dot_nt(a, b):
    return lax.dot_general(a, b, (((1,), (1,)), ((), ())), preferred_element_type=F32)


def _log_sigmoid(z):
    neg_abs = lax.bitcast_convert_type(lax.bitcast_convert_type(z, jnp.uint32) | jnp.uint32(0x80000000), F32)
    return jnp.minimum(z, 0.0) - jnp.log(1.0 + jnp.exp(neg_abs))


def _split_hi_lo(x):
    hi = x.astype(BF16)
    lo = (x - hi.astype(F32)).astype(BF16)
    return hi, lo


def _rope_block(x, cos, sin):
    half = ROPE_DIM // 2
    lane = lax.broadcasted_iota(jnp.int32, x.shape, 1)
    partner = jnp.where(lane % ROPE_DIM < half, pltpu.roll(x, LANES - half, 1), pltpu.roll(x, half, 1))
    return x * cos + partner * sin


def _proj_even_body(*refs, tm, seq_len, whole_seqs):
    if whole_seqs:
        (x_ref, g_ref, w_ref, wc_ref, e1_ref, e2_ref,
         q_ref, k_ref, v_ref, kb_ref, vb_ref, gc_ref, ci_ref) = refs
    else:
        (x_ref, g_ref, w_ref, wc_ref, wkv_ref,
         q_ref, k_ref, v_ref, kb_ref, vb_ref, gc_ref, cs_ref, tail_ref) = refs
    if not whole_seqs:
        @pl.when((pl.program_id(0) % (seq_len // tm)) == 0)
        def _():
            tail_ref[...] = jnp.zeros_like(tail_ref)

    h = _rms(x_ref[...], g_ref[...]).astype(BF16)

    def proj(j):
        return _dot(h, w_ref[:, j * SB_WIDTH:(j + 1) * SB_WIDTH])

    g_post = proj(3)
    ci = proj(4) * proj(5)
    r1 = pltpu.roll(ci, 1, 0)
    r2 = pltpu.roll(ci, 2, 0)
    row = lax.broadcasted_iota(jnp.int32, (tm, 1), 0)
    if whole_seqs:
        tpos = row % seq_len
        s1 = jnp.where(tpos < 1, e1_ref[...], r1)
        s2 = jnp.where(tpos < 2, e2_ref[...], r2)
        ci_ref[...] = ci
    else:
        t1 = tail_ref[7:8, :]
        t2 = tail_ref[6:7, :]
        s1 = jnp.where(row == 0, t1, r1)
        s2 = jnp.where(row == 0, t2, jnp.where(row == 1, t1, r2))
        tail_ref[...] = ci[tm - 8:, :]
        cs_ref[0] = ci[tm - (CONV_W - 1):, :]
    conv = wc_ref[0:1, :] * s2 + wc_ref[1:2, :] * s1 + wc_ref[2:3, :] * ci
    gc_ref[...] = (g_post * conv).astype(BF16)

    q_ref[...] = (proj(0) * SB_SCALE).astype(BF16)
    if whole_seqs:
        k = proj(1)
        v = proj(2)
        k_ref[...] = k
        v_ref[...] = v
        kb_ref[...] = k.astype(BF16)
        vb_ref[...] = v.astype(BF16)
    else:
        k = _dot_nt(wkv_ref[:SB_WIDTH, :], h)
        v = _dot_nt(wkv_ref[SB_WIDTH:, :], h)
        k_ref[0] = k
        v_ref[0] = v
        kb_ref[0] = k.astype(BF16)
        vb_ref[0] = v.astype(BF16)


def _proj_even(x, g, w_in, w_kv_t, w_conv, conv_prev, *, batch, seq_len, tm):
    n = x.shape[0]
    whole = tm % seq_len == 0
    grid = (n // tm,)
    row_spec = lambda w: pl.BlockSpec((tm, w), lambda i: (i, 0))
    in_specs = [row_spec(D_MODEL), _const_spec((1, D_MODEL)), _const_spec(w_in.shape), _const_spec(w_conv.shape)]
    args = [x, g, w_in, w_conv]
    if whole:
        kv_shape, kv_spec = (n, SB_WIDTH), row_spec(SB_WIDTH)
    else:
        seq_tiles = seq_len // tm
        kv_shape = (batch, SB_WIDTH, seq_len)
        kv_spec = pl.BlockSpec((1, SB_WIDTH, tm), lambda i: (i // seq_tiles, 0, i % seq_tiles))
    out_shape = [jax.ShapeDtypeStruct((n, SB_WIDTH), BF16), jax.ShapeDtypeStruct(kv_shape, F32),
                 jax.ShapeDtypeStruct(kv_shape, F32), jax.ShapeDtypeStruct(kv_shape, BF16),
                 jax.ShapeDtypeStruct(kv_shape, BF16), jax.ShapeDtypeStruct((n, CONV_DIM), BF16)]
    out_specs = [row_spec(SB_WIDTH)] + [kv_spec] * 4 + [row_spec(CONV_DIM)]
    scratch = []
    if whole:
        if conv_prev is None:
            conv_prev = jnp.zeros((batch, CONV_W - 1, CONV_DIM), F32)
        e1 =jnp.zeros((batch, seq_len, CONV_DIM), F32).at[:, 0].set(conv_prev[:, 1])
        e2 = jnp.zeros((batch, seq_len, CONV_DIM), F32).at[:, 0].set(conv_prev[:, 0]).at[:, 1].set(conv_prev[:, 1])
        args += [e1.reshape(n, CONV_DIM), e2.reshape(n, CONV_DIM)]
        in_specs += [row_spec(CONV_DIM), row_spec(CONV_DIM)]
        out_shape.append(jax.ShapeDtypeStruct((n, CONV_DIM), F32))
        out_specs.append(row_spec(CONV_DIM))
    else:
        assert seq_len % tm == 0 and conv_prev is None
        args.append(w_kv_t)
        in_specs.append(_const_spec(w_kv_t.shape))
        out_shape.append(jax.ShapeDtypeStruct((batch, CONV_W - 1, CONV_DIM), F32))
        out_specs.append(pl.BlockSpec((1, CONV_W - 1, CONV_DIM), lambda i: (i // seq_tiles, 0, 0)))
        scratch.append(pltpu.VMEM((8, CONV_DIM), F32))
    outs = pl.pallas_call(
        functools.partial(_proj_even_body, tm=tm, seq_len=seq_len, whole_seqs=whole),
        grid=grid, in_specs=in_specs, out_specs=out_specs, out_shape=out_shape, scratch_shapes=scratch,
        compiler_params=_cparams(("arbitrary",)), name="proj_even")(*args)
    q, k, v, kb, vb, gc, last = outs
    if whole:
        last = last.reshape(batch, seq_len, CONV_DIM)[:, seq_len - (CONV_W - 1):]
    return q, k, v, kb, vb, gc, last


def _suffix_rhs(tk):
    j = np.arange(2 * tk)[:, None] % tk
    c = np.arange(2 * tk)[None, :]
    return jnp.asarray(np.where(c < tk, j > c, True), dtype=BF16)


def _sb_prompt_body(q_ref, k_ref, v_ref, r_ref, o_ref, carry_ref, acc_ref, *, tq, tk):
    qi = pl.program_id(1)
    ratio = tq // tk
    even = lax.broadcasted_iota(jnp.int32, (LANES, tk), 0) < SB_HEAD_DIM
    rhs = r_ref[...]
    carry_ref[...] = jnp.zeros_like(carry_ref)
    acc_ref[...] = jnp.zeros_like(acc_ref)
    qs = [q_ref[:, p * LANES:(p + 1) * LANES] for p in range(HEAD_PAIRS)]

    def block_diag(blk):
        zero = jnp.zeros_like(blk)
        return jnp.concatenate([jnp.where(even, blk, zero), jnp.where(even, zero, blk)], axis=1)

    def tile(kj, masked, r0=0):
        start = pl.multiple_of(kj * tk, tk)
        if masked:
            q_pos = qi * tq + r0 + lax.broadcasted_iota(jnp.int32, (tq - r0, tk), 0)
            k_pos = kj * tk + lax.broadcasted_iota(jnp.int32, (tq - r0, tk), 1)
            visible = k_pos < q_pos
        zs = [_dot(qs[p][r0:], block_diag(k_ref[0, p * LANES:(p + 1) * LANES, pl.ds(start, tk)]))
              for p in range(HEAD_PAIRS)]
        log_betas, splits = [], []
        for hd in range(SB_HEADS):
            z = zs[hd // 2][:, (hd % 2) * tk:(hd % 2 + 1) * tk]
            log_beta = _log_sigmoid(z)
            log_1m = log_beta - z
            if masked:
                log_1m = jnp.where(visible, log_1m, 0.0)
            hi, lo = _split_hi_lo(log_1m)
            log_betas.append(log_beta)
            splits.append(jnp.concatenate([hi, lo], axis=1))
        sums = [_dot(sp, rhs) for sp in splits]
        ws = []
        for hd in range(SB_HEADS):
            carry = carry_ref[hd, r0:, :]
            w = jnp.exp(log_betas[hd] + sums[hd][:, :tk] + carry)
            if masked:
                w = jnp.where(visible, w, 0.0)
            carry_ref[hd, r0:, :] = carry + sums[hd][:, tk:]
            ws.append(w.astype(BF16))
        for p in range(HEAD_PAIRS):
            vbd = block_diag(v_ref[0, p * LANES:(p + 1) * LANES, pl.ds(start, tk)])
            acc_ref[p, r0:, :] += _dot_nt(jnp.concatenate([ws[2 * p], ws[2 * p + 1]], axis=1), vbd)

    for t in reversed(range(ratio)):
        tile(qi * ratio + t, True, r0=t * tk)

    def body(it, c):
        tile(qi * ratio - 1 - 2 * it, False)
        tile(qi * ratio - 2 - 2 * it, False)
        return c

    lax.fori_loop(0, qi * (ratio // 2), body, 0)
    for p in range(HEAD_PAIRS):
        o_ref[:, p * LANES:(p + 1) * LANES] = acc_ref[p].astype(BF16)


def _sb_prompt(q, kb, vb, *, batch, seq_len, tq, tk):
    assert (tq // tk) % 2 == 0 and tq % tk == 0 and seq_len % tq == 0
    n = q.shape[0]
    nq = seq_len // tq
    rhs = _suffix_rhs(tk)
    q_spec = pl.BlockSpec((tq, SB_WIDTH), lambda b, i: (b * nq + i, 0))
    kv_spec = pl.BlockSpec((1, SB_WIDTH, seq_len), lambda b, i: (b, 0, 0))
    return pl.pallas_call(
        functools.partial(_sb_prompt_body, tq=tq, tk=tk),
        grid=(batch, nq),
        in_specs=[q_spec, kv_spec, kv_spec, _const_spec(rhs.shape)],
        out_specs=q_spec,
        out_shape=jax.ShapeDtypeStruct((n, SB_WIDTH), BF16),
        scratch_shapes=[pltpu.VMEM((SB_HEADS, tq, LANES), F32), pltpu.VMEM((HEAD_PAIRS, tq, LANES), F32)],
        compiler_params=_cparams(("arbitrary", "arbitrary")), name="sb_attn_prompt")(q, kb, vb, rhs)


def _sb_sample_body(q_ref, kn_ref, vn_ref, kc_ref, vc_ref, l_ref, o_ref, qh_ref, carry_ref, acc_ref,
                    *, dec, kblk, tk):
    j = pl.program_id(1)
    rows = SB_HEADS * dec
    rhs = l_ref[...]

    def tile(kt, vt, visible):
        z = _dot(qh_ref[...], kt)
        log_beta = _log_sigmoid(z)
        log_1m = log_beta - z
        if visible is not None:
            log_1m = jnp.where(visible, log_1m, 0.0)
        hi, lo = _split_hi_lo(log_1m)
        sums = _dot(jnp.concatenate([hi, lo], axis=1), rhs)
        w = jnp.exp(log_beta + sums[:, :tk] + carry_ref[...])
        if visible is not None:
            w = jnp.where(visible, w, 0.0)
        acc_ref[...] += _dot_nt(w.astype(BF16), vt)
        carry_ref[...] += sums[:, tk:]

    def new_keys_t(x):
        x = jnp.concatenate([x.astype(F32), jnp.zeros((tk - dec, SB_WIDTH), F32)], axis=0)
        return x.T.astype(BF16)

    @pl.when(j == 0)
    def _():
        q = q_ref[...]
        qt = jnp.concatenate([q] * SB_HEADS, axis=0)
        rr = lax.broadcasted_iota(jnp.int32, (rows, SB_WIDTH), 0)
        cc = lax.broadcasted_iota(jnp.int32, (rows, SB_WIDTH), 1)
        qh_ref[...] = jnp.where(rr // dec == cc // SB_HEAD_DIM, qt, jnp.zeros_like(qt))
        carry_ref[...] = jnp.zeros_like(carry_ref)
        acc_ref[...] = jnp.zeros_like(acc_ref)
        query_t = lax.broadcasted_iota(jnp.int32, (rows, tk), 0) % dec
        key_i = lax.broadcasted_iota(jnp.int32, (rows, tk), 1)
        tile(new_keys_t(kn_ref[...]), new_keys_t(vn_ref[...]),
             key_i < query_t)

    for s in reversed(range(kblk // tk)):
        tile(kc_ref[0, :, s * tk:(s + 1) * tk].astype(BF16), vc_ref[0, :, s * tk:(s + 1) * tk].astype(BF16), None)

    @pl.when(j == pl.num_programs(1) - 1)
    def _():
        cc = lax.broadcasted_iota(jnp.int32, (dec, SB_WIDTH), 1)
        out = jnp.zeros((dec, SB_WIDTH), F32)
        for h in range(SB_HEADS):
            out = out + jnp.where(cc // SB_HEAD_DIM == h, acc_ref[h * dec:(h + 1) * dec, :], 0.0)
        o_ref[...] = out.astype(BF16)


def _sb_sample(q, kb, vb, cache_kt, cache_vt, *, batch, dec, kblk=1024, tk=256):
    n = q.shape[0]
    past = cache_kt.shape[2]
    nkb = past // kblk
    rows = SB_HEADS * dec
    rhs = _suffix_rhs(tk)
    new_spec = pl.BlockSpec((dec, SB_WIDTH), lambda b, j: (b, 0))
    cache_spec = pl.BlockSpec((1, SB_WIDTH, kblk), lambda b, j: (b, 0, nkb - 1 - j))
    return pl.pallas_call(
        functools.partial(_sb_sample_body, dec=dec, kblk=kblk, tk=tk),
        grid=(batch, nkb),
        in_specs=[new_spec, new_spec, new_spec, cache_spec, cache_spec, _const_spec(rhs.shape)],
        out_specs=new_spec,
        out_shape=jax.ShapeDtypeStruct((n, SB_WIDTH), BF16),
        scratch_shapes=[pltpu.VMEM((rows, SB_WIDTH), BF16), pltpu.VMEM((rows, tk), F32),
                        pltpu.VMEM((rows, SB_WIDTH), F32)],
        compiler_params=_cparams(("arbitrary", "arbitrary")), name="sb_attn_sample")(
            q, kb, vb, cache_kt, cache_vt, rhs)


def _proj_odd_body(*refs, tm, gate_len, emit_vn, kpe_transposed):
    (x_ref, g_ref, w_ref, lng_ref, lnb_ref, ws_ref, bs_ref, qg_ref, kvg_ref, wqn_ref, wqp_ref,
     cos_ref, sin_ref) = refs[:13]
    outs = refs[13:]
    if emit_vn:
        sgu_ref, vn_ref, qn_ref, qp_ref, ckv_ref, ckvb_ref, kpe_ref, kpeb_ref = outs
    else:
        sgu_ref, qn_ref, qp_ref, ckv_ref, ckvb_ref, kpe_ref, kpeb_ref = outs
    h = _rms(x_ref[...], g_ref[...]).astype(BF16)
    o_v, o_q, o_kv, o_pe = SGU_DIM, 2 * SGU_DIM, 2 * SGU_DIM + Q_LORA, 2 * SGU_DIM + Q_LORA + KV_LORA
    cos = cos_ref[...]
    sin = sin_ref[...]
    kpe = _rope_block(_dot(h, w_ref[:, o_pe:]), cos, sin)
    if kpe_transposed:
        kpe_ref[0] = kpe.T[:ROPE_DIM, :]
    else:
        kpe_ref[...] = kpe[:, :ROPE_DIM]
    kpeb_ref[...] = kpe.astype(BF16)
    ckv = _rms(_dot(h, w_ref[:, o_kv:o_pe]), kvg_ref[...])
    ckv_ref[...] = ckv
    ckvb_ref[...] = ckv.astype(BF16)
    cq = _rms(_dot(h, w_ref[:, o_q:o_kv]), qg_ref[...]).astype(BF16)
    for blk in range(MLA_HEADS * ROPE_DIM // LANES):
        bs = slice(blk * LANES, (blk + 1) * LANES)
        qp_ref[:, bs] = _rope_block(_dot(cq, wqp_ref[:, bs]), cos, sin).astype(BF16)
    qn_ref[...] = _dot(cq, wqn_ref[...]).astype(BF16)
    u = _dot(h, w_ref[:, :o_v])
    v = _dot(h, w_ref[:, o_v:o_q])
    mu = jnp.mean(v, axis=-1, keepdims=True)
    vc = v - mu
    var = jnp.mean(vc * vc, axis=-1, keepdims=True)
    vn = vc * lax.rsqrt(var + EPS) * lng_ref[...] + lnb_ref[...]
    if emit_vn:
        vn_ref[...] = vn
    vnb = vn.astype(BF16)
    rr = lax.broadcasted_iota(jnp.int32, (SGU_CHUNK, SGU_CHUNK), 0)
    cc = lax.broadcasted_iota(jnp.int32, (SGU_CHUNK, SGU_CHUNK), 1)
    causal = (rr // gate_len == cc // gate_len) & (cc <= rr)
    for g in range(SGU_GROUPS):
        gs = slice(g * SGU_GROUP_DIM, (g + 1) * SGU_GROUP_DIM)
        wg = jnp.where(causal, ws_ref[g], 0.0).astype(BF16)
        for c in range(tm // SGU_CHUNK):
            rs = slice(c * SGU_CHUNK, (c + 1) * SGU_CHUNK)
            s = _dot(wg, vnb[rs, gs]) + bs_ref[:, gs]
            sgu_ref[rs, gs] = (u[rs, gs] * s).astype(BF16)


def _proj_odd(x, g, w_in, ln_g, ln_b, w_s, b_s, qg, kvg, wqn, wqp, cos, sin, *, batch, seq_len, tm, gate_len,
              emit_vn):
    n = x.shape[0]
    pos_tiles = cos.shape[0] // tm
    kpe_transposed = tm % seq_len != 0
    if kpe_transposed:
        seq_tiles = seq_len // tm
        kpe_shape = (batch, ROPE_DIM, seq_len)
        kpe_spec = pl.BlockSpec((1, ROPE_DIM, tm), lambda i: (i // seq_tiles, 0, i % seq_tiles))
    else:
        kpe_shape, kpe_spec = (n, ROPE_DIM), pl.BlockSpec((tm, ROPE_DIM), lambda i: (i, 0))
    row_spec = lambda w: pl.BlockSpec((tm, w), lambda i: (i, 0))
    tab_spec = pl.BlockSpec((tm, LANES), lambda i: (i % pos_tiles, 0))
    consts = [g, w_in, ln_g, ln_b, w_s, b_s, qg, kvg, wqn, wqp]
    out_shape = [jax.ShapeDtypeStruct((n, SGU_DIM), BF16)]
    out_specs = [row_spec(SGU_DIM)]
    if emit_vn:
        out_shape.append(jax.ShapeDtypeStruct((n, SGU_DIM), F32))
        out_specs.append(row_spec(SGU_DIM))
    out_shape += [jax.ShapeDtypeStruct((n, MLA_HEADS * NOPE_DIM), BF16),
                  jax.ShapeDtypeStruct((n, MLA_HEADS * ROPE_DIM), BF16),
                  jax.ShapeDtypeStruct((n, KV_LORA), F32), jax.ShapeDtypeStruct((n, KV_LORA), BF16),
                  jax.ShapeDtypeStruct(kpe_shape, F32), jax.ShapeDtypeStruct((n, LANES), BF16)]
    out_specs += [row_spec(MLA_HEADS * NOPE_DIM), row_spec(MLA_HEADS * ROPE_DIM),
                  row_spec(KV_LORA), row_spec(KV_LORA), kpe_spec, row_spec(LANES)]
    return pl.pallas_call(
        functools.partial(_proj_odd_body, tm=tm, gate_len=gate_len, emit_vn=emit_vn, kpe_transposed=kpe_transposed),
        grid=(n // tm,),
        in_specs=[row_spec(D_MODEL)] + [_const_spec(c.shape) for c in consts] + [tab_spec, tab_spec],
        out_specs=out_specs, out_shape=out_shape,
        compiler_params=_cparams(("arbitrary",)), name="proj_odd")(x, *consts, cos, sin)


def _mla_queries(qn, qpe, wuk_ref, qcat_ref, tq):
    lane = lax.broadcasted_iota(jnp.int32, (tq, LANES), 1)
    low = lane < NOPE_DIM
    qpe = qpe.astype(F32)
    qcat_ref[:, KV_LORA:] = jnp.zeros((MLA_HEADS * tq, MLA_QK - KV_LORA), BF16)
    for hd in range(MLA_HEADS):
        p = hd // 2
        rs = slice(hd * tq, (hd + 1) * tq)
        pair = qn[:, p * LANES:(p + 1) * LANES]
        qm = jnp.where(low if hd % 2 == 0 else jnp.logical_not(low), pair, jnp.zeros_like(pair))
        qcat_ref[rs, :KV_LORA] = _dot(qm, wuk_ref[p]).astype(BF16)
        qcat_ref[rs, KV_LORA:KV_LORA + ROPE_DIM] = qpe[:, hd * ROPE_DIM:(hd + 1) * ROPE_DIM].astype(BF16)


def _lane_tile(x, width):
    return jnp.concatenate([x] * (width // LANES), axis=1)


def _mla_scores(qcat_ref, kcat, sc):
    sc[0][...] = _dot_nt(qcat_ref[...], kcat)


def _mla_softmax(sc, visible_fn):
    s_ref, p_ref, m_ref, l_ref, acc_ref = sc
    rows, tk = s_ref.shape
    for c in range(rows // MLA_ROW_CHUNK):
        rs = slice(c * MLA_ROW_CHUNK, (c + 1) * MLA_ROW_CHUNK)
        s = s_ref[rs, :] * (MLA_SCALE * LOG2_E)
        if visible_fn is not None:
            s = jnp.where(visible_fn(c), s, NEG_INF)
        m_prev = m_ref[rs, :]
        m_new = jnp.maximum(m_prev, jnp.max(s, axis=-1, keepdims=True))
        alpha = jnp.exp2(m_prev - m_new)
        p = jnp.exp2(s - _lane_tile(m_new, tk))
        l_ref[rs, :] = alpha * l_ref[rs, :] + jnp.sum(p, axis=-1, keepdims=True)
        m_ref[rs, :] = m_new
        p_ref[rs, :] = p.astype(BF16)
        acc_ref[rs, :] = acc_ref[rs, :] * _lane_tile(alpha, KV_LORA)


def _mla_pv(ck, sc):
    sc[4][...] += _dot(sc[1][...], ck)


def _mla_tile(qcat_ref, kcat, sc, visible_fn):
    _mla_scores(qcat_ref, kcat, sc)
    _mla_softmax(sc, visible_fn)
    _mla_pv(kcat[:, :KV_LORA], sc)


def _mla_finish(sc, wuv_ref, o_ref, tq):
    l_ref, acc_ref = sc[3], sc[4]
    o_lat = (acc_ref[...] / _lane_tile(l_ref[...], KV_LORA)).astype(BF16)
    for p in range(MLA_HEADS // 2):
        h0, h1 = 2 * p, 2 * p + 1
        o_ref[:, p * LANES:(p + 1) * LANES] = (
            _dot(o_lat[h0 * tq:(h0 + 1) * tq], wuv_ref[h0]) + _dot(o_lat[h1 * tq:(h1 + 1) * tq], wuv_ref[h1])
        ).astype(BF16)


def _mla_init(sc):
    m_ref, l_ref, acc_ref = sc[2], sc[3], sc[4]
    m_ref[...] = jnp.full_like(m_ref, -jnp.inf)
    l_ref[...] = jnp.zeros_like(l_ref)
    acc_ref[...] = jnp.zeros_like(acc_ref)


def _mla_prompt_body(qn_ref, qp_ref, ckv_ref, kpe_ref, wuk_ref, wuv_ref, o_ref, *scratch, tq, tk, streams):
    qi = pl.program_id(1)
    rows = MLA_HEADS * tq
    per = len(scratch) // streams
    qcats = [scratch[st * per] for st in range(streams)]
    scs = [scratch[st * per + 1:(st + 1) * per] for st in range(streams)]
    for st in range(streams):
        _mla_queries(qn_ref[st], qp_ref[st], wuk_ref, qcats[st], tq)
        _mla_init(scs[st])

    n_full = (qi * tq) // tk
    col = lax.broadcasted_iota(jnp.int32, (MLA_ROW_CHUNK, tk), 1)

    def visible(c):
        q_chunk_end = qi * tq + ((c * MLA_ROW_CHUNK) % tq) // CHUNK * CHUNK + CHUNK
        return col < q_chunk_end - n_full * tk

    def tile(kj, visible_fn):
        start = pl.multiple_of(kj * tk, tk)
        kcats = [jnp.concatenate([ckv_ref[st, pl.ds(start, tk), :], kpe_ref[st, pl.ds(start, tk), :],
                                  jnp.zeros((tk, MLA_QK - KV_LORA - LANES), BF16)], axis=1) for st in range(streams)]
        for st in range(streams):
            _mla_scores(qcats[st], kcats[st], scs[st])
        for st in range(streams):
            _mla_softmax(scs[st], visible_fn)
        for st in range(streams):
            _mla_pv(kcats[st][:, :KV_LORA], scs[st])

    def body(kj, c):
        tile(kj, None)
        return c

    lax.fori_loop(0, n_full, body, 0)
    tile(n_full, visible)
    for st in range(streams):
        _mla_finish(scs[st], wuv_ref, o_ref.at[st], tq)


def _mla_scratch(rows, tk):
    return [pltpu.VMEM((rows, MLA_QK), BF16), pltpu.VMEM((rows, tk), F32), pltpu.VMEM((rows, tk), BF16),
            pltpu.VMEM((rows, LANES), F32), pltpu.VMEM((rows, LANES), F32), pltpu.VMEM((rows, KV_LORA), F32)]


def _mla_prompt(qn, qp, ckvb, kpeb, wuk, wuv, *, batch, seq_len, tq, tk):
    assert tq % CHUNK == 0 and CHUNK % MLA_ROW_CHUNK == 0 and seq_len % tq == 0 and seq_len % tk == 0
    n = qn.shape[0]
    nq = seq_len // tq
    width = MLA_HEADS * V_DIM
    streams = MLA_STREAMS if batch % MLA_STREAMS == 0 else 1
    seq3 = lambda a: a.reshape(batch, seq_len, a.shape[-1])
    out = pl.pallas_call(
        functools.partial(_mla_prompt_body, tq=tq, tk=tk, streams=streams),
        grid=(batch // streams, nq),
        in_specs=[pl.BlockSpec((streams, tq, MLA_HEADS * NOPE_DIM), lambda g, i: (g, i, 0)),
                  pl.BlockSpec((streams, tq, MLA_HEADS * ROPE_DIM), lambda g, i: (g, i, 0)),
                  pl.BlockSpec((streams, seq_len, KV_LORA), lambda g, i: (g, 0, 0)),
                  pl.BlockSpec((streams, seq_len, LANES), lambda g, i: (g, 0, 0)),
                  _const_spec(wuk.shape), _const_spec(wuv.shape)],
        out_specs=pl.BlockSpec((streams, tq, width), lambda g, i: (g, i, 0)),
        out_shape=jax.ShapeDtypeStruct((batch, seq_len, width), BF16),
        scratch_shapes=_mla_scratch(MLA_HEADS * tq, tk) * streams,
        compiler_params=_cparams(("arbitrary", "arbitrary")), name="mla_attn_prompt")(
            seq3(qn), seq3(qp), seq3(ckvb), seq3(kpeb), wuk, wuv)
    return out.reshape(n, width)


def _mla_sample_body(qn_ref, qp_ref, cn_ref, pn_ref, cc_ref, pc_ref, wuk_ref, wuv_ref, o_ref, kcat_ref, qcat_ref,
                     *sc, dec, past, kblk, tk):
    j = pl.program_id(1)
    rows = MLA_HEADS * dec

    @pl.when(j == 0)
    def _():
        _mla_queries(qn_ref[...], qp_ref[...], wuk_ref, qcat_ref, dec)
        _mla_init(sc)
        kcat_ref[:, KV_LORA:] = jnp.zeros((tk, MLA_QK - KV_LORA), BF16)
        new = jnp.concatenate([cn_ref[...], pn_ref[...], jnp.zeros((dec, MLA_QK - KV_LORA - LANES), BF16)], axis=1)
        kcat = jnp.concatenate([new, jnp.zeros((tk - dec, MLA_QK), BF16)], axis=0)

        def visible(c):
            col = lax.broadcasted_iota(jnp.int32, (MLA_ROW_CHUNK, tk), 1)
            row = c * MLA_ROW_CHUNK + lax.broadcasted_iota(jnp.int32, (MLA_ROW_CHUNK, tk), 0)
            return (col < dec) & ((past + col) // CHUNK <= (past + row % dec) // CHUNK)

        _mla_tile(qcat_ref, kcat, sc, visible)

    def body(it, c):
        start = pl.multiple_of(it * tk, tk)
        kcat_ref[:, :KV_LORA] = cc_ref[0, pl.ds(start, tk), :].astype(BF16)
        kp_t = jnp.concatenate([pc_ref[0, :, pl.ds(start, tk)], jnp.zeros((LANES - ROPE_DIM, tk), F32)], axis=0)
        kcat_ref[:, KV_LORA:KV_LORA + LANES] = kp_t.T.astype(BF16)
        _mla_tile(qcat_ref, kcat_ref[...], sc, None)
        return c

    lax.fori_loop(0, kblk // tk, body, 0)

    @pl.when(j == pl.num_programs(1) - 1)
    def _():
        _mla_finish(sc, wuv_ref, o_ref, dec)


def _mla_sample(qn, qp, ckvb, kpeb, cache_ckv, cache_kpe_t, wuk, wuv, *, batch, dec, kblk=1024, tk=1024):
    n = qn.shape[0]
    past = cache_ckv.shape[1]
    width = MLA_HEADS * V_DIM
    return pl.pallas_call(
        functools.partial(_mla_sample_body, dec=dec, past=past, kblk=kblk, tk=tk),
        grid=(batch, past // kblk),
        in_specs=[pl.BlockSpec((dec, MLA_HEADS * NOPE_DIM), lambda b, j: (b, 0)),
                  pl.BlockSpec((dec, MLA_HEADS * ROPE_DIM), lambda b, j: (b, 0)),
                  pl.BlockSpec((dec, KV_LORA), lambda b, j: (b, 0)),
                  pl.BlockSpec((dec, LANES), lambda b, j: (b, 0)),
                  pl.BlockSpec((1, kblk, KV_LORA), lambda b, j: (b, j, 0)),
                  pl.BlockSpec((1, ROPE_DIM, kblk), lambda b, j: (b, 0, j)),
                  _const_spec(wuk.shape), _const_spec(wuv.shape)],
        out_specs=pl.BlockSpec((dec, width), lambda b, j: (b, 0)),
        out_shape=jax.ShapeDtypeStruct((n, width), BF16),
        scratch_shapes=[pltpu.VMEM((tk, MLA_QK), BF16)] + _mla_scratch(MLA_HEADS * dec, tk),
        compiler_params=_cparams(("arbitrary", "arbitrary")), name="mla_attn_sample")(
            qn, qp, ckvb, kpeb, cache_ckv, cache_kpe_t, wuk, wuv)


def _out_ffn_body(a_ref, b_ref, x_ref, woa_ref, wob_ref, gpost_ref, gpre_ref, wup_ref, wdn_ref, gfpost_ref, o_ref):
    tm = x_ref.shape[0]
    groups = [slice(r * tm // FFN_ROW_GROUPS, (r + 1) * tm // FFN_ROW_GROUPS) for r in range(FFN_ROW_GROUPS)]
    mixed = [_dot(a_ref[rs, :], woa_ref[...]) + _dot(b_ref[rs, :], wob_ref[...]) for rs in groups]
    x1 = [x_ref[rs, :] + _rms(m, gpost_ref[...]) for rs, m in zip(groups, mixed)]
    h = [_rms(x, gpre_ref[...]).astype(BF16) for x in x1]
    down = [jnp.zeros_like(x) for x in x1]
    for c in range(D_FF // FF_CHUNK):
        for r in range(FFN_ROW_GROUPS):
            up = _dot(h[r], wup_ref[:, c * FF_CHUNK:(c + 1) * FF_CHUNK])
            act = jnp.square(jnp.maximum(up, 0.0)).astype(BF16)
            down[r] = down[r] + _dot(act, wdn_ref[c * FF_CHUNK:(c + 1) * FF_CHUNK, :])
    for r, rs in enumerate(groups):
        o_ref[rs, :] = x1[r] + _rms(down[r], gfpost_ref[...])


def _out_ffn(a, b, x, woa, wob, g_post, g_pre, w_up, w_down, g_fpost, *, tm):
    n = x.shape[0]
    row_spec = lambda w: pl.BlockSpec((tm, w), lambda i: (i, 0))
    consts = [woa, wob, g_post, g_pre, w_up, w_down, g_fpost]
    return pl.pallas_call(
        _out_ffn_body,
        grid=(n // tm,),
        in_specs=[row_spec(a.shape[1]), row_spec(b.shape[1]), row_spec(D_MODEL)] + [_const_spec(c.shape) for c in consts],
        out_specs=row_spec(D_MODEL),
        out_shape=jax.ShapeDtypeStruct((n, D_MODEL), F32),
        compiler_params=_cparams(("arbitrary",)), name="out_ffn")(a, b, x, *consts)


def _rope_tables(pos, reps):
    half = ROPE_DIM // 2
    inv = ROPE_THETA ** (-jnp.arange(half, dtype=F32) / half)
    ang = pos.astype(F32)[:, None] * inv[None, :]
    cos = jnp.tile(jnp.concatenate([jnp.cos(ang), jnp.cos(ang)], axis=1), (reps, LANES // ROPE_DIM))
    sin = jnp.tile(jnp.concatenate([-jnp.sin(ang), jnp.sin(ang)], axis=1), (reps, LANES // ROPE_DIM))
    return cos, sin


def _prep_even(p, j):
    w_in = p["even_w_in"][j]
    return dict(w_in=w_in.astype(BF16), w_kv_t=w_in[:, SB_WIDTH:3 * SB_WIDTH].T.astype(BF16), w_conv=p["even_w_conv"][j],
                woa=p["even_w_out"][j, :SB_WIDTH].astype(BF16), wob=p["even_w_out"][j, SB_WIDTH:].astype(BF16))


def _prep_odd(p, j, gate_len):
    w_in = p["odd_w_in"][j]
    w_in = jnp.pad(w_in, ((0, 0), (0, ODD_IN_PAD - w_in.shape[1]))).astype(BF16)
    reps = SGU_CHUNK // gate_len
    w_s = jnp.tile(p["sgu_w_s"][j, :, :gate_len, :gate_len], (1, reps, reps))
    b_s = jnp.tile(p["sgu_b_s"][j, :, :gate_len], (1, reps))
    b_s = jnp.repeat(b_s.T, SGU_GROUP_DIM, axis=1)
    w_uq = p["mla_w_uq"][j].reshape(Q_LORA, MLA_HEADS, NOPE_DIM + ROPE_DIM)
    wqn = w_uq[:, :, :NOPE_DIM].reshape(Q_LORA, MLA_HEADS * NOPE_DIM).astype(BF16)
    wqp = w_uq[:, :, NOPE_DIM:].reshape(Q_LORA, MLA_HEADS * ROPE_DIM).astype(BF16)
    wuk = p["mla_w_uk"][j].reshape(MLA_HEADS // 2, 2 * NOPE_DIM, KV_LORA).astype(BF16)
    w_uv = p["mla_w_uv"][j]
    wuv = jnp.stack([jnp.pad(w_uv[h], ((0, 0), ((h % 2) * V_DIM, (1 - h % 2) * V_DIM))) for h in range(MLA_HEADS)])
    return dict(w_in=w_in, ln_g=p["sgu_ln_g"][j][None], ln_b=p["sgu_ln_b"][j][None], w_s=w_s, b_s=b_s,
                qg=p["mla_q_norm_g"][j][None], kvg=p["mla_kv_norm_g"][j][None], wqn=wqn, wqp=wqp, wuk=wuk,
                wuv=wuv.astype(BF16),
                woa=p["odd_w_out"][j, :SGU_DIM].astype(BF16), wob=p["odd_w_out"][j, SGU_DIM:].astype(BF16))


def _run_trunk(x, pos, past, p, *, batch, seq_len):
    depth = p["mix_pre_g"].shape[0]
    n = batch * seq_len
    x = x.reshape(n, D_MODEL)
    is_sample = past is not None
    tm = min(ROW_TILE, n)
    assert n % tm == 0 and (tm % seq_len == 0 or seq_len % tm == 0) and tm % SGU_CHUNK == 0
    gate_len = min(seq_len, SGU_CHUNK)
    tm_proj = PROJ_ROW_TILE if (seq_len % PROJ_ROW_TILE == 0 and seq_len > PROJ_ROW_TILE) else tm
    cos, sin = _rope_tables(pos, max(1, tm_proj // seq_len))
    st = {k: [] for k in ("sb_k", "sb_v", "conv", "ckv", "kpe", "sgu_v")}
    for layer in range(depth):
        j = layer // 2
        g_pre = p["mix_pre_g"][layer][None]
        if layer % 2 == 0:
            w = _prep_even(p, j)
            conv_prev = past["conv"][j] if is_sample else None
            q, k, v, kb, vb, b_mix, conv_state = _proj_even(x, g_pre, w["w_in"], w["w_kv_t"], w["w_conv"], conv_prev,
                                                            batch=batch, seq_len=seq_len, tm=tm_proj)
            if is_sample:
                assert tm % seq_len == 0

                def cache_t(c):
                    return jnp.transpose(c, (0, 2, 3, 1)).reshape(batch, SB_WIDTH, -1)

                a_mix = _sb_sample(q, kb, vb, cache_t(past["sb_k"][j]), cache_t(past["sb_v"][j]),
                                   batch=batch, dec=seq_len)
                k, v = (t.reshape(batch, seq_len, SB_HEADS, SB_HEAD_DIM) for t in (k, v))
            else:
                assert seq_len % tm == 0
                a_mix = _sb_prompt(q, kb, vb, batch=batch, seq_len=seq_len, tq=SB_TQ, tk=SB_TK)
                k, v = (jnp.transpose(t.reshape(batch, SB_HEADS, SB_HEAD_DIM, seq_len), (0, 3, 1, 2)) for t in (k, v))
            st["sb_k"].append(k)
            st["sb_v"].append(v)
            st["conv"].append(conv_state)
        else:
            w = _prep_odd(p, j, gate_len)
            outs = _proj_odd(x, g_pre, w["w_in"], w["ln_g"], w["ln_b"], w["w_s"], w["b_s"], w["qg"], w["kvg"],
                             w["wqn"], w["wqp"], cos, sin, batch=batch, seq_len=seq_len, tm=tm_proj, gate_len=gate_len,
                             emit_vn=is_sample)
            if is_sample:
                a_mix, vn, qn, qp, ckv, ckvb, kpe, kpeb = outs
                st["sgu_v"].append(vn.reshape(batch, seq_len, SGU_DIM))
                b_mix = _mla_sample(qn, qp, ckvb, kpeb, past["ckv"][j], jnp.transpose(past["kpe"][j], (0, 2, 1)),
                                    w["wuk"], w["wuv"],
                                    batch=batch, dec=seq_len)
            else:
                a_mix, qn, qp, ckv, ckvb, kpe, kpeb = outs
                b_mix = _mla_prompt(qn, qp, ckvb, kpeb, w["wuk"], w["wuv"], batch=batch, seq_len=seq_len,
                                    tq=MLA_TQ, tk=MLA_TK)
            st["ckv"].append(ckv.reshape(batch, seq_len, KV_LORA))
            st["kpe"].append(jnp.transpose(kpe, (0, 2, 1)) if kpe.ndim == 3 else kpe.reshape(batch, seq_len, ROPE_DIM))
        x = _out_ffn(a_mix, b_mix, x, w["woa"], w["wob"], p["mix_post_g"][layer][None], p["ffn_pre_g"][layer][None],
                     p["ffn_w_up"][layer].astype(BF16), p["ffn_w_down"][layer].astype(BF16),
                     p["ffn_post_g"][layer][None], tm=tm)
    states = {k: jnp.stack(v) for k, v in st.items() if v}
    return x.reshape(batch, seq_len, D_MODEL), states


def kernel(x_prompt, x_sample, cache_sb_k, cache_sb_v, state_conv, cache_mla_ckv, cache_mla_kpe,
           mix_pre_g, mix_post_g, ffn_pre_g, ffn_post_g, even_w_in, even_w_conv, even_w_out,
           odd_w_in, sgu_ln_g, sgu_ln_b, sgu_w_s, sgu_b_s, mla_q_norm_g, mla_kv_norm_g,
           mla_w_uq, mla_w_uk, mla_w_uv, odd_w_out, ffn_w_up, ffn_w_down):
    params = {
        "mix_pre_g": mix_pre_g, "mix_post_g": mix_post_g, "ffn_pre_g": ffn_pre_g, "ffn_post_g": ffn_post_g,
        "even_w_in": even_w_in, "even_w_conv": even_w_conv, "even_w_out": even_w_out,
        "odd_w_in": odd_w_in, "sgu_ln_g": sgu_ln_g, "sgu_ln_b": sgu_ln_b, "sgu_w_s": sgu_w_s,
        "sgu_b_s": sgu_b_s, "mla_q_norm_g": mla_q_norm_g, "mla_kv_norm_g": mla_kv_norm_g,
        "mla_w_uq": mla_w_uq, "mla_w_uk": mla_w_uk, "mla_w_uv": mla_w_uv, "odd_w_out": odd_w_out,
        "ffn_w_up": ffn_w_up, "ffn_w_down": ffn_w_down,
    }
    batch, seq_len, _ = x_prompt.shape
    pos_p = jnp.arange(seq_len, dtype=jnp.int32)
    y_prompt, st_p = _run_trunk(x_prompt, pos_p, None, params, batch=batch, seq_len=seq_len)
    dec_batch, dec_seq, _ = x_sample.shape
    past_len = cache_sb_k.shape[2]
    pos_s = past_len + jnp.arange(dec_seq, dtype=jnp.int32)
    past = {"sb_k": cache_sb_k, "sb_v": cache_sb_v, "conv": state_conv, "ckv": cache_mla_ckv, "kpe": cache_mla_kpe}
    y_sample, st_s = _run_trunk(x_sample, pos_s, past, params, batch=dec_batch, seq_len=dec_seq)
    return (y_prompt, y_sample,
            st_p["sb_k"], st_p["sb_v"], st_p["conv"], st_p["ckv"], st_p["kpe"],
            st_s["sb_k"], st_s["sb_v"], st_s["conv"], st_s["ckv"], st_s["kpe"], st_s["sgu_v"])
```

```python
import functools
import math

import numpy as np
import jax
import jax.numpy as jnp
from jax import lax
from jax.experimental import pallas as pl
from jax.experimental.pallas import tpu as pltpu

F32 = jnp.float32
BF16 = jnp.bfloat16

EPS = 1e-6
D_MODEL = 1024
CHUNK = 64
SB_HEADS = 8
SB_HEAD_DIM = 64
SB_WIDTH = SB_HEADS * SB_HEAD_DIM
SB_SCALE = 1.0 / math.sqrt(SB_HEAD_DIM)
CONV_DIM = D_MODEL // 2
CONV_W = 3
SGU_CHUNK = 128
SGU_GROUPS = 4
SGU_DIM = D_MODEL // 2
SGU_GROUP_DIM = SGU_DIM // SGU_GROUPS
MLA_HEADS = 8
Q_LORA = 384
KV_LORA = 256
NOPE_DIM = 64
ROPE_DIM = 32
V_DIM = 64
ROPE_THETA = 10000.0
MLA_SCALE = 1.0 / math.sqrt(NOPE_DIM + ROPE_DIM)
D_FF = 4 * D_MODEL
FF_CHUNK = 1024
FFN_ROW_GROUPS = 2

LANES = 128
HEAD_PAIRS = SB_HEADS // 2
MLA_QK = 2 * KV_LORA
ODD_IN_PAD =2 * SGU_DIM + Q_LORA + KV_LORA + LANES
VMEM_LIMIT = 56 * 1024 * 1024
NEG_INF = -1e30
LOG2_E = math.log2(math.e)

ROW_TILE = 512
PROJ_ROW_TILE = 1024
SB_TQ, SB_TK = 512, 128
MLA_TQ, MLA_TK = 256, 256
MLA_ROW_CHUNK = 64


def _cparams(sem):
    return pltpu.CompilerParams(dimension_semantics=sem, vmem_limit_bytes=VMEM_LIMIT)


def _const_spec(shape):
    nd = len(shape)
    return pl.BlockSpec(shape, lambda *_: (0,) * nd, pipeline_mode=pl.Buffered(1))


def _rms(x, g):
    return x * lax.rsqrt(jnp.mean(x * x, axis=-1, keepdims=True) + EPS) * g


def _dot(a, b):
    return jnp.dot(a, b, preferred_element_type=F32)


def _dot_nt(a, b):
    return lax.dot_general(a, b, (((1,), (1,)), ((), ())), preferred_element_type=F32)


def _log_sigmoid(z):
    neg_abs = lax.bitcast_convert_type(lax.bitcast_convert_type(z, jnp.uint32) | jnp.uint32(0x80000000), F32)
    return jnp.minimum(z, 0.0) - jnp.log(1.0 + jnp.exp(neg_abs))


def _split_hi_lo(x):
    hi = x.astype(BF16)
    lo = (x - hi.astype(F32)).astype(BF16)
    return hi, lo


def _rope_block(x, cos, sin):
    half = ROPE_DIM // 2
    lane = lax.broadcasted_iota(jnp.int32, x.shape, 1)
    partner = jnp.where(lane % ROPE_DIM < half, pltpu.roll(x, LANES - half, 1), pltpu.roll(x, half, 1))
    return x * cos + partner * sin


def _proj_even_body(*refs, tm, seq_len, whole_seqs):
    if whole_seqs:
        (x_ref, g_ref, w_ref, wc_ref, e1_ref, e2_ref,
         q_ref, k_ref, v_ref, kb_ref, vb_ref, gc_ref, ci_ref) = refs
    else:
        (x_ref, g_ref, w_ref, wc_ref, wkv_ref,
         q_ref, k_ref, v_ref, kb_ref, vb_ref, gc_ref, cs_ref, tail_ref) = refs
    if not whole_seqs:
        @pl.when((pl.program_id(0) % (seq_len // tm)) == 0)
        def _():
            tail_ref[...] = jnp.zeros_like(tail_ref)

    h = _rms(x_ref[...], g_ref[...]).astype(BF16)

    def proj(j):
        return _dot(h, w_ref[:, j * SB_WIDTH:(j + 1) * SB_WIDTH])

    g_post = proj(3)
    ci = proj(4) * proj(5)
    r1 = pltpu.roll(ci, 1, 0)
    r2 = pltpu.roll(ci, 2, 0)
    row = lax.broadcasted_iota(jnp.int32, (tm, 1), 0)
    if whole_seqs:
        tpos = row % seq_len
        s1 = jnp.where(tpos < 1, e1_ref[...], r1)
        s2 = jnp.where(tpos < 2, e2_ref[...], r2)
        ci_ref[...] = ci
    else:
        t1 = tail_ref[7:8, :]
        t2 = tail_ref[6:7, :]
        s1 = jnp.where(row == 0, t1, r1)
        s2 = jnp.where(row == 0, t2, jnp.where(row == 1, t1, r2))
        tail_ref[...] = ci[tm - 8:, :]
        cs_ref[0] = ci[tm - (CONV_W - 1):, :]
    conv = wc_ref[0:1, :] * s2 + wc_ref[1:2, :] * s1 + wc_ref[2:3, :] * ci
    gc_ref[...] = (g_post * conv).astype(BF16)

    q_ref[...] = (proj(0) * SB_SCALE).astype(BF16)
    if whole_seqs:
        k = proj(1)
        v = proj(2)
        k_ref[...] = k
        v_ref[...] = v
        kb_ref[...] = k.astype(BF16)
        vb_ref[...] = v.astype(BF16)
    else:
        k = _dot_nt(wkv_ref[:SB_WIDTH, :], h)
        v = _dot_nt(wkv_ref[SB_WIDTH:, :], h)
        k_ref[0] = k
        v_ref[0] = v
        kb_ref[0] = k.astype(BF16)
        vb_ref[0] = v.astype(BF16)


def _proj_even(x, g, w_in, w_kv_t, w_conv, conv_prev, *, batch, seq_len, tm):
    n = x.shape[0]
    whole = tm % seq_len == 0
    grid = (n // tm,)
    row_spec = lambda w: pl.BlockSpec((tm, w), lambda i: (i, 0))
    in_specs = [row_spec(D_MODEL), _const_spec((1, D_MODEL)), _const_spec(w_in.shape), _const_spec(w_conv.shape)]
    args = [x, g, w_in, w_conv]
    if whole:
        kv_shape, kv_spec = (n, SB_WIDTH), row_spec(SB_WIDTH)
    else:
        seq_tiles = seq_len // tm
        kv_shape = (batch, SB_WIDTH, seq_len)
        kv_spec = pl.BlockSpec((1, SB_WIDTH, tm), lambda i: (i // seq_tiles, 0, i % seq_tiles))
    out_shape = [jax.ShapeDtypeStruct((n, SB_WIDTH), BF16), jax.ShapeDtypeStruct(kv_shape, F32),
                 jax.ShapeDtypeStruct(kv_shape, F32), jax.ShapeDtypeStruct(kv_shape, BF16),
                 jax.ShapeDtypeStruct(kv_shape, BF16), jax.ShapeDtypeStruct((n, CONV_DIM), BF16)]
    out_specs = [row_spec(SB_WIDTH)] + [kv_spec] * 4 + [row_spec(CONV_DIM)]
    scratch = []
    if whole:
        if conv_prev is None:
            conv_prev = jnp.zeros((batch, CONV_W - 1, CONV_DIM), F32)
        e1 =jnp.zeros((batch, seq_len, CONV_DIM), F32).at[:, 0].set(conv_prev[:, 1])
        e2 = jnp.zeros((batch, seq_len, CONV_DIM), F32).at[:, 0].set(conv_prev[:, 0]).at[:, 1].set(conv_prev[:, 1])
        args += [e1.reshape(n, CONV_DIM), e2.reshape(n, CONV_DIM)]
        in_specs += [row_spec(CONV_DIM), row_spec(CONV_DIM)]
        out_shape.append(jax.ShapeDtypeStruct((n, CONV_DIM), F32))
        out_specs.append(row_spec(CONV_DIM))
    else:
        assert seq_len % tm == 0 and conv_prev is None
        args.append(w_kv_t)
        in_specs.append(_const_spec(w_kv_t.shape))
        out_shape.append(jax.ShapeDtypeStruct((batch, CONV_W - 1, CONV_DIM), F32))
        out_specs.append(pl.BlockSpec((1, CONV_W - 1, CONV_DIM), lambda i: (i // seq_tiles, 0, 0)))
        scratch.append(pltpu.VMEM((8, CONV_DIM), F32))
    outs = pl.pallas_call(
        functools.partial(_proj_even_body, tm=tm, seq_len=seq_len, whole_seqs=whole),
        grid=grid, in_specs=in_specs, out_specs=out_specs, out_shape=out_shape, scratch_shapes=scratch,
        compiler_params=_cparams(("arbitrary",)), name="proj_even")(*args)
    q, k, v, kb, vb, gc, last = outs
    if whole:
        last = last.reshape(batch, seq_len, CONV_DIM)[:, seq_len - (CONV_W - 1):]
    return q, k, v, kb, vb, gc, last


def _suffix_rhs(tk):
    j = np.arange(2 * tk)[:, None] % tk
    c = np.arange(2 * tk)[None, :]
    return jnp.asarray(np.where(c < tk, j > c, True), dtype=BF16)


def _sb_prompt_body(q_ref, k_ref, v_ref, r_ref, o_ref, carry_ref, acc_ref, *, tq, tk):
    qi = pl.program_id(1)
    ratio = tq // tk
    even = lax.broadcasted_iota(jnp.int32, (LANES, tk), 0) < SB_HEAD_DIM
    rhs = r_ref[...]
    carry_ref[...] = jnp.zeros_like(carry_ref)
    acc_ref[...] = jnp.zeros_like(acc_ref)
    qs = [q_ref[:, p * LANES:(p + 1) * LANES] for p in range(HEAD_PAIRS)]

    def block_diag(blk):
        zero = jnp.zeros_like(blk)
        return jnp.concatenate([jnp.where(even, blk, zero), jnp.where(even, zero, blk)], axis=1)

    def tile(kj, masked, r0=0):
        start = pl.multiple_of(kj * tk, tk)
        if masked:
            q_pos = qi * tq + r0 + lax.broadcasted_iota(jnp.int32, (tq - r0, tk), 0)
            k_pos = kj * tk + lax.broadcasted_iota(jnp.int32, (tq - r0, tk), 1)
            visible = k_pos < q_pos
        zs = [_dot(qs[p][r0:], block_diag(k_ref[0, p * LANES:(p + 1) * LANES, pl.ds(start, tk)]))
              for p in range(HEAD_PAIRS)]
        log_betas, splits = [], []
        for hd in range(SB_HEADS):
            z = zs[hd // 2][:, (hd % 2) * tk:(hd % 2 + 1) * tk]
            log_beta = _log_sigmoid(z)
            log_1m = log_beta - z
            if masked:
                log_1m = jnp.where(visible, log_1m, 0.0)
            hi, lo = _split_hi_lo(log_1m)
            log_betas.append(log_beta)
            splits.append(jnp.concatenate([hi, lo], axis=1))
        sums = [_dot(sp, rhs) for sp in splits]
        ws = []
        for hd in range(SB_HEADS):
            carry = carry_ref[hd, r0:, :]
            w = jnp.exp(log_betas[hd] + sums[hd][:, :tk] + carry)
            if masked:
                w = jnp.where(visible, w, 0.0)
            carry_ref[hd, r0:, :] = carry + sums[hd][:, tk:]
            ws.append(w.astype(BF16))
        for p in range(HEAD_PAIRS):
            vbd = block_diag(v_ref[0, p * LANES:(p + 1) * LANES, pl.ds(start, tk)])
            acc_ref[p, r0:, :] += _dot_nt(jnp.concatenate([ws[2 * p], ws[2 * p + 1]], axis=1), vbd)

    for t in reversed(range(ratio)):
        tile(qi * ratio + t, True, r0=t * tk)

    def body(it, c):
        tile(qi * ratio - 1 - 2 * it, False)
        tile(qi * ratio - 2 - 2 * it, False)
        return c

    lax.fori_loop(0, qi * (ratio // 2), body, 0)
    for p in range(HEAD_PAIRS):
        o_ref[:, p * LANES:(p + 1) * LANES] = acc_ref[p].astype(BF16)


def _sb_prompt(q, kb, vb, *, batch, seq_len, tq, tk):
    assert (tq // tk) % 2 == 0 and tq % tk == 0 and seq_len % tq == 0
    n = q.shape[0]
    nq = seq_len // tq
    rhs = _suffix_rhs(tk)
    q_spec = pl.BlockSpec((tq, SB_WIDTH), lambda b, i: (b * nq + i, 0))
    kv_spec = pl.BlockSpec((1, SB_WIDTH, seq_len), lambda b, i: (b, 0, 0))
    return pl.pallas_call(
        functools.partial(_sb_prompt_body, tq=tq, tk=tk),
        grid=(batch, nq),
        in_specs=[q_spec, kv_spec, kv_spec, _const_spec(rhs.shape)],
        out_specs=q_spec,
        out_shape=jax.ShapeDtypeStruct((n, SB_WIDTH), BF16),
        scratch_shapes=[pltpu.VMEM((SB_HEADS, tq, LANES), F32), pltpu.VMEM((HEAD_PAIRS, tq, LANES), F32)],
        compiler_params=_cparams(("arbitrary", "arbitrary")), name="sb_attn_prompt")(q, kb, vb, rhs)


def _sb_sample_body(q_ref, kn_ref, vn_ref, kc_ref, vc_ref, l_ref, o_ref, qh_ref, carry_ref, acc_ref,
                    *, dec, kblk, tk):
    j = pl.program_id(1)
    rows = SB_HEADS * dec
    rhs = l_ref[...]

    def tiles(kts, vts, visible):
        log_betas, sums = [], []
        for kt in kts:
            z = _dot(qh_ref[...], kt)
            log_beta = _log_sigmoid(z)
            log_1m = log_beta - z
            if visible is not None:
                log_1m = jnp.where(visible, log_1m, 0.0)
            hi, lo = _split_hi_lo(log_1m)
            log_betas.append(log_beta)
            sums.append(_dot(jnp.concatenate([hi, lo], axis=1), rhs))
        carry = carry_ref[...]
        acc = acc_ref[...]
        for log_beta, sm, vt in zip(log_betas, sums, vts):
            w = jnp.exp(log_beta + sm[:, :tk] + carry)
            if visible is not None:
                w = jnp.where(visible, w, 0.0)
            acc = acc + _dot_nt(w.astype(BF16), vt)
            carry = carry + sm[:, tk:]
        carry_ref[...] = carry
        acc_ref[...] = acc

    def new_keys_t(x):
        x = jnp.concatenate([x.astype(F32), jnp.zeros((tk - dec, SB_WIDTH), F32)], axis=0)
        return x.T.astype(BF16)

    @pl.when(j == 0)
    def _():
        q = q_ref[...]
        qt = jnp.concatenate([q] * SB_HEADS, axis=0)
        rr = lax.broadcasted_iota(jnp.int32, (rows, SB_WIDTH), 0)
        cc = lax.broadcasted_iota(jnp.int32, (rows, SB_WIDTH), 1)
        qh_ref[...] = jnp.where(rr // dec == cc // SB_HEAD_DIM, qt, jnp.zeros_like(qt))
        carry_ref[...] = jnp.zeros_like(carry_ref)
        acc_ref[...] = jnp.zeros_like(acc_ref)
        query_t = lax.broadcasted_iota(jnp.int32, (rows, tk), 0) % dec
        key_i = lax.broadcasted_iota(jnp.int32, (rows, tk), 1)
        tiles([new_keys_t(kn_ref[...])], [new_keys_t(vn_ref[...])],
              key_i < query_t)

    order = list(reversed(range(kblk // tk)))
    tiles([kc_ref[0, :, s * tk:(s + 1) * tk].astype(BF16) for s in order],
          [vc_ref[0, :, s * tk:(s + 1) * tk].astype(BF16) for s in order], None)

    @pl.when(j == pl.num_programs(1) - 1)
    def _():
        cc = lax.broadcasted_iota(jnp.int32, (dec, SB_WIDTH), 1)
        out = jnp.zeros((dec, SB_WIDTH), F32)
        for h in range(SB_HEADS):
            out = out + jnp.where(cc // SB_HEAD_DIM == h, acc_ref[h * dec:(h + 1) * dec, :], 0.0)
        o_ref[...] = out.astype(BF16)


def _sb_sample(q, kb, vb, cache_kt, cache_vt, *, batch, dec, kblk=1024, tk=256):
    n = q.shape[0]
    past = cache_kt.shape[2]
    nkb = past // kblk
    rows = SB_HEADS * dec
    rhs = _suffix_rhs(tk)
    new_spec = pl.BlockSpec((dec, SB_WIDTH), lambda b, j: (b, 0))
    cache_spec = pl.BlockSpec((1, SB_WIDTH, kblk), lambda b, j: (b, 0, nkb - 1 - j))
    return pl.pallas_call(
        functools.partial(_sb_sample_body, dec=dec, kblk=kblk, tk=tk),
        grid=(batch, nkb),
        in_specs=[new_spec, new_spec, new_spec, cache_spec, cache_spec, _const_spec(rhs.shape)],
        out_specs=new_spec,
        out_shape=jax.ShapeDtypeStruct((n, SB_WIDTH), BF16),
        scratch_shapes=[pltpu.VMEM((rows, SB_WIDTH), BF16), pltpu.VMEM((rows, tk), F32),
                        pltpu.VMEM((rows, SB_WIDTH), F32)],
        compiler_params=_cparams(("arbitrary", "arbitrary")), name="sb_attn_sample")(
            q, kb, vb, cache_kt, cache_vt, rhs)


def _proj_odd_body(*refs, tm, gate_len, emit_vn, kpe_transposed):
    (x_ref, g_ref, w_ref, lng_ref, lnb_ref, ws_ref, bs_ref, qg_ref, kvg_ref, wqn_ref, wqp_ref,
     cos_ref, sin_ref) = refs[:13]
    outs = refs[13:]
    if emit_vn:
        sgu_ref, vn_ref, qn_ref, qp_ref, ckv_ref, ckvb_ref, kpe_ref, kpeb_ref = outs
    else:
        sgu_ref, qn_ref, qp_ref, ckv_ref, ckvb_ref, kpe_ref, kpeb_ref = outs
    h = _rms(x_ref[...], g_ref[...]).astype(BF16)
    o_v, o_q, o_kv, o_pe = SGU_DIM, 2 * SGU_DIM, 2 * SGU_DIM + Q_LORA, 2 * SGU_DIM + Q_LORA + KV_LORA
    cos = cos_ref[...]
    sin = sin_ref[...]
    kpe = _rope_block(_dot(h, w_ref[:, o_pe:]), cos, sin)
    if kpe_transposed:
        kpe_ref[0] = kpe.T[:ROPE_DIM, :]
    else:
        kpe_ref[...] = kpe[:, :ROPE_DIM]
    kpeb_ref[...] = kpe.astype(BF16)
    ckv = _rms(_dot(h, w_ref[:, o_kv:o_pe]), kvg_ref[...])
    ckv_ref[...] = ckv
    ckvb_ref[...] = ckv.astype(BF16)
    cq = _rms(_dot(h, w_ref[:, o_q:o_kv]), qg_ref[...]).astype(BF16)
    for blk in range(MLA_HEADS * ROPE_DIM // LANES):
        bs = slice(blk * LANES, (blk + 1) * LANES)
        qp_ref[:, bs] = _rope_block(_dot(cq, wqp_ref[:, bs]), cos, sin).astype(BF16)
    qn_ref[...] = _dot(cq, wqn_ref[...]).astype(BF16)
    u = _dot(h, w_ref[:, :o_v])
    v = _dot(h, w_ref[:, o_v:o_q])
    mu = jnp.mean(v, axis=-1, keepdims=True)
    vc = v - mu
    var = jnp.mean(vc * vc, axis=-1, keepdims=True)
    vn = vc * lax.rsqrt(var + EPS) * lng_ref[...] + lnb_ref[...]
    if emit_vn:
        vn_ref[...] = vn
    vnb = vn.astype(BF16)
    rr = lax.broadcasted_iota(jnp.int32, (SGU_CHUNK, SGU_CHUNK), 0)
    cc = lax.broadcasted_iota(jnp.int32, (SGU_CHUNK, SGU_CHUNK), 1)
    causal = (rr // gate_len == cc // gate_len) & (cc <= rr)
    for g in range(SGU_GROUPS):
        gs = slice(g * SGU_GROUP_DIM, (g + 1) * SGU_GROUP_DIM)
        wg = jnp.where(causal, ws_ref[g], 0.0).astype(BF16)
        for c in range(tm // SGU_CHUNK):
            rs = slice(c * SGU_CHUNK, (c + 1) * SGU_CHUNK)
            s = _dot(wg, vnb[rs, gs]) + bs_ref[:, gs]
            sgu_ref[rs, gs] = (u[rs, gs] * s).astype(BF16)


def _proj_odd(x, g, w_in, ln_g, ln_b, w_s, b_s, qg, kvg, wqn, wqp, cos, sin, *, batch, seq_len, tm, gate_len,
              emit_vn):
    n = x.shape[0]
    pos_tiles = cos.shape[0] // tm
    kpe_transposed = tm % seq_len != 0
    if kpe_transposed:
        seq_tiles = seq_len // tm
        kpe_shape = (batch, ROPE_DIM, seq_len)
        kpe_spec = pl.BlockSpec((1, ROPE_DIM, tm), lambda i: (i // seq_tiles, 0, i % seq_tiles))
    else:
        kpe_shape, kpe_spec = (n, ROPE_DIM), pl.BlockSpec((tm, ROPE_DIM), lambda i: (i, 0))
    row_spec = lambda w: pl.BlockSpec((tm, w), lambda i: (i, 0))
    tab_spec = pl.BlockSpec((tm, LANES), lambda i: (i % pos_tiles, 0))
    consts = [g, w_in, ln_g, ln_b, w_s, b_s, qg, kvg, wqn, wqp]
    out_shape = [jax.ShapeDtypeStruct((n, SGU_DIM), BF16)]
    out_specs = [row_spec(SGU_DIM)]
    if emit_vn:
        out_shape.append(jax.ShapeDtypeStruct((n, SGU_DIM), F32))
        out_specs.append(row_spec(SGU_DIM))
    out_shape += [jax.ShapeDtypeStruct((n, MLA_HEADS * NOPE_DIM), BF16),
                  jax.ShapeDtypeStruct((n, MLA_HEADS * ROPE_DIM), BF16),
                  jax.ShapeDtypeStruct((n, KV_LORA), F32), jax.ShapeDtypeStruct((n, KV_LORA), BF16),
                  jax.ShapeDtypeStruct(kpe_shape, F32), jax.ShapeDtypeStruct((n, LANES), BF16)]
    out_specs += [row_spec(MLA_HEADS * NOPE_DIM), row_spec(MLA_HEADS * ROPE_DIM),
                  row_spec(KV_LORA), row_spec(KV_LORA), kpe_spec, row_spec(LANES)]
    return pl.pallas_call(
        functools.partial(_proj_odd_body, tm=tm, gate_len=gate_len, emit_vn=emit_vn, kpe_transposed=kpe_transposed),
        grid=(n // tm,),
        in_specs=[row_spec(D_MODEL)] + [_const_spec(c.shape) for c in consts] + [tab_spec, tab_spec],
        out_specs=out_specs, out_shape=out_shape,
        compiler_params=_cparams(("arbitrary",)), name="proj_odd")(x, *consts, cos, sin)


def _mla_queries(qn, qpe, wuk_ref, qcat_ref, tq):
    lane = lax.broadcasted_iota(jnp.int32, (tq, LANES), 1)
    low = lane < NOPE_DIM
    qpe = qpe.astype(F32)
    qcat_ref[:, KV_LORA:] = jnp.zeros((MLA_HEADS * tq, MLA_QK - KV_LORA), BF16)
    for hd in range(MLA_HEADS):
        p = hd // 2
        rs = slice(hd * tq, (hd + 1) * tq)
        pair = qn[:, p * LANES:(p + 1) * LANES]
        qm = jnp.where(low if hd % 2 == 0 else jnp.logical_not(low), pair, jnp.zeros_like(pair))
        qcat_ref[rs, :KV_LORA] = _dot(qm, wuk_ref[p]).astype(BF16)
        qcat_ref[rs, KV_LORA:KV_LORA + ROPE_DIM] = qpe[:, hd * ROPE_DIM:(hd + 1) * ROPE_DIM].astype(BF16)


def _lane_tile(x, width):
    return jnp.concatenate([x] * (width // LANES), axis=1)


def _mla_scores(qcat_ref, kcat, sc):
    sc[0][...] = _dot_nt(qcat_ref[...], kcat)


def _mla_softmax(sc, visible_fn, first):
    s_ref, p_ref, m_ref, l_ref, _, a_ref = sc
    rows, tk = s_ref.shape
    for c in range(rows // MLA_ROW_CHUNK):
        rs = slice(c * MLA_ROW_CHUNK, (c + 1) * MLA_ROW_CHUNK)
        s = s_ref[rs, :] * (MLA_SCALE * LOG2_E)
        if visible_fn is not None:
            s = jnp.where(visible_fn(c), s, NEG_INF)
        row_max = jnp.broadcast_to(jnp.max(s, axis=-1, keepdims=True), (MLA_ROW_CHUNK, LANES))
        if first:
            m_new = row_max
        else:
            m_prev = m_ref[rs, :]
            m_new = jnp.maximum(m_prev, row_max)
            alpha = jnp.exp2(m_prev - m_new)
            a_ref[rs, :] = alpha
        p = jnp.exp2(s - _lane_tile(m_new, tk))
        row_sum = jnp.sum(p, axis=-1, keepdims=True)
        l_ref[rs, :] = jnp.broadcast_to(row_sum, (MLA_ROW_CHUNK, LANES)) if first else alpha * l_ref[rs, :] + row_sum
        m_ref[rs, :] = m_new
        p_ref[rs, :] = p.astype(BF16)


def _mla_pv(ck, sc, first):
    _, p_ref, _, _, acc_ref, a_ref = sc
    pv = _dot(p_ref[...], ck)
    acc_ref[...] = pv if first else acc_ref[...] * _lane_tile(a_ref[...], KV_LORA) + pv


def _mla_tile(qcat_ref, kcat, sc, visible_fn, first=False):
    _mla_scores(qcat_ref, kcat, sc)
    _mla_softmax(sc, visible_fn, first)
    _mla_pv(kcat[:, :KV_LORA], sc, first)


def _mla_finish(sc, wuv_ref, o_ref, tq):
    l_ref, acc_ref = sc[3], sc[4]
    o_lat = (acc_ref[...] / _lane_tile(l_ref[...], KV_LORA)).astype(BF16)
    for p in range(MLA_HEADS // 2):
        h0, h1 = 2 * p, 2 * p + 1
        o_ref[:, p * LANES:(p + 1) * LANES] = (
            _dot(o_lat[h0 * tq:(h0 + 1) * tq], wuv_ref[h0]) + _dot(o_lat[h1 * tq:(h1 + 1) * tq], wuv_ref[h1])
        ).astype(BF16)


def _mla_prompt_body(qn_ref, qp_ref, ckv_ref, kpe_ref, wuk_ref, wuv_ref, o_ref, qcat_ref, *sc, tq, tk):
    qi = pl.program_id(1)
    _mla_queries(qn_ref[...], qp_ref[...], wuk_ref, qcat_ref, tq)

    n_full = (qi * tq) // tk
    col = lax.broadcasted_iota(jnp.int32, (MLA_ROW_CHUNK, tk), 1)

    def visible(c):
        q_chunk_end = qi * tq + ((c * MLA_ROW_CHUNK) % tq) // CHUNK * CHUNK + CHUNK
        return col < q_chunk_end - n_full * tk

    def keys(kj):
        start = pl.multiple_of(kj * tk, tk)
        return jnp.concatenate([ckv_ref[pl.ds(start, tk), :], kpe_ref[pl.ds(start, tk), :],
                                jnp.zeros((tk, MLA_QK - KV_LORA - LANES), BF16)], axis=1)

    _mla_tile(qcat_ref, keys(n_full), sc, visible, first=True)

    def body(kj, c):
        _mla_tile(qcat_ref, keys(kj), sc, None)
        return c

    lax.fori_loop(0, n_full, body, 0)
    _mla_finish(sc, wuv_ref, o_ref, tq)


def _mla_scratch(rows, tk):
    return [pltpu.VMEM((rows, MLA_QK), BF16), pltpu.VMEM((rows, tk), F32), pltpu.VMEM((rows, tk), BF16),
            pltpu.VMEM((rows, LANES), F32), pltpu.VMEM((rows, LANES), F32), pltpu.VMEM((rows, KV_LORA), F32),
            pltpu.VMEM((rows, LANES), F32)]


def _mla_prompt(qn, qp, ckvb, kpeb, wuk, wuv, *, batch, seq_len, tq, tk):
    assert tq % CHUNK == 0 and CHUNK % MLA_ROW_CHUNK == 0 and seq_len % tq == 0 and seq_len % tk == 0
    n = qn.shape[0]
    nq = seq_len // tq
    width = MLA_HEADS * V_DIM
    return pl.pallas_call(
        functools.partial(_mla_prompt_body, tq=tq, tk=tk),
        grid=(batch, nq),
        in_specs=[pl.BlockSpec((tq, MLA_HEADS * NOPE_DIM), lambda b, i: (b * nq + i, 0)),
                  pl.BlockSpec((tq, MLA_HEADS * ROPE_DIM), lambda b, i: (b * nq + i, 0)),
                  pl.BlockSpec((seq_len, KV_LORA), lambda b, i: (b, 0)),
                  pl.BlockSpec((seq_len, LANES), lambda b, i: (b, 0)),
                  _const_spec(wuk.shape), _const_spec(wuv.shape)],
        out_specs=pl.BlockSpec((tq, width), lambda b, i: (b * nq + i, 0)),
        out_shape=jax.ShapeDtypeStruct((n, width), BF16),
        scratch_shapes=_mla_scratch(MLA_HEADS * tq, tk),
        compiler_params=_cparams(("arbitrary", "arbitrary")), name="mla_attn_prompt")(qn, qp, ckvb, kpeb, wuk, wuv)


def _mla_sample_body(qn_ref, qp_ref, cn_ref, pn_ref, cc_ref, pc_ref, wuk_ref, wuv_ref, o_ref, kcat_ref, qcat_ref,
                     *sc, dec, past, kblk, tk):
    j = pl.program_id(1)
    rows = MLA_HEADS * dec

    @pl.when(j == 0)
    def _():
        _mla_queries(qn_ref[...], qp_ref[...], wuk_ref, qcat_ref, dec)
        kcat_ref[:, KV_LORA:] = jnp.zeros((tk, MLA_QK - KV_LORA), BF16)
        new = jnp.concatenate([cn_ref[...], pn_ref[...], jnp.zeros((dec, MLA_QK - KV_LORA - LANES), BF16)], axis=1)
        kcat = jnp.concatenate([new, jnp.zeros((tk - dec, MLA_QK), BF16)], axis=0)

        def visible(c):
            col = lax.broadcasted_iota(jnp.int32, (MLA_ROW_CHUNK, tk), 1)
            row = c * MLA_ROW_CHUNK + lax.broadcasted_iota(jnp.int32, (MLA_ROW_CHUNK, tk), 0)
            return (col < dec) & ((past + col) // CHUNK <= (past + row % dec) // CHUNK)

        _mla_tile(qcat_ref, kcat, sc, visible, first=True)

    def body(it, c):
        start = pl.multiple_of(it * tk, tk)
        kcat_ref[:, :KV_LORA] = cc_ref[0, pl.ds(start, tk), :].astype(BF16)
        kp_t = jnp.concatenate([pc_ref[0, :, pl.ds(start, tk)], jnp.zeros((LANES - ROPE_DIM, tk), F32)], axis=0)
        kcat_ref[:, KV_LORA:KV_LORA + LANES] = kp_t.T.astype(BF16)
        _mla_tile(qcat_ref, kcat_ref[...], sc, None)
        return c

    lax.fori_loop(0, kblk // tk, body, 0)

    @pl.when(j == pl.num_programs(1) - 1)
    def _():
        _mla_finish(sc, wuv_ref, o_ref, dec)


def _mla_sample(qn, qp, ckvb, kpeb, cache_ckv, cache_kpe_t, wuk, wuv, *, batch, dec, kblk=1024, tk=1024):
    n = qn.shape[0]
    past = cache_ckv.shape[1]
    width = MLA_HEADS * V_DIM
    return pl.pallas_call(
        functools.partial(_mla_sample_body, dec=dec, past=past, kblk=kblk, tk=tk),
        grid=(batch, past // kblk),
        in_specs=[pl.BlockSpec((dec, MLA_HEADS * NOPE_DIM), lambda b, j: (b, 0)),
                  pl.BlockSpec((dec, MLA_HEADS * ROPE_DIM), lambda b, j: (b, 0)),
                  pl.BlockSpec((dec, KV_LORA), lambda b, j: (b, 0)),
                  pl.BlockSpec((dec, LANES), lambda b, j: (b, 0)),
                  pl.BlockSpec((1, kblk, KV_LORA), lambda b, j: (b, j, 0)),
                  pl.BlockSpec((1, ROPE_DIM, kblk), lambda b, j: (b, 0, j)),
                  _const_spec(wuk.shape), _const_spec(wuv.shape)],
        out_specs=pl.BlockSpec((dec, width), lambda b, j: (b, 0)),
        out_shape=jax.ShapeDtypeStruct((n, width), BF16),
        scratch_shapes=[pltpu.VMEM((tk, MLA_QK), BF16)] + _mla_scratch(MLA_HEADS * dec, tk),
        compiler_params=_cparams(("arbitrary", "arbitrary")), name="mla_attn_sample")(
            qn, qp, ckvb, kpeb, cache_ckv, cache_kpe_t, wuk, wuv)


def _out_ffn_body(a_ref, b_ref, x_ref, woa_ref, wob_ref, gpost_ref, gpre_ref, wup_ref, wdn_ref, gfpost_ref, o_ref):
    tm = x_ref.shape[0]
    groups = [slice(r * tm // FFN_ROW_GROUPS, (r + 1) * tm // FFN_ROW_GROUPS) for r in range(FFN_ROW_GROUPS)]
    mixed = [_dot(a_ref[rs, :], woa_ref[...]) + _dot(b_ref[rs, :], wob_ref[...]) for rs in groups]
    x1 = [x_ref[rs, :] + _rms(m, gpost_ref[...]) for rs, m in zip(groups, mixed)]
    h = [_rms(x, gpre_ref[...]).astype(BF16) for x in x1]
    down = [jnp.zeros_like(x) for x in x1]
    for c in range(D_FF // FF_CHUNK):
        for r in range(FFN_ROW_GROUPS):
            up = _dot(h[r], wup_ref[:, c * FF_CHUNK:(c + 1) * FF_CHUNK])
            act = jnp.square(jnp.maximum(up, 0.0)).astype(BF16)
            down[r] = down[r] + _dot(act, wdn_ref[c * FF_CHUNK:(c + 1) * FF_CHUNK, :])
    for r, rs in enumerate(groups):
        o_ref[rs, :] = x1[r] + _rms(down[r], gfpost_ref[...])


def _out_ffn(a, b, x, woa, wob, g_post, g_pre, w_up, w_down, g_fpost, *, tm):
    n = x.shape[0]
    row_spec = lambda w: pl.BlockSpec((tm, w), lambda i: (i, 0))
    consts = [woa, wob, g_post, g_pre, w_up, w_down, g_fpost]
    return pl.pallas_call(
        _out_ffn_body,
        grid=(n // tm,),
        in_specs=[row_spec(a.shape[1]), row_spec(b.shape[1]), row_spec(D_MODEL)] + [_const_spec(c.shape) for c in consts],
        out_specs=row_spec(D_MODEL),
        out_shape=jax.ShapeDtypeStruct((n, D_MODEL), F32),
        compiler_params=_cparams(("arbitrary",)), name="out_ffn")(a, b, x, *consts)


def _rope_tables(pos, reps):
    half = ROPE_DIM // 2
    inv = ROPE_THETA ** (-jnp.arange(half, dtype=F32) / half)
    ang = pos.astype(F32)[:, None] * inv[None, :]
    cos = jnp.tile(jnp.concatenate([jnp.cos(ang), jnp.cos(ang)], axis=1), (reps, LANES // ROPE_DIM))
    sin = jnp.tile(jnp.concatenate([-jnp.sin(ang), jnp.sin(ang)], axis=1), (reps, LANES // ROPE_DIM))
    return cos, sin


def _prep_even(p, j):
    w_in = p["even_w_in"][j]
    return dict(w_in=w_in.astype(BF16), w_kv_t=w_in[:, SB_WIDTH:3 * SB_WIDTH].T.astype(BF16), w_conv=p["even_w_conv"][j],
                woa=p["even_w_out"][j, :SB_WIDTH].astype(BF16), wob=p["even_w_out"][j, SB_WIDTH:].astype(BF16))


def _prep_odd(p, j, gate_len):
    w_in = p["odd_w_in"][j]
    w_in = jnp.pad(w_in, ((0, 0), (0, ODD_IN_PAD - w_in.shape[1]))).astype(BF16)
    reps = SGU_CHUNK // gate_len
    w_s = jnp.tile(p["sgu_w_s"][j, :, :gate_len, :gate_len], (1, reps, reps))
    b_s = jnp.tile(p["sgu_b_s"][j, :, :gate_len], (1, reps))
    b_s = jnp.repeat(b_s.T, SGU_GROUP_DIM, axis=1)
    w_uq = p["mla_w_uq"][j].reshape(Q_LORA, MLA_HEADS, NOPE_DIM + ROPE_DIM)
    wqn = w_uq[:, :, :NOPE_DIM].reshape(Q_LORA, MLA_HEADS * NOPE_DIM).astype(BF16)
    wqp = w_uq[:, :, NOPE_DIM:].reshape(Q_LORA, MLA_HEADS * ROPE_DIM).astype(BF16)
    wuk = p["mla_w_uk"][j].reshape(MLA_HEADS // 2, 2 * NOPE_DIM, KV_LORA).astype(BF16)
    w_uv = p["mla_w_uv"][j]
    wuv = jnp.stack([jnp.pad(w_uv[h], ((0, 0), ((h % 2) * V_DIM, (1 - h % 2) * V_DIM))) for h in range(MLA_HEADS)])
    return dict(w_in=w_in, ln_g=p["sgu_ln_g"][j][None], ln_b=p["sgu_ln_b"][j][None], w_s=w_s, b_s=b_s,
                qg=p["mla_q_norm_g"][j][None], kvg=p["mla_kv_norm_g"][j][None], wqn=wqn, wqp=wqp, wuk=wuk,
                wuv=wuv.astype(BF16),
                woa=p["odd_w_out"][j, :SGU_DIM].astype(BF16), wob=p["odd_w_out"][j, SGU_DIM:].astype(BF16))


def _run_trunk(x, pos, past, p, *, batch, seq_len):
    depth = p["mix_pre_g"].shape[0]
    n = batch * seq_len
    x = x.reshape(n, D_MODEL)
    is_sample = past is not None
    tm = min(ROW_TILE, n)
    assert n % tm == 0 and (tm % seq_len == 0 or seq_len % tm == 0) and tm % SGU_CHUNK == 0
    gate_len = min(seq_len, SGU_CHUNK)
    tm_proj = PROJ_ROW_TILE if (seq_len % PROJ_ROW_TILE == 0 and seq_len > PROJ_ROW_TILE) else tm
    cos, sin = _rope_tables(pos, max(1, tm_proj // seq_len))
    st = {k: [] for k in ("sb_k", "sb_v", "conv", "ckv", "kpe", "sgu_v")}
    for layer in range(depth):
        j = layer // 2
        g_pre = p["mix_pre_g"][layer][None]
        if layer % 2 == 0:
            w = _prep_even(p, j)
            conv_prev = past["conv"][j] if is_sample else None
            q, k, v, kb, vb, b_mix, conv_state = _proj_even(x, g_pre, w["w_in"], w["w_kv_t"], w["w_conv"], conv_prev,
                                                            batch=batch, seq_len=seq_len, tm=tm_proj)
            if is_sample:
                assert tm % seq_len == 0

                def cache_t(c):
                    return jnp.transpose(c, (0, 2, 3, 1)).reshape(batch, SB_WIDTH, -1)

                a_mix = _sb_sample(q, kb, vb, cache_t(past["sb_k"][j]), cache_t(past["sb_v"][j]),
                                   batch=batch, dec=seq_len)
                k, v = (t.reshape(batch, seq_len, SB_HEADS, SB_HEAD_DIM) for t in (k, v))
            else:
                assert seq_len % tm == 0
                a_mix = _sb_prompt(q, kb, vb, batch=batch, seq_len=seq_len, tq=SB_TQ, tk=SB_TK)
                k, v = (jnp.transpose(t.reshape(batch, SB_HEADS, SB_HEAD_DIM, seq_len), (0, 3, 1, 2)) for t in (k, v))
            st["sb_k"].append(k)
            st["sb_v"].append(v)
            st["conv"].append(conv_state)
        else:
            w = _prep_odd(p, j, gate_len)
            outs = _proj_odd(x, g_pre, w["w_in"], w["ln_g"], w["ln_b"], w["w_s"], w["b_s"], w["qg"], w["kvg"],
                             w["wqn"], w["wqp"], cos, sin, batch=batch, seq_len=seq_len, tm=tm_proj, gate_len=gate_len,
                             emit_vn=is_sample)
            if is_sample:
                a_mix, vn, qn, qp, ckv, ckvb, kpe, kpeb = outs
                st["sgu_v"].append(vn.reshape(batch, seq_len, SGU_DIM))
                b_mix = _mla_sample(qn, qp, ckvb, kpeb, past["ckv"][j], jnp.transpose(past["kpe"][j], (0, 2, 1)),
                                    w["wuk"], w["wuv"],
                                    batch=batch, dec=seq_len)
            else:
                a_mix, qn, qp, ckv, ckvb, kpe, kpeb = outs
                b_mix = _mla_prompt(qn, qp, ckvb, kpeb, w["wuk"], w["wuv"], batch=batch, seq_len=seq_len,
                                    tq=MLA_TQ, tk=MLA_TK)
            st["ckv"].append(ckv.reshape(batch, seq_len, KV_LORA))
            st["kpe"].append(jnp.transpose(kpe, (0, 2, 1)) if kpe.ndim == 3 else kpe.reshape(batch, seq_len, ROPE_DIM))
        x = _out_ffn(a_mix, b_mix, x, w["woa"], w["wob"], p["mix_post_g"][layer][None], p["ffn_pre_g"][layer][None],
                     p["ffn_w_up"][layer].astype(BF16), p["ffn_w_down"][layer].astype(BF16),
                     p["ffn_post_g"][layer][None], tm=tm)
    states = {k: jnp.stack(v) for k, v in st.items() if v}
    return x.reshape(batch, seq_len, D_MODEL), states


def kernel(x_prompt, x_sample, cache_sb_k, cache_sb_v, state_conv, cache_mla_ckv, cache_mla_kpe,
           mix_pre_g, mix_post_g, ffn_pre_g, ffn_post_g, even_w_in, even_w_conv, even_w_out,
           odd_w_in, sgu_ln_g, sgu_ln_b, sgu_w_s, sgu_b_s, mla_q_norm_g, mla_kv_norm_g,
           mla_w_uq, mla_w_uk, mla_w_uv, odd_w_out, ffn_w_up, ffn_w_down):
    params = {
        "mix_pre_g": mix_pre_g, "mix_post_g": mix_post_g, "ffn_pre_g": ffn_pre_g, "ffn_post_g": ffn_post_g,
        "even_w_in": even_w_in, "even_w_conv": even_w_conv, "even_w_out": even_w_out,
        "odd_w_in": odd_w_in, "sgu_ln_g": sgu_ln_g, "sgu_ln_b": sgu_ln_b, "sgu_w_s": sgu_w_s,
        "sgu_b_s": sgu_b_s, "mla_q_norm_g": mla_q_norm_g, "mla_kv_norm_g": mla_kv_norm_g,
        "mla_w_uq": mla_w_uq, "mla_w_uk": mla_w_uk, "mla_w_uv": mla_w_uv, "odd_w_out": odd_w_out,
        "ffn_w_up": ffn_w_up, "ffn_w_down": ffn_w_down,
    }
    batch, seq_len, _ = x_prompt.shape
    pos_p = jnp.arange(seq_len, dtype=jnp.int32)
    y_prompt, st_p = _run_trunk(x_prompt, pos_p, None, params, batch=batch, seq_len=seq_len)
    dec_batch, dec_seq, _ = x_sample.shape
    past_len = cache_sb_k.shape[2]
    pos_s = past_len + jnp.arange(dec_seq, dtype=jnp.int32)
    past = {"sb_k": cache_sb_k, "sb_v": cache_sb_v, "conv": state_conv, "ckv": cache_mla_ckv, "kpe": cache_mla_kpe}
    y_sample, st_s = _run_trunk(x_sample, pos_s, past, params, batch=dec_batch, seq_len=dec_seq)
    return (y_prompt, y_sample,
            st_p["sb_k"], st_p["sb_v"], st_p["conv"], st_p["ckv"], st_p["kpe"],
            st_s["sb_k"], st_s["sb_v"], st_s["conv"], st_s["ckv"], st_s["kpe"], st_s["sgu_v"])
```

```python
import functools
import math

import numpy as np
import jax
import jax.numpy as jnp
from jax import lax
from jax.experimental import pallas as pl
from jax.experimental.pallas import tpu as pltpu

F32 = jnp.float32
BF16 = jnp.bfloat16

EPS = 1e-6
D_MODEL = 1024
CHUNK = 64
SB_HEADS = 8
SB_HEAD_DIM = 64
SB_WIDTH = SB_HEADS * SB_HEAD_DIM
SB_SCALE = 1.0 / math.sqrt(SB_HEAD_DIM)
CONV_DIM = D_MODEL // 2
CONV_W = 3
SGU_CHUNK = 128
SGU_GROUPS = 4
SGU_DIM = D_MODEL // 2
SGU_GROUP_DIM = SGU_DIM // SGU_GROUPS
MLA_HEADS = 8
Q_LORA = 384
KV_LORA = 256
NOPE_DIM = 64
ROPE_DIM = 32
V_DIM = 64
ROPE_THETA = 10000.0
MLA_SCALE = 1.0 / math.sqrt(NOPE_DIM + ROPE_DIM)
D_FF = 4 * D_MODEL
FF_CHUNK = 1024
FFN_ROW_GROUPS = 2

LANES = 128
HEAD_PAIRS = SB_HEADS // 2
MLA_QK = 2 * KV_LORA
ODD_IN_PAD =2 * SGU_DIM + Q_LORA + KV_LORA + LANES
VMEM_LIMIT = 56 * 1024 * 1024
NEG_INF = -1e30
LOG2_E = math.log2(math.e)

ROW_TILE = 512
PROJ_ROW_TILE = 1024
SB_TQ, SB_TK = 512, 128
MLA_TQ, MLA_TK = 256, 256
MLA_ROW_CHUNK = 64


def _cparams(sem):
    return pltpu.CompilerParams(dimension_semantics=sem, vmem_limit_bytes=VMEM_LIMIT)


def _const_spec(shape):
    nd = len(shape)
    return pl.BlockSpec(shape, lambda *_: (0,) * nd, pipeline_mode=pl.Buffered(1))


def _rms(x, g):
    return x * lax.rsqrt(jnp.mean(x * x, axis=-1, keepdims=True) + EPS) * g


def _dot(a, b):
    return jnp.dot(a, b, preferred_element_type=F32)


def _dot_nt(a, b):
    return lax.dot_general(a, b, (((1,), (1,)), ((), ())), preferred_element_type=F32)


def _log_sigmoid(z):
    neg_abs = lax.bitcast_convert_type(lax.bitcast_convert_type(z, jnp.uint32) | jnp.uint32(0x80000000), F32)
    return jnp.minimum(z, 0.0) - jnp.log(1.0 + jnp.exp(neg_abs))


def _split_hi_lo(x):
    hi = x.astype(BF16)
    lo = (x - hi.astype(F32)).astype(BF16)
    return hi, lo


def _rope_block(x, cos, sin):
    half = ROPE_DIM // 2
    lane = lax.broadcasted_iota(jnp.int32, x.shape, 1)
    partner = jnp.where(lane % ROPE_DIM < half, pltpu.roll(x, LANES - half, 1), pltpu.roll(x, half, 1))
    return x * cos + partner * sin


def _proj_even_body(*refs, tm, seq_len, whole_seqs):
    if whole_seqs:
        (x_ref, g_ref, w_ref, wc_ref, e1_ref, e2_ref,
         q_ref, k_ref, v_ref, kb_ref, vb_ref, gc_ref, ci_ref) = refs
    else:
        (x_ref, g_ref, w_ref, wc_ref, wkv_ref,
         q_ref, k_ref, v_ref, kb_ref, vb_ref, gc_ref, cs_ref, tail_ref) = refs
    if not whole_seqs:
        @pl.when((pl.program_id(0) % (seq_len // tm)) == 0)
        def _():
            tail_ref[...] = jnp.zeros_like(tail_ref)

    h = _rms(x_ref[...], g_ref[...]).astype(BF16)

    def proj(j):
        return _dot(h, w_ref[:, j * SB_WIDTH:(j + 1) * SB_WIDTH])

    g_post = proj(3)
    ci = proj(4) * proj(5)
    r1 = pltpu.roll(ci, 1, 0)
    r2 = pltpu.roll(ci, 2, 0)
    row = lax.broadcasted_iota(jnp.int32, (tm, 1), 0)
    if whole_seqs:
        tpos = row % seq_len
        s1 = jnp.where(tpos < 1, e1_ref[...], r1)
        s2 = jnp.where(tpos < 2, e2_ref[...], r2)
        ci_ref[...] = ci
    else:
        t1 = tail_ref[7:8, :]
        t2 = tail_ref[6:7, :]
        s1 = jnp.where(row == 0, t1, r1)
        s2 = jnp.where(row == 0, t2, jnp.where(row == 1, t1, r2))
        tail_ref[...] = ci[tm - 8:, :]
        cs_ref[0] = ci[tm - (CONV_W - 1):, :]
    conv = wc_ref[0:1, :] * s2 + wc_ref[1:2, :] * s1 + wc_ref[2:3, :] * ci
    gc_ref[...] = (g_post * conv).astype(BF16)

    q_ref[...] = (proj(0) * SB_SCALE).astype(BF16)
    if whole_seqs:
        k = proj(1)
        v = proj(2)
        k_ref[...] = k
        v_ref[...] = v
        kb_ref[...] = k.astype(BF16)
        vb_ref[...] = v.astype(BF16)
    else:
        k = _dot_nt(wkv_ref[:SB_WIDTH, :], h)
        v = _dot_nt(wkv_ref[SB_WIDTH:, :], h)
        k_ref[0] = k
        v_ref[0] = v
        kb_ref[0] = k.astype(BF16)
        vb_ref[0] = v.astype(BF16)


def _proj_even(x, g, w_in, w_kv_t, w_conv, conv_prev, *, batch, seq_len, tm):
    n = x.shape[0]
    whole = tm % seq_len == 0
    grid = (n // tm,)
    row_spec = lambda w: pl.BlockSpec((tm, w), lambda i: (i, 0))
    in_specs = [row_spec(D_MODEL), _const_spec((1, D_MODEL)), _const_spec(w_in.shape), _const_spec(w_conv.shape)]
    args = [x, g, w_in, w_conv]
    if whole:
        kv_shape, kv_spec = (n, SB_WIDTH), row_spec(SB_WIDTH)
    else:
        seq_tiles = seq_len // tm
        kv_shape = (batch, SB_WIDTH, seq_len)
        kv_spec = pl.BlockSpec((1, SB_WIDTH, tm), lambda i: (i // seq_tiles, 0, i % seq_tiles))
    out_shape = [jax.ShapeDtypeStruct((n, SB_WIDTH), BF16), jax.ShapeDtypeStruct(kv_shape, F32),
                 jax.ShapeDtypeStruct(kv_shape, F32), jax.ShapeDtypeStruct(kv_shape, BF16),
                 jax.ShapeDtypeStruct(kv_shape, BF16), jax.ShapeDtypeStruct((n, CONV_DIM), BF16)]
    out_specs = [row_spec(SB_WIDTH)] + [kv_spec] * 4 + [row_spec(CONV_DIM)]
    scratch = []
    if whole:
        if conv_prev is None:
            conv_prev = jnp.zeros((batch, CONV_W - 1, CONV_DIM), F32)
        e1 =jnp.zeros((batch, seq_len, CONV_DIM), F32).at[:, 0].set(conv_prev[:, 1])
        e2 = jnp.zeros((batch, seq_len, CONV_DIM), F32).at[:, 0].set(conv_prev[:, 0]).at[:, 1].set(conv_prev[:, 1])
        args += [e1.reshape(n, CONV_DIM), e2.reshape(n, CONV_DIM)]
        in_specs += [row_spec(CONV_DIM), row_spec(CONV_DIM)]
        out_shape.append(jax.ShapeDtypeStruct((n, CONV_DIM), F32))
        out_specs.append(row_spec(CONV_DIM))
    else:
        assert seq_len % tm == 0 and conv_prev is None
        args.append(w_kv_t)
        in_specs.append(_const_spec(w_kv_t.shape))
        out_shape.append(jax.ShapeDtypeStruct((batch, CONV_W - 1, CONV_DIM), F32))
        out_specs.append(pl.BlockSpec((1, CONV_W - 1, CONV_DIM), lambda i: (i // seq_tiles, 0, 0)))
        scratch.append(pltpu.VMEM((8, CONV_DIM), F32))
    outs = pl.pallas_call(
        functools.partial(_proj_even_body, tm=tm, seq_len=seq_len, whole_seqs=whole),
        grid=grid, in_specs=in_specs, out_specs=out_specs, out_shape=out_shape, scratch_shapes=scratch,
        compiler_params=_cparams(("arbitrary",)), name="proj_even")(*args)
    q, k, v, kb, vb, gc, last = outs
    if whole:
        last = last.reshape(batch, seq_len, CONV_DIM)[:, seq_len - (CONV_W - 1):]
    return q, k, v, kb, vb, gc, last


def _suffix_rhs(tk):
    j = np.arange(2 * tk)[:, None] % tk
    c = np.arange(2 * tk)[None, :]
    return jnp.asarray(np.where(c < tk, j > c, True), dtype=BF16)


def _sb_prompt_body(q_ref, k_ref, v_ref, r_ref, o_ref, carry_ref, acc_ref, *, tq, tk):
    qi = pl.program_id(1)
    ratio = tq // tk
    even = lax.broadcasted_iota(jnp.int32, (LANES, tk), 0) < SB_HEAD_DIM
    rhs = r_ref[...]
    carry_ref[...] = jnp.zeros_like(carry_ref)
    acc_ref[...] = jnp.zeros_like(acc_ref)
    qs = [q_ref[:, p * LANES:(p + 1) * LANES] for p in range(HEAD_PAIRS)]

    def block_diag(blk):
        zero = jnp.zeros_like(blk)
        return jnp.concatenate([jnp.where(even, blk, zero), jnp.where(even, zero, blk)], axis=1)

    def tile(kj, masked, r0=0):
        start = pl.multiple_of(kj * tk, tk)
        if masked:
            q_pos = qi * tq + r0 + lax.broadcasted_iota(jnp.int32, (tq - r0, tk), 0)
            k_pos = kj * tk + lax.broadcasted_iota(jnp.int32, (tq - r0, tk), 1)
            visible = k_pos < q_pos
        zs = [_dot(qs[p][r0:], block_diag(k_ref[0, p * LANES:(p + 1) * LANES, pl.ds(start, tk)]))
              for p in range(HEAD_PAIRS)]
        log_betas, splits = [], []
        for hd in range(SB_HEADS):
            z = zs[hd // 2][:, (hd % 2) * tk:(hd % 2 + 1) * tk]
            log_beta = _log_sigmoid(z)
            log_1m = log_beta - z
            if masked:
                log_1m = jnp.where(visible, log_1m, 0.0)
            hi, lo = _split_hi_lo(log_1m)
            log_betas.append(log_beta)
            splits.append(jnp.concatenate([hi, lo], axis=1))
        sums = [_dot(sp, rhs) for sp in splits]
        ws = []
        for hd in range(SB_HEADS):
            carry = carry_ref[hd, r0:, :]
            w = jnp.exp(log_betas[hd] + sums[hd][:, :tk] + carry)
            if masked:
                w = jnp.where(visible, w, 0.0)
            carry_ref[hd, r0:, :] = carry + sums[hd][:, tk:]
            ws.append(w.astype(BF16))
        for p in range(HEAD_PAIRS):
            vbd = block_diag(v_ref[0, p * LANES:(p + 1) * LANES, pl.ds(start, tk)])
            acc_ref[p, r0:, :] += _dot_nt(jnp.concatenate([ws[2 * p], ws[2 * p + 1]], axis=1), vbd)

    for t in reversed(range(ratio)):
        tile(qi * ratio + t, True, r0=t * tk)

    def body(it, c):
        tile(qi * ratio - 1 - 2 * it, False)
        tile(qi * ratio - 2 - 2 * it, False)
        return c

    lax.fori_loop(0, qi * (ratio // 2), body, 0)
    for p in range(HEAD_PAIRS):
        o_ref[:, p * LANES:(p + 1) * LANES] = acc_ref[p].astype(BF16)


def _sb_prompt(q, kb, vb, *, batch, seq_len, tq, tk):
    assert (tq // tk) % 2 == 0 and tq % tk == 0 and seq_len % tq == 0
    n = q.shape[0]
    nq = seq_len // tq
    rhs = _suffix_rhs(tk)
    q_spec = pl.BlockSpec((tq, SB_WIDTH), lambda b, i: (b * nq + i, 0))
    kv_spec = pl.BlockSpec((1, SB_WIDTH, seq_len), lambda b, i: (b, 0, 0))
    return pl.pallas_call(
        functools.partial(_sb_prompt_body, tq=tq, tk=tk),
        grid=(batch, nq),
        in_specs=[q_spec, kv_spec, kv_spec, _const_spec(rhs.shape)],
        out_specs=q_spec,
        out_shape=jax.ShapeDtypeStruct((n, SB_WIDTH), BF16),
        scratch_shapes=[pltpu.VMEM((SB_HEADS, tq, LANES), F32), pltpu.VMEM((HEAD_PAIRS, tq, LANES), F32)],
        compiler_params=_cparams(("arbitrary", "arbitrary")), name="sb_attn_prompt")(q, kb, vb, rhs)


def _sb_sample_body(q_ref, kn_ref, vn_ref, kc_ref, vc_ref, l_ref, o_ref, qh_ref, carry_ref, acc_ref,
                    *, dec, kblk, tk):
    j = pl.program_id(1)
    rows = SB_HEADS * dec
    rhs = l_ref[...]

    def tiles(kts, vts, visible):
        log_betas, sums = [], []
        for kt in kts:
            z = _dot(qh_ref[...], kt)
            log_beta = _log_sigmoid(z)
            log_1m = log_beta - z
            if visible is not None:
                log_1m = jnp.where(visible, log_1m, 0.0)
            hi, lo = _split_hi_lo(log_1m)
            log_betas.append(log_beta)
            sums.append(_dot(jnp.concatenate([hi, lo], axis=1), rhs))
        carry = carry_ref[...]
        acc = acc_ref[...]
        for log_beta, sm, vt in zip(log_betas, sums, vts):
            w = jnp.exp(log_beta + sm[:, :tk] + carry)
            if visible is not None:
                w = jnp.where(visible, w, 0.0)
            acc = acc + _dot_nt(w.astype(BF16), vt)
            carry = carry + sm[:, tk:]
        carry_ref[...] = carry
        acc_ref[...] = acc

    def new_keys_t(x):
        x = jnp.concatenate([x.astype(F32), jnp.zeros((tk - dec, SB_WIDTH), F32)], axis=0)
        return x.T.astype(BF16)

    @pl.when(j == 0)
    def _():
        q = q_ref[...]
        qt = jnp.concatenate([q] * SB_HEADS, axis=0)
        rr = lax.broadcasted_iota(jnp.int32, (rows, SB_WIDTH), 0)
        cc = lax.broadcasted_iota(jnp.int32, (rows, SB_WIDTH), 1)
        qh_ref[...] = jnp.where(rr // dec == cc // SB_HEAD_DIM, qt, jnp.zeros_like(qt))
        carry_ref[...] = jnp.zeros_like(carry_ref)
        acc_ref[...] = jnp.zeros_like(acc_ref)
        query_t = lax.broadcasted_iota(jnp.int32, (rows, tk), 0) % dec
        key_i = lax.broadcasted_iota(jnp.int32, (rows, tk), 1)
        tiles([new_keys_t(kn_ref[...])], [new_keys_t(vn_ref[...])],
              key_i < query_t)

    order = list(reversed(range(kblk // tk)))
    tiles([kc_ref[0, :, s * tk:(s + 1) * tk].astype(BF16) for s in order],
          [vc_ref[0, :, s * tk:(s + 1) * tk].astype(BF16) for s in order], None)

    @pl.when(j == pl.num_programs(1) - 1)
    def _():
        cc = lax.broadcasted_iota(jnp.int32, (dec, SB_WIDTH), 1)
        out = jnp.zeros((dec, SB_WIDTH), F32)
        for h in range(SB_HEADS):
            out = out + jnp.where(cc // SB_HEAD_DIM == h, acc_ref[h * dec:(h + 1) * dec, :], 0.0)
        o_ref[...] = out.astype(BF16)


def _sb_sample(q, kb, vb, cache_kt, cache_vt, *, batch, dec, kblk=1024, tk=256):
    n = q.shape[0]
    past = cache_kt.shape[2]
    nkb = past // kblk
    rows = SB_HEADS * dec
    rhs = _suffix_rhs(tk)
    new_spec = pl.BlockSpec((dec, SB_WIDTH), lambda b, j: (b, 0))
    cache_spec = pl.BlockSpec((1, SB_WIDTH, kblk), lambda b, j: (b, 0, nkb - 1 - j))
    return pl.pallas_call(
        functools.partial(_sb_sample_body, dec=dec, kblk=kblk, tk=tk),
        grid=(batch, nkb),
        in_specs=[new_spec, new_spec, new_spec, cache_spec, cache_spec, _const_spec(rhs.shape)],
        out_specs=new_spec,
        out_shape=jax.ShapeDtypeStruct((n, SB_WIDTH), BF16),
        scratch_shapes=[pltpu.VMEM((rows, SB_WIDTH), BF16), pltpu.VMEM((rows, tk), F32),
                        pltpu.VMEM((rows, SB_WIDTH), F32)],
        compiler_params=_cparams(("arbitrary", "arbitrary")), name="sb_attn_sample")(
            q, kb, vb, cache_kt, cache_vt, rhs)


def _proj_odd_body(*refs, tm, gate_len, emit_vn, kpe_transposed):
    (x_ref, g_ref, w_ref, lng_ref, lnb_ref, ws_ref, bs_ref, qg_ref, kvg_ref, wqn_ref, wqp_ref,
     cos_ref, sin_ref) = refs[:13]
    outs = refs[13:]
    if emit_vn:
        sgu_ref, vn_ref, qn_ref, qp_ref, ckv_ref, ckvb_ref, kpe_ref, kpeb_ref = outs
    else:
        sgu_ref, qn_ref, qp_ref, ckv_ref, ckvb_ref, kpe_ref, kpeb_ref = outs
    h = _rms(x_ref[...], g_ref[...]).astype(BF16)
    o_v, o_q, o_kv, o_pe = SGU_DIM, 2 * SGU_DIM, 2 * SGU_DIM + Q_LORA, 2 * SGU_DIM + Q_LORA + KV_LORA
    cos = cos_ref[...]
    sin = sin_ref[...]
    kpe = _rope_block(_dot(h, w_ref[:, o_pe:]), cos, sin)
    if kpe_transposed:
        kpe_ref[0] = kpe.T[:ROPE_DIM, :]
    else:
        kpe_ref[...] = kpe[:, :ROPE_DIM]
    kpeb_ref[...] = kpe.astype(BF16)
    ckv = _rms(_dot(h, w_ref[:, o_kv:o_pe]), kvg_ref[...])
    ckv_ref[...] = ckv
    ckvb_ref[...] = ckv.astype(BF16)
    cq = _rms(_dot(h, w_ref[:, o_q:o_kv]), qg_ref[...]).astype(BF16)
    for blk in range(MLA_HEADS * ROPE_DIM // LANES):
        bs = slice(blk * LANES, (blk + 1) * LANES)
        qp_ref[:, bs] = _rope_block(_dot(cq, wqp_ref[:, bs]), cos, sin).astype(BF16)
    qn_ref[...] = _dot(cq, wqn_ref[...]).astype(BF16)
    u = _dot(h, w_ref[:, :o_v])
    v = _dot(h, w_ref[:, o_v:o_q])
    mu = jnp.mean(v, axis=-1, keepdims=True)
    vc = v - mu
    var = jnp.mean(vc * vc, axis=-1, keepdims=True)
    vn = vc * lax.rsqrt(var + EPS) * lng_ref[...] + lnb_ref[...]
    if emit_vn:
        vn_ref[...] = vn
    vnb = vn.astype(BF16)
    rr = lax.broadcasted_iota(jnp.int32, (SGU_CHUNK, SGU_CHUNK), 0)
    cc = lax.broadcasted_iota(jnp.int32, (SGU_CHUNK, SGU_CHUNK), 1)
    causal = (rr // gate_len == cc // gate_len) & (cc <= rr)
    for g in range(SGU_GROUPS):
        gs = slice(g * SGU_GROUP_DIM, (g + 1) * SGU_GROUP_DIM)
        wg = jnp.where(causal, ws_ref[g], 0.0).astype(BF16)
        for c in range(tm // SGU_CHUNK):
            rs = slice(c * SGU_CHUNK, (c + 1) * SGU_CHUNK)
            s = _dot(wg, vnb[rs, gs]) + bs_ref[:, gs]
            sgu_ref[rs, gs] = (u[rs, gs] * s).astype(BF16)


def _proj_odd(x, g, w_in, ln_g, ln_b, w_s, b_s, qg, kvg, wqn, wqp, cos, sin, *, batch, seq_len, tm, gate_len,
              emit_vn):
    n = x.shape[0]
    pos_tiles = cos.shape[0] // tm
    kpe_transposed = tm % seq_len != 0
    if kpe_transposed:
        seq_tiles = seq_len // tm
        kpe_shape = (batch, ROPE_DIM, seq_len)
        kpe_spec = pl.BlockSpec((1, ROPE_DIM, tm), lambda i: (i // seq_tiles, 0, i % seq_tiles))
    else:
        kpe_shape, kpe_spec = (n, ROPE_DIM), pl.BlockSpec((tm, ROPE_DIM), lambda i: (i, 0))
    row_spec = lambda w: pl.BlockSpec((tm, w), lambda i: (i, 0))
    tab_spec = pl.BlockSpec((tm, LANES), lambda i: (i % pos_tiles, 0))
    consts = [g, w_in, ln_g, ln_b, w_s, b_s, qg, kvg, wqn, wqp]
    out_shape = [jax.ShapeDtypeStruct((n, SGU_DIM), BF16)]
    out_specs = [row_spec(SGU_DIM)]
    if emit_vn:
        out_shape.append(jax.ShapeDtypeStruct((n, SGU_DIM), F32))
        out_specs.append(row_spec(SGU_DIM))
    out_shape += [jax.ShapeDtypeStruct((n, MLA_HEADS * NOPE_DIM), BF16),
                  jax.ShapeDtypeStruct((n, MLA_HEADS * ROPE_DIM), BF16),
                  jax.ShapeDtypeStruct((n, KV_LORA), F32), jax.ShapeDtypeStruct((n, KV_LORA), BF16),
                  jax.ShapeDtypeStruct(kpe_shape, F32), jax.ShapeDtypeStruct((n, LANES), BF16)]
    out_specs += [row_spec(MLA_HEADS * NOPE_DIM), row_spec(MLA_HEADS * ROPE_DIM),
                  row_spec(KV_LORA), row_spec(KV_LORA), kpe_spec, row_spec(LANES)]
    return pl.pallas_call(
        functools.partial(_proj_odd_body, tm=tm, gate_len=gate_len, emit_vn=emit_vn, kpe_transposed=kpe_transposed),
        grid=(n // tm,),
        in_specs=[row_spec(D_MODEL)] + [_const_spec(c.shape) for c in consts] + [tab_spec, tab_spec],
        out_specs=out_specs, out_shape=out_shape,
        compiler_params=_cparams(("arbitrary",)), name="proj_odd")(x, *consts, cos, sin)


def _mla_queries(qn, qpe, wuk_ref, qcat_ref, tq):
    lane = lax.broadcasted_iota(jnp.int32, (tq, LANES), 1)
    low = lane < NOPE_DIM
    qpe = qpe.astype(F32)
    qcat_ref[:, KV_LORA:] = jnp.zeros((MLA_HEADS * tq, MLA_QK - KV_LORA), BF16)
    for hd in range(MLA_HEADS):
        p = hd // 2
        rs = slice(hd * tq, (hd + 1) * tq)
        pair = qn[:, p * LANES:(p + 1) * LANES]
        qm = jnp.where(low if hd % 2 == 0 else jnp.logical_not(low), pair, jnp.zeros_like(pair))
        qcat_ref[rs, :KV_LORA] = _dot(qm, wuk_ref[p]).astype(BF16)
        qcat_ref[rs, KV_LORA:KV_LORA + ROPE_DIM] = qpe[:, hd * ROPE_DIM:(hd + 1) * ROPE_DIM].astype(BF16)


def _lane_tile(x, width):
    return jnp.concatenate([x] * (width // LANES), axis=1)


def _mla_scores(qcat_ref, kcat, sc):
    sc[0][...] = _dot_nt(qcat_ref[...], kcat)


def _mla_softmax(sc, visible_fn, first):
    s_ref, p_ref, m_ref, l_ref, acc_ref = sc
    rows, tk = s_ref.shape
    for c in range(rows // MLA_ROW_CHUNK):
        rs = slice(c * MLA_ROW_CHUNK, (c + 1) * MLA_ROW_CHUNK)
        s = s_ref[rs, :] * (MLA_SCALE * LOG2_E)
        if visible_fn is not None:
            s = jnp.where(visible_fn(c), s, NEG_INF)
        if first:
            m_new = jnp.broadcast_to(jnp.max(s, axis=-1, keepdims=True), (MLA_ROW_CHUNK, LANES))
            p = jnp.exp2(s - _lane_tile(m_new, tk))
            l_ref[rs, :] = jnp.broadcast_to(jnp.sum(p, axis=-1, keepdims=True), (MLA_ROW_CHUNK, LANES))
        else:
            m_prev = m_ref[rs, :]
            m_new = jnp.maximum(m_prev, jnp.max(s, axis=-1, keepdims=True))
            alpha = jnp.exp2(m_prev - m_new)
            p = jnp.exp2(s - _lane_tile(m_new, tk))
            l_ref[rs, :] = alpha * l_ref[rs, :] + jnp.sum(p, axis=-1, keepdims=True)
        m_ref[rs, :] = m_new
        p_ref[rs, :] = p.astype(BF16)
        if not first:
            acc_ref[rs, :] = acc_ref[rs, :] * _lane_tile(alpha, KV_LORA)


def _mla_pv(ck, sc, first):
    p_ref, acc_ref = sc[1], sc[4]
    pv = _dot(p_ref[...], ck)
    acc_ref[...] = pv if first else acc_ref[...] + pv


def _mla_tile(qcat_ref, kcat, sc, visible_fn, first=False):
    _mla_scores(qcat_ref, kcat, sc)
    _mla_softmax(sc, visible_fn, first)
    _mla_pv(kcat[:, :KV_LORA], sc, first)


def _mla_finish(sc, wuv_ref, o_ref, tq):
    l_ref, acc_ref = sc[3], sc[4]
    o_lat = (acc_ref[...] / _lane_tile(l_ref[...], KV_LORA)).astype(BF16)
    for p in range(MLA_HEADS // 2):
        h0, h1 = 2 * p, 2 * p + 1
        o_ref[:, p * LANES:(p + 1) * LANES] = (
            _dot(o_lat[h0 * tq:(h0 + 1) * tq], wuv_ref[h0]) + _dot(o_lat[h1 * tq:(h1 + 1) * tq], wuv_ref[h1])
        ).astype(BF16)


def _mla_prompt_body(qn_ref, qp_ref, ckv_ref, kpe_ref, wuk_ref, wuv_ref, o_ref, qcat_ref, *sc, tq, tk):
    qi = pl.program_id(1)
    _mla_queries(qn_ref[...], qp_ref[...], wuk_ref, qcat_ref, tq)

    n_full = (qi * tq) // tk
    col = lax.broadcasted_iota(jnp.int32, (MLA_ROW_CHUNK, tk), 1)

    def visible(c):
        q_chunk_end = qi * tq + ((c * MLA_ROW_CHUNK) % tq) // CHUNK * CHUNK + CHUNK
        return col < q_chunk_end - n_full * tk

    def keys(kj):
        start = pl.multiple_of(kj * tk, tk)
        return jnp.concatenate([ckv_ref[pl.ds(start, tk), :], kpe_ref[pl.ds(start, tk), :],
                                jnp.zeros((tk, MLA_QK - KV_LORA - LANES), BF16)], axis=1)

    _mla_tile(qcat_ref, keys(n_full), sc, visible, first=True)

    def body(kj, c):
        _mla_tile(qcat_ref, keys(kj), sc, None)
        return c

    lax.fori_loop(0, n_full, body, 0)
    _mla_finish(sc, wuv_ref, o_ref, tq)


def _mla_scratch(rows, tk):
    return [pltpu.VMEM((rows, MLA_QK), BF16), pltpu.VMEM((rows, tk), F32), pltpu.VMEM((rows, tk), BF16),
            pltpu.VMEM((rows, LANES), F32), pltpu.VMEM((rows, LANES), F32), pltpu.VMEM((rows, KV_LORA), F32)]


def _mla_prompt(qn, qp, ckvb, kpeb, wuk, wuv, *, batch, seq_len, tq, tk):
    assert tq % CHUNK == 0 and CHUNK % MLA_ROW_CHUNK == 0 and seq_len % tq == 0 and seq_len % tk == 0
    n = qn.shape[0]
    nq = seq_len // tq
    width = MLA_HEADS * V_DIM
    return pl.pallas_call(
        functools.partial(_mla_prompt_body, tq=tq, tk=tk),
        grid=(batch, nq),
        in_specs=[pl.BlockSpec((tq, MLA_HEADS * NOPE_DIM), lambda b, i: (b * nq + i, 0)),
                  pl.BlockSpec((tq, MLA_HEADS * ROPE_DIM), lambda b, i: (b * nq + i, 0)),
                  pl.BlockSpec((seq_len, KV_LORA), lambda b, i: (b, 0)),
                  pl.BlockSpec((seq_len, LANES), lambda b, i: (b, 0)),
                  _const_spec(wuk.shape), _const_spec(wuv.shape)],
        out_specs=pl.BlockSpec((tq, width), lambda b, i: (b * nq + i, 0)),
        out_shape=jax.ShapeDtypeStruct((n, width), BF16),
        scratch_shapes=_mla_scratch(MLA_HEADS * tq, tk),
        compiler_params=_cparams(("arbitrary", "arbitrary")), name="mla_attn_prompt")(qn, qp, ckvb, kpeb, wuk, wuv)


def _mla_sample_body(qn_ref, qp_ref, cn_ref, pn_ref, cc_ref, pc_ref, wuk_ref, wuv_ref, o_ref, kcat_ref, qcat_ref,
                     *sc, dec, past, kblk, tk):
    j = pl.program_id(1)
    rows = MLA_HEADS * dec

    @pl.when(j == 0)
    def _():
        _mla_queries(qn_ref[...], qp_ref[...], wuk_ref, qcat_ref, dec)
        kcat_ref[:, KV_LORA:] = jnp.zeros((tk, MLA_QK - KV_LORA), BF16)
        new = jnp.concatenate([cn_ref[...], pn_ref[...], jnp.zeros((dec, MLA_QK - KV_LORA - LANES), BF16)], axis=1)
        kcat = jnp.concatenate([new, jnp.zeros((tk - dec, MLA_QK), BF16)], axis=0)

        def visible(c):
            col = lax.broadcasted_iota(jnp.int32, (MLA_ROW_CHUNK, tk), 1)
            row = c * MLA_ROW_CHUNK + lax.broadcasted_iota(jnp.int32, (MLA_ROW_CHUNK, tk), 0)
            return (col < dec) & ((past + col) // CHUNK <= (past + row % dec) // CHUNK)

        _mla_tile(qcat_ref, kcat, sc, visible, first=True)

    def body(it, c):
        start = pl.multiple_of(it * tk, tk)
        kcat_ref[:, :KV_LORA] = cc_ref[0, pl.ds(start, tk), :].astype(BF16)
        kp_t = jnp.concatenate([pc_ref[0, :, pl.ds(start, tk)], jnp.zeros((LANES - ROPE_DIM, tk), F32)], axis=0)
        kcat_ref[:, KV_LORA:KV_LORA + LANES] = kp_t.T.astype(BF16)
        _mla_tile(qcat_ref, kcat_ref[...], sc, None)
        return c

    lax.fori_loop(0, kblk // tk, body, 0)

    @pl.when(j == pl.num_programs(1) - 1)
    def _():
        _mla_finish(sc, wuv_ref, o_ref, dec)


def _mla_sample(qn, qp, ckvb, kpeb, cache_ckv, cache_kpe_t, wuk, wuv, *, batch, dec, kblk=1024, tk=1024):
    n = qn.shape[0]
    past = cache_ckv.shape[1]
    width = MLA_HEADS * V_DIM
    return pl.pallas_call(
        functools.partial(_mla_sample_body, dec=dec, past=past, kblk=kblk, tk=tk),
        grid=(batch, past // kblk),
        in_specs=[pl.BlockSpec((dec, MLA_HEADS * NOPE_DIM), lambda b, j: (b, 0)),
                  pl.BlockSpec((dec, MLA_HEADS * ROPE_DIM), lambda b, j: (b, 0)),
                  pl.BlockSpec((dec, KV_LORA), lambda b, j: (b, 0)),
                  pl.BlockSpec((dec, LANES), lambda b, j: (b, 0)),
                  pl.BlockSpec((1, kblk, KV_LORA), lambda b, j: (b, j, 0)),
                  pl.BlockSpec((1, ROPE_DIM, kblk), lambda b, j: (b, 0, j)),
                  _const_spec(wuk.shape), _const_spec(wuv.shape)],
        out_specs=pl.BlockSpec((dec, width), lambda b, j: (b, 0)),
        out_shape=jax.ShapeDtypeStruct((n, width), BF16),
        scratch_shapes=[pltpu.VMEM((tk, MLA_QK), BF16)] + _mla_scratch(MLA_HEADS * dec, tk),
        compiler_params=_cparams(("arbitrary", "arbitrary")), name="mla_attn_sample")(
            qn, qp, ckvb, kpeb, cache_ckv, cache_kpe_t, wuk, wuv)


def _out_ffn_body(a_ref, b_ref, x_ref, woa_ref, wob_ref, gpost_ref, gpre_ref, wup_ref, wdn_ref, gfpost_ref, o_ref):
    tm = x_ref.shape[0]
    groups = [slice(r * tm // FFN_ROW_GROUPS, (r + 1) * tm // FFN_ROW_GROUPS) for r in range(FFN_ROW_GROUPS)]
    mixed = [_dot(a_ref[rs, :], woa_ref[...]) + _dot(b_ref[rs, :], wob_ref[...]) for rs in groups]
    x1 = [x_ref[rs, :] + _rms(m, gpost_ref[...]) for rs, m in zip(groups, mixed)]
    h = [_rms(x, gpre_ref[...]).astype(BF16) for x in x1]
    down = [jnp.zeros_like(x) for x in x1]
    for c in range(D_FF // FF_CHUNK):
        for r in range(FFN_ROW_GROUPS):
            up = _dot(h[r], wup_ref[:, c * FF_CHUNK:(c + 1) * FF_CHUNK])
            act = jnp.square(jnp.maximum(up, 0.0)).astype(BF16)
            down[r] = down[r] + _dot(act, wdn_ref[c * FF_CHUNK:(c + 1) * FF_CHUNK, :])
    for r, rs in enumerate(groups):
        o_ref[rs, :] = x1[r] + _rms(down[r], gfpost_ref[...])


def _out_ffn(a, b, x, woa, wob, g_post, g_pre, w_up, w_down, g_fpost, *, tm):
    n = x.shape[0]
    row_spec = lambda w: pl.BlockSpec((tm, w), lambda i: (i, 0))
    consts = [woa, wob, g_post, g_pre, w_up, w_down, g_fpost]
    return pl.pallas_call(
        _out_ffn_body,
        grid=(n // tm,),
        in_specs=[row_spec(a.shape[1]), row_spec(b.shape[1]), row_spec(D_MODEL)] + [_const_spec(c.shape) for c in consts],
        out_specs=row_spec(D_MODEL),
        out_shape=jax.ShapeDtypeStruct((n, D_MODEL), F32),
        compiler_params=_cparams(("arbitrary",)), name="out_ffn")(a, b, x, *consts)


def _rope_tables(pos, reps):
    half = ROPE_DIM // 2
    inv = ROPE_THETA ** (-jnp.arange(half, dtype=F32) / half)
    ang = pos.astype(F32)[:, None] * inv[None, :]
    cos = jnp.tile(jnp.concatenate([jnp.cos(ang), jnp.cos(ang)], axis=1), (reps, LANES // ROPE_DIM))
    sin = jnp.tile(jnp.concatenate([-jnp.sin(ang), jnp.sin(ang)], axis=1), (reps, LANES // ROPE_DIM))
    return cos, sin


def _prep_even(p, j):
    w_in = p["even_w_in"][j]
    return dict(w_in=w_in.astype(BF16), w_kv_t=w_in[:, SB_WIDTH:3 * SB_WIDTH].T.astype(BF16), w_conv=p["even_w_conv"][j],
                woa=p["even_w_out"][j, :SB_WIDTH].astype(BF16), wob=p["even_w_out"][j, SB_WIDTH:].astype(BF16))


def _prep_odd(p, j, gate_len):
    w_in = p["odd_w_in"][j]
    w_in = jnp.pad(w_in, ((0, 0), (0, ODD_IN_PAD - w_in.shape[1]))).astype(BF16)
    reps = SGU_CHUNK // gate_len
    w_s = jnp.tile(p["sgu_w_s"][j, :, :gate_len, :gate_len], (1, reps, reps))
    b_s = jnp.tile(p["sgu_b_s"][j, :, :gate_len], (1, reps))
    b_s = jnp.repeat(b_s.T, SGU_GROUP_DIM, axis=1)
    w_uq = p["mla_w_uq"][j].reshape(Q_LORA, MLA_HEADS, NOPE_DIM + ROPE_DIM)
    wqn = w_uq[:, :, :NOPE_DIM].reshape(Q_LORA, MLA_HEADS * NOPE_DIM).astype(BF16)
    wqp = w_uq[:, :, NOPE_DIM:].reshape(Q_LORA, MLA_HEADS * ROPE_DIM).astype(BF16)
    wuk = p["mla_w_uk"][j].reshape(MLA_HEADS // 2, 2 * NOPE_DIM, KV_LORA).astype(BF16)
    w_uv = p["mla_w_uv"][j]
    wuv = jnp.stack([jnp.pad(w_uv[h], ((0, 0), ((h % 2) * V_DIM, (1 - h % 2) * V_DIM))) for h in range(MLA_HEADS)])
    return dict(w_in=w_in, ln_g=p["sgu_ln_g"][j][None], ln_b=p["sgu_ln_b"][j][None], w_s=w_s, b_s=b_s,
                qg=p["mla_q_norm_g"][j][None], kvg=p["mla_kv_norm_g"][j][None], wqn=wqn, wqp=wqp, wuk=wuk,
                wuv=wuv.astype(BF16),
                woa=p["odd_w_out"][j, :SGU_DIM].astype(BF16), wob=p["odd_w_out"][j, SGU_DIM:].astype(BF16))


def _run_trunk(x, pos, past, p, *, batch, seq_len):
    depth = p["mix_pre_g"].shape[0]
    n = batch * seq_len
    x = x.reshape(n, D_MODEL)
    is_sample = past is not None
    tm = min(ROW_TILE, n)
    assert n % tm == 0 and (tm % seq_len == 0 or seq_len % tm == 0) and tm % SGU_CHUNK == 0
    gate_len = min(seq_len, SGU_CHUNK)
    tm_proj = PROJ_ROW_TILE if (seq_len % PROJ_ROW_TILE == 0 and seq_len > PROJ_ROW_TILE) else tm
    cos, sin = _rope_tables(pos, max(1, tm_proj // seq_len))
    st = {k: [] for k in ("sb_k", "sb_v", "conv", "ckv", "kpe", "sgu_v")}
    for layer in range(depth):
        j = layer // 2
        g_pre = p["mix_pre_g"][layer][None]
        if layer % 2 == 0:
            w = _prep_even(p, j)
            conv_prev = past["conv"][j] if is_sample else None
            q, k, v, kb, vb, b_mix, conv_state = _proj_even(x, g_pre, w["w_in"], w["w_kv_t"], w["w_conv"], conv_prev,
                                                            batch=batch, seq_len=seq_len, tm=tm_proj)
            if is_sample:
                assert tm % seq_len == 0

                def cache_t(c):
                    return jnp.transpose(c, (0, 2, 3, 1)).reshape(batch, SB_WIDTH, -1)

                a_mix = _sb_sample(q, kb, vb, cache_t(past["sb_k"][j]), cache_t(past["sb_v"][j]),
                                   batch=batch, dec=seq_len)
                k, v = (t.reshape(batch, seq_len, SB_HEADS, SB_HEAD_DIM) for t in (k, v))
            else:
                assert seq_len % tm == 0
                a_mix = _sb_prompt(q, kb, vb, batch=batch, seq_len=seq_len, tq=SB_TQ, tk=SB_TK)
                k, v = (jnp.transpose(t.reshape(batch, SB_HEADS, SB_HEAD_DIM, seq_len), (0, 3, 1, 2)) for t in (k, v))
            st["sb_k"].append(k)
            st["sb_v"].append(v)
            st["conv"].append(conv_state)
        else:
            w = _prep_odd(p, j, gate_len)
            outs = _proj_odd(x, g_pre, w["w_in"], w["ln_g"], w["ln_b"], w["w_s"], w["b_s"], w["qg"], w["kvg"],
                             w["wqn"], w["wqp"], cos, sin, batch=batch, seq_len=seq_len, tm=tm_proj, gate_len=gate_len,
                             emit_vn=is_sample)
            if is_sample:
                a_mix, vn, qn, qp, ckv, ckvb, kpe, kpeb = outs
                st["sgu_v"].append(vn.reshape(batch, seq_len, SGU_DIM))
                b_mix = _mla_sample(qn, qp, ckvb, kpeb, past["ckv"][j], jnp.transpose(past["kpe"][j], (0, 2, 1)),
                                    w["wuk"], w["wuv"],
                                    batch=batch, dec=seq_len)
            else:
                a_mix, qn, qp, ckv, ckvb, kpe, kpeb = outs
                b_mix = _mla_prompt(qn, qp, ckvb, kpeb, w["wuk"], w["wuv"], batch=batch, seq_len=seq_len,
                                    tq=MLA_TQ, tk=MLA_TK)
            st["ckv"].append(ckv.reshape(batch, seq_len, KV_LORA))
            st["kpe"].append(jnp.transpose(kpe, (0, 2, 1)) if kpe.ndim == 3 else kpe.reshape(batch, seq_len, ROPE_DIM))
        x = _out_ffn(a_mix, b_mix, x, w["woa"], w["wob"], p["mix_post_g"][layer][None], p["ffn_pre_g"][layer][None],
                     p["ffn_w_up"][layer].astype(BF16), p["ffn_w_down"][layer].astype(BF16),
                     p["ffn_post_g"][layer][None], tm=tm)
    states = {k: jnp.stack(v) for k, v in st.items() if v}
    return x.reshape(batch, seq_len, D_MODEL), states


def kernel(x_prompt, x_sample, cache_sb_k, cache_sb_v, state_conv, cache_mla_ckv, cache_mla_kpe,
           mix_pre_g, mix_post_g, ffn_pre_g, ffn_post_g, even_w_in, even_w_conv, even_w_out,
           odd_w_in, sgu_ln_g, sgu_ln_b, sgu_w_s, sgu_b_s, mla_q_norm_g, mla_kv_norm_g,
           mla_w_uq, mla_w_uk, mla_w_uv, odd_w_out, ffn_w_up, ffn_w_down):
    params = {
        "mix_pre_g": mix_pre_g, "mix_post_g": mix_post_g, "ffn_pre_g": ffn_pre_g, "ffn_post_g": ffn_post_g,
        "even_w_in": even_w_in, "even_w_conv": even_w_conv, "even_w_out": even_w_out,
        "odd_w_in": odd_w_in, "sgu_ln_g": sgu_ln_g, "sgu_ln_b": sgu_ln_b, "sgu_w_s": sgu_w_s,
        "sgu_b_s": sgu_b_s, "mla_q_norm_g": mla_q_norm_g, "mla_kv_norm_g": mla_kv_norm_g,
        "mla_w_uq": mla_w_uq, "mla_w_uk": mla_w_uk, "mla_w_uv": mla_w_uv, "odd_w_out": odd_w_out,
        "ffn_w_up": ffn_w_up, "ffn_w_down": ffn_w_down,
    }
    batch, seq_len, _ = x_prompt.shape
    pos_p = jnp.arange(seq_len, dtype=jnp.int32)
    y_prompt, st_p = _run_trunk(x_prompt, pos_p, None, params, batch=batch, seq_len=seq_len)
    dec_batch, dec_seq, _ = x_sample.shape
    past_len = cache_sb_k.shape[2]
    pos_s = past_len + jnp.arange(dec_seq, dtype=jnp.int32)
    past = {"sb_k": cache_sb_k, "sb_v": cache_sb_v, "conv": state_conv, "ckv": cache_mla_ckv, "kpe": cache_mla_kpe}
    y_sample, st_s = _run_trunk(x_sample, pos_s, past, params, batch=dec_batch, seq_len=dec_seq)
    return (y_prompt, y_sample,
            st_p["sb_k"], st_p["sb_v"], st_p["conv"], st_p["ckv"], st_p["kpe"],
            st_s["sb_k"], st_s["sb_v"], st_s["conv"], st_s["ckv"], st_s["kpe"], st_s["sgu_v"])
```

```python
import functools
import math

import numpy as np
import jax
import jax.numpy as jnp
from jax import lax
from jax.experimental import pallas as pl
from jax.experimental.pallas import tpu as pltpu

F32 = jnp.float32
BF16 = jnp.bfloat16

EPS = 1e-6
D_MODEL = 1024
CHUNK = 64
SB_HEADS = 8
SB_HEAD_DIM = 64
SB_WIDTH = SB_HEADS * SB_HEAD_DIM
SB_SCALE = 1.0 / math.sqrt(SB_HEAD_DIM)
CONV_DIM = D_MODEL // 2
CONV_W = 3
SGU_CHUNK = 128
SGU_GROUPS = 4
SGU_DIM = D_MODEL // 2
SGU_GROUP_DIM = SGU_DIM // SGU_GROUPS
MLA_HEADS = 8
Q_LORA = 384
KV_LORA = 256
NOPE_DIM = 64
ROPE_DIM = 32
V_DIM = 64
ROPE_THETA = 10000.0
MLA_SCALE = 1.0 / math.sqrt(NOPE_DIM + ROPE_DIM)
D_FF = 4 * D_MODEL
FF_CHUNK = 1024
FFN_ROW_GROUPS = 2

LANES = 128
HEAD_PAIRS = SB_HEADS // 2
MLA_QK = 2 * KV_LORA
ODD_IN_PAD =2 * SGU_DIM + Q_LORA + KV_LORA + LANES
VMEM_LIMIT = 56 * 1024 * 1024
NEG_INF = -1e30
LOG2_E = math.log2(math.e)

ROW_TILE = 512
PROJ_ROW_TILE = 1024
SB_TQ, SB_TK = 512, 128
MLA_TQ, MLA_TK = 256, 256
MLA_ROW_CHUNK = 64


def _cparams(sem):
    return pltpu.CompilerParams(dimension_semantics=sem, vmem_limit_bytes=VMEM_LIMIT)


def _const_spec(shape):
    nd = len(shape)
    return pl.BlockSpec(shape, lambda *_: (0,) * nd, pipeline_mode=pl.Buffered(1))


def _rms(x, g):
    return x * lax.rsqrt(jnp.mean(x * x, axis=-1, keepdims=True) + EPS) * g


def _dot(a, b):
    return jnp.dot(a, b, preferred_element_type=F32)


def _dot_nt(a, b):
    return lax.dot_general(a, b, (((1,), (1,)), ((), ())), preferred_element_type=F32)


def _log_sigmoid(z):
    neg_abs = lax.bitcast_convert_type(lax.bitcast_convert_type(z, jnp.uint32) | jnp.uint32(0x80000000), F32)
    return jnp.minimum(z, 0.0) - jnp.log(1.0 + jnp.exp(neg_abs))


def _split_hi_lo(x):
    hi = x.astype(BF16)
    lo = (x - hi.astype(F32)).astype(BF16)
    return hi, lo


def _rope_block(x, cos, sin):
    half = ROPE_DIM // 2
    lane = lax.broadcasted_iota(jnp.int32, x.shape, 1)
    partner = jnp.where(lane % ROPE_DIM < half, pltpu.roll(x, LANES - half, 1), pltpu.roll(x, half, 1))
    return x * cos + partner * sin


def _proj_even_body(*refs, tm, seq_len, whole_seqs):
    if whole_seqs:
        (x_ref, g_ref, w_ref, wc_ref, e1_ref, e2_ref,
         q_ref, k_ref, v_ref, kb_ref, vb_ref, gc_ref, ci_ref) = refs
    else:
        (x_ref, g_ref, w_ref, wc_ref, wkv_ref,
         q_ref, k_ref, v_ref, kb_ref, vb_ref, gc_ref, cs_ref, tail_ref) = refs
    if not whole_seqs:
        @pl.when((pl.program_id(0) % (seq_len // tm)) == 0)
        def _():
            tail_ref[...] = jnp.zeros_like(tail_ref)

    h = _rms(x_ref[...], g_ref[...]).astype(BF16)

    def proj(j):
        return _dot(h, w_ref[:, j * SB_WIDTH:(j + 1) * SB_WIDTH])

    g_post = proj(3)
    ci = proj(4) * proj(5)
    r1 = pltpu.roll(ci, 1, 0)
    r2 = pltpu.roll(ci, 2, 0)
    row = lax.broadcasted_iota(jnp.int32, (tm, 1), 0)
    if whole_seqs:
        tpos = row % seq_len
        s1 = jnp.where(tpos < 1, e1_ref[...], r1)
        s2 = jnp.where(tpos < 2, e2_ref[...], r2)
        ci_ref[...] = ci
    else:
        t1 = tail_ref[7:8, :]
        t2 = tail_ref[6:7, :]
        s1 = jnp.where(row == 0, t1, r1)
        s2 = jnp.where(row == 0, t2, jnp.where(row == 1, t1, r2))
        tail_ref[...] = ci[tm - 8:, :]
        cs_ref[0] = ci[tm - (CONV_W - 1):, :]
    conv = wc_ref[0:1, :] * s2 + wc_ref[1:2, :] * s1 + wc_ref[2:3, :] * ci
    gc_ref[...] = (g_post * conv).astype(BF16)

    q_ref[...] = (proj(0) * SB_SCALE).astype(BF16)
    if whole_seqs:
        k = proj(1)
        v = proj(2)
        k_ref[...] = k
        v_ref[...] = v
        kb_ref[...] = k.astype(BF16)
        vb_ref[...] = v.astype(BF16)
    else:
        k = _dot_nt(wkv_ref[:SB_WIDTH, :], h)
        v = _dot_nt(wkv_ref[SB_WIDTH:, :], h)
        k_ref[0] = k
        v_ref[0] = v
        kb_ref[0] = k.astype(BF16)
        vb_ref[0] = v.astype(BF16)


def _proj_even(x, g, w_in, w_kv_t, w_conv, conv_prev, *, batch, seq_len, tm):
    n = x.shape[0]
    whole = tm % seq_len == 0
    grid = (n // tm,)
    row_spec = lambda w: pl.BlockSpec((tm, w), lambda i: (i, 0))
    in_specs = [row_spec(D_MODEL), _const_spec((1, D_MODEL)), _const_spec(w_in.shape), _const_spec(w_conv.shape)]
    args = [x, g, w_in, w_conv]
    if whole:
        kv_shape, kv_spec = (n, SB_WIDTH), row_spec(SB_WIDTH)
    else:
        seq_tiles = seq_len // tm
        kv_shape = (batch, SB_WIDTH, seq_len)
        kv_spec = pl.BlockSpec((1, SB_WIDTH, tm), lambda i: (i // seq_tiles, 0, i % seq_tiles))
    out_shape = [jax.ShapeDtypeStruct((n, SB_WIDTH), BF16), jax.ShapeDtypeStruct(kv_shape, F32),
                 jax.ShapeDtypeStruct(kv_shape, F32), jax.ShapeDtypeStruct(kv_shape, BF16),
                 jax.ShapeDtypeStruct(kv_shape, BF16), jax.ShapeDtypeStruct((n, CONV_DIM), BF16)]
    out_specs = [row_spec(SB_WIDTH)] + [kv_spec] * 4 + [row_spec(CONV_DIM)]
    scratch = []
    if whole:
        if conv_prev is None:
            conv_prev = jnp.zeros((batch, CONV_W - 1, CONV_DIM), F32)
        e1 =jnp.zeros((batch, seq_len, CONV_DIM), F32).at[:, 0].set(conv_prev[:, 1])
        e2 = jnp.zeros((batch, seq_len, CONV_DIM), F32).at[:, 0].set(conv_prev[:, 0]).at[:, 1].set(conv_prev[:, 1])
        args += [e1.reshape(n, CONV_DIM), e2.reshape(n, CONV_DIM)]
        in_specs += [row_spec(CONV_DIM), row_spec(CONV_DIM)]
        out_shape.append(jax.ShapeDtypeStruct((n, CONV_DIM), F32))
        out_specs.append(row_spec(CONV_DIM))
    else:
        assert seq_len % tm == 0 and conv_prev is None
        args.append(w_kv_t)
        in_specs.append(_const_spec(w_kv_t.shape))
        out_shape.append(jax.ShapeDtypeStruct((batch, CONV_W - 1, CONV_DIM), F32))
        out_specs.append(pl.BlockSpec((1, CONV_W - 1, CONV_DIM), lambda i: (i // seq_tiles, 0, 0)))
        scratch.append(pltpu.VMEM((8, CONV_DIM), F32))
    outs = pl.pallas_call(
        functools.partial(_proj_even_body, tm=tm, seq_len=seq_len, whole_seqs=whole),
        grid=grid, in_specs=in_specs, out_specs=out_specs, out_shape=out_shape, scratch_shapes=scratch,
        compiler_params=_cparams(("arbitrary",)), name="proj_even")(*args)
    q, k, v, kb, vb, gc, last = outs
    if whole:
        last = last.reshape(batch, seq_len, CONV_DIM)[:, seq_len - (CONV_W - 1):]
    return q, k, v, kb, vb, gc, last


def _suffix_rhs(tk):
    j = np.arange(2 * tk)[:, None] % tk
    c = np.arange(2 * tk)[None, :]
    return jnp.asarray(np.where(c < tk, j > c, True), dtype=BF16)


def _sb_prompt_body(q_ref, k_ref, v_ref, r_ref, o_ref, carry_ref, acc_ref, *, tq, tk):
    qi = pl.program_id(1)
    ratio = tq // tk
    even = lax.broadcasted_iota(jnp.int32, (LANES, tk), 0) < SB_HEAD_DIM
    rhs = r_ref[...]
    carry_ref[...] = jnp.zeros_like(carry_ref)
    acc_ref[...] = jnp.zeros_like(acc_ref)
    qs = [q_ref[:, p * LANES:(p + 1) * LANES] for p in range(HEAD_PAIRS)]

    def block_diag(blk):
        zero = jnp.zeros_like(blk)
        return jnp.concatenate([jnp.where(even, blk, zero), jnp.where(even, zero, blk)], axis=1)

    def tile(kj, masked, r0=0):
        start = pl.multiple_of(kj * tk, tk)
        if masked:
            q_pos = qi * tq + r0 + lax.broadcasted_iota(jnp.int32, (tq - r0, tk), 0)
            k_pos = kj * tk + lax.broadcasted_iota(jnp.int32, (tq - r0, tk), 1)
            visible = k_pos < q_pos
        zs = [_dot(qs[p][r0:], block_diag(k_ref[0, p * LANES:(p + 1) * LANES, pl.ds(start, tk)]))
              for p in range(HEAD_PAIRS)]
        log_betas, splits = [], []
        for hd in range(SB_HEADS):
            z = zs[hd // 2][:, (hd % 2) * tk:(hd % 2 + 1) * tk]
            log_beta = _log_sigmoid(z)
            log_1m = log_beta - z
            if masked:
                log_1m = jnp.where(visible, log_1m, 0.0)
            hi, lo = _split_hi_lo(log_1m)
            log_betas.append(log_beta)
            splits.append(jnp.concatenate([hi, lo], axis=1))
        sums = [_dot(sp, rhs) for sp in splits]
        ws = []
        for hd in range(SB_HEADS):
            carry = carry_ref[hd, r0:, :]
            w = jnp.exp(log_betas[hd] + sums[hd][:, :tk] + carry)
            if masked:
                w = jnp.where(visible, w, 0.0)
            carry_ref[hd, r0:, :] = carry + sums[hd][:, tk:]
            ws.append(w.astype(BF16))
        for p in range(HEAD_PAIRS):
            vbd = block_diag(v_ref[0, p * LANES:(p + 1) * LANES, pl.ds(start, tk)])
            acc_ref[p, r0:, :] += _dot_nt(jnp.concatenate([ws[2 * p], ws[2 * p + 1]], axis=1), vbd)

    for t in reversed(range(ratio)):
        tile(qi * ratio + t, True, r0=t * tk)

    def body(it, c):
        tile(qi * ratio - 1 - 2 * it, False)
        tile(qi * ratio - 2 - 2 * it, False)
        return c

    lax.fori_loop(0, qi * (ratio // 2), body, 0)
    for p in range(HEAD_PAIRS):
        o_ref[:, p * LANES:(p + 1) * LANES] = acc_ref[p].astype(BF16)


def _sb_prompt(q, kb, vb, *, batch, seq_len, tq, tk):
    assert (tq // tk) % 2 == 0 and tq % tk == 0 and seq_len % tq == 0
    n = q.shape[0]
    nq = seq_len // tq
    rhs = _suffix_rhs(tk)
    q_spec = pl.BlockSpec((tq, SB_WIDTH), lambda b, i: (b * nq + i, 0))
    kv_spec = pl.BlockSpec((1, SB_WIDTH, seq_len), lambda b, i: (b, 0, 0))
    return pl.pallas_call(
        functools.partial(_sb_prompt_body, tq=tq, tk=tk),
        grid=(batch, nq),
        in_specs=[q_spec, kv_spec, kv_spec, _const_spec(rhs.shape)],
        out_specs=q_spec,
        out_shape=jax.ShapeDtypeStruct((n, SB_WIDTH), BF16),
        scratch_shapes=[pltpu.VMEM((SB_HEADS, tq, LANES), F32), pltpu.VMEM((HEAD_PAIRS, tq, LANES), F32)],
        compiler_params=_cparams(("arbitrary", "arbitrary")), name="sb_attn_prompt")(q, kb, vb, rhs)


def _sb_sample_body(q_ref, kn_ref, vn_ref, kc_ref, vc_ref, l_ref, o_ref, qh_ref, carry_ref, acc_ref,
                    *, dec, kblk, tk):
    j = pl.program_id(1)
    rows = SB_HEADS * dec
    rhs = l_ref[...]

    def tiles(kts, vts, visible):
        log_betas, sums = [], []
        for kt in kts:
            z = _dot(qh_ref[...], kt)
            log_beta = _log_sigmoid(z)
            log_1m = log_beta - z
            if visible is not None:
                log_1m = jnp.where(visible, log_1m, 0.0)
            hi, lo = _split_hi_lo(log_1m)
            log_betas.append(log_beta)
            sums.append(_dot(jnp.concatenate([hi, lo], axis=1), rhs))
        carry = carry_ref[...]
        acc = acc_ref[...]
        for log_beta, sm, vt in zip(log_betas, sums, vts):
            w = jnp.exp(log_beta + sm[:, :tk] + carry)
            if visible is not None:
                w = jnp.where(visible, w, 0.0)
            acc = acc + _dot_nt(w.astype(BF16), vt)
            carry = carry + sm[:, tk:]
        carry_ref[...] = carry
        acc_ref[...] = acc

    def new_keys_t(x):
        x = jnp.concatenate([x.astype(F32), jnp.zeros((tk - dec, SB_WIDTH), F32)], axis=0)
        return x.T.astype(BF16)

    @pl.when(j == 0)
    def _():
        q = q_ref[...]
        qt = jnp.concatenate([q] * SB_HEADS, axis=0)
        rr = lax.broadcasted_iota(jnp.int32, (rows, SB_WIDTH), 0)
        cc = lax.broadcasted_iota(jnp.int32, (rows, SB_WIDTH), 1)
        qh_ref[...] = jnp.where(rr // dec == cc // SB_HEAD_DIM, qt, jnp.zeros_like(qt))
        carry_ref[...] = jnp.zeros_like(carry_ref)
        acc_ref[...] = jnp.zeros_like(acc_ref)
        query_t = lax.broadcasted_iota(jnp.int32, (rows, tk), 0) % dec
        key_i = lax.broadcasted_iota(jnp.int32, (rows, tk), 1)
        tiles([new_keys_t(kn_ref[...])], [new_keys_t(vn_ref[...])],
              key_i < query_t)

    order = list(reversed(range(kblk // tk)))
    tiles([kc_ref[0, :, s * tk:(s + 1) * tk].astype(BF16) for s in order],
          [vc_ref[0, :, s * tk:(s + 1) * tk].astype(BF16) for s in order], None)

    @pl.when(j == pl.num_programs(1) - 1)
    def _():
        cc = lax.broadcasted_iota(jnp.int32, (dec, SB_WIDTH), 1)
        out = jnp.zeros((dec, SB_WIDTH), F32)
        for h in range(SB_HEADS):
            out = out + jnp.where(cc // SB_HEAD_DIM == h, acc_ref[h * dec:(h + 1) * dec, :], 0.0)
        o_ref[...] = out.astype(BF16)


def _sb_sample(q, kb, vb, cache_kt, cache_vt, *, batch, dec, kblk=1024, tk=256):
    n = q.shape[0]
    past = cache_kt.shape[2]
    nkb = past // kblk
    rows = SB_HEADS * dec
    rhs = _suffix_rhs(tk)
    new_spec = pl.BlockSpec((dec, SB_WIDTH), lambda b, j: (b, 0))
    cache_spec = pl.BlockSpec((1, SB_WIDTH, kblk), lambda b, j: (b, 0, nkb - 1 - j))
    return pl.pallas_call(
        functools.partial(_sb_sample_body, dec=dec, kblk=kblk, tk=tk),
        grid=(batch, nkb),
        in_specs=[new_spec, new_spec, new_spec, cache_spec, cache_spec, _const_spec(rhs.shape)],
        out_specs=new_spec,
        out_shape=jax.ShapeDtypeStruct((n, SB_WIDTH), BF16),
        scratch_shapes=[pltpu.VMEM((rows, SB_WIDTH), BF16), pltpu.VMEM((rows, tk), F32),
                        pltpu.VMEM((rows, SB_WIDTH), F32)],
        compiler_params=_cparams(("arbitrary", "arbitrary")), name="sb_attn_sample")(
            q, kb, vb, cache_kt, cache_vt, rhs)


def _proj_odd_body(*refs, tm, gate_len, emit_vn, kpe_transposed):
    (x_ref, g_ref, w_ref, lng_ref, lnb_ref, ws_ref, bs_ref, qg_ref, kvg_ref, wqn_ref, wqp_ref,
     cos_ref, sin_ref) = refs[:13]
    outs = refs[13:]
    if emit_vn:
        sgu_ref, vn_ref, qn_ref, qp_ref, ckv_ref, ckvb_ref, kpe_ref, kpeb_ref = outs
    else:
        sgu_ref, qn_ref, qp_ref, ckv_ref, ckvb_ref, kpe_ref, kpeb_ref = outs
    h = _rms(x_ref[...], g_ref[...]).astype(BF16)
    o_v, o_q, o_kv, o_pe = SGU_DIM, 2 * SGU_DIM, 2 * SGU_DIM + Q_LORA, 2 * SGU_DIM + Q_LORA + KV_LORA
    cos = cos_ref[...]
    sin = sin_ref[...]
    kpe = _rope_block(_dot(h, w_ref[:, o_pe:]), cos, sin)
    if kpe_transposed:
        kpe_ref[0] = kpe.T[:ROPE_DIM, :]
    else:
        kpe_ref[...] = kpe[:, :ROPE_DIM]
    kpeb_ref[...] = kpe.astype(BF16)
    ckv = _rms(_dot(h, w_ref[:, o_kv:o_pe]), kvg_ref[...])
    ckv_ref[...] = ckv
    ckvb_ref[...] = ckv.astype(BF16)
    cq = _rms(_dot(h, w_ref[:, o_q:o_kv]), qg_ref[...]).astype(BF16)
    for blk in range(MLA_HEADS * ROPE_DIM // LANES):
        bs = slice(blk * LANES, (blk + 1) * LANES)
        qp_ref[:, bs] = _rope_block(_dot(cq, wqp_ref[:, bs]), cos, sin).astype(BF16)
    qn_ref[...] = _dot(cq, wqn_ref[...]).astype(BF16)
    u = _dot(h, w_ref[:, :o_v])
    v = _dot(h, w_ref[:, o_v:o_q])
    mu = jnp.mean(v, axis=-1, keepdims=True)
    vc = v - mu
    var = jnp.mean(vc * vc, axis=-1, keepdims=True)
    vn = vc * lax.rsqrt(var + EPS) * lng_ref[...] + lnb_ref[...]
    if emit_vn:
        vn_ref[...] = vn
    vnb = vn.astype(BF16)
    rr = lax.broadcasted_iota(jnp.int32, (SGU_CHUNK, SGU_CHUNK), 0)
    cc = lax.broadcasted_iota(jnp.int32, (SGU_CHUNK, SGU_CHUNK), 1)
    causal = (rr // gate_len == cc // gate_len) & (cc <= rr)
    for g in range(SGU_GROUPS):
        gs = slice(g * SGU_GROUP_DIM, (g + 1) * SGU_GROUP_DIM)
        wg = jnp.where(causal, ws_ref[g], 0.0).astype(BF16)
        for c in range(tm // SGU_CHUNK):
            rs = slice(c * SGU_CHUNK, (c + 1) * SGU_CHUNK)
            s = _dot(wg, vnb[rs, gs]) + bs_ref[:, gs]
            sgu_ref[rs, gs] = (u[rs, gs] * s).astype(BF16)


def _proj_odd(x, g, w_in, ln_g, ln_b, w_s, b_s, qg, kvg, wqn, wqp, cos, sin, *, batch, seq_len, tm, gate_len,
              emit_vn):
    n = x.shape[0]
    pos_tiles = cos.shape[0] // tm
    kpe_transposed = tm % seq_len != 0
    if kpe_transposed:
        seq_tiles = seq_len // tm
        kpe_shape = (batch, ROPE_DIM, seq_len)
        kpe_spec = pl.BlockSpec((1, ROPE_DIM, tm), lambda i: (i // seq_tiles, 0, i % seq_tiles))
    else:
        kpe_shape, kpe_spec = (n, ROPE_DIM), pl.BlockSpec((tm, ROPE_DIM), lambda i: (i, 0))
    row_spec = lambda w: pl.BlockSpec((tm, w), lambda i: (i, 0))
    tab_spec = pl.BlockSpec((tm, LANES), lambda i: (i % pos_tiles, 0))
    consts = [g, w_in, ln_g, ln_b, w_s, b_s, qg, kvg, wqn, wqp]
    out_shape = [jax.ShapeDtypeStruct((n, SGU_DIM), BF16)]
    out_specs = [row_spec(SGU_DIM)]
    if emit_vn:
        out_shape.append(jax.ShapeDtypeStruct((n, SGU_DIM), F32))
        out_specs.append(row_spec(SGU_DIM))
    out_shape += [jax.ShapeDtypeStruct((n, MLA_HEADS * NOPE_DIM), BF16),
                  jax.ShapeDtypeStruct((n, MLA_HEADS * ROPE_DIM), BF16),
                  jax.ShapeDtypeStruct((n, KV_LORA), F32), jax.ShapeDtypeStruct((n, KV_LORA), BF16),
                  jax.ShapeDtypeStruct(kpe_shape, F32), jax.ShapeDtypeStruct((n, LANES), BF16)]
    out_specs += [row_spec(MLA_HEADS * NOPE_DIM), row_spec(MLA_HEADS * ROPE_DIM),
                  row_spec(KV_LORA), row_spec(KV_LORA), kpe_spec, row_spec(LANES)]
    return pl.pallas_call(
        functools.partial(_proj_odd_body, tm=tm, gate_len=gate_len, emit_vn=emit_vn, kpe_transposed=kpe_transposed),
        grid=(n // tm,),
        in_specs=[row_spec(D_MODEL)] + [_const_spec(c.shape) for c in consts] + [tab_spec, tab_spec],
        out_specs=out_specs, out_shape=out_shape,
        compiler_params=_cparams(("arbitrary",)), name="proj_odd")(x, *consts, cos, sin)


def _mla_queries(qn, qpe, wuk_ref, qcat_ref, tq):
    lane = lax.broadcasted_iota(jnp.int32, (tq, LANES), 1)
    low = lane < NOPE_DIM
    qpe = qpe.astype(F32)
    qcat_ref[:, KV_LORA:] = jnp.zeros((MLA_HEADS * tq, MLA_QK - KV_LORA), BF16)
    for hd in range(MLA_HEADS):
        p = hd // 2
        rs = slice(hd * tq, (hd + 1) * tq)
        pair = qn[:, p * LANES:(p + 1) * LANES]
        qm = jnp.where(low if hd % 2 == 0 else jnp.logical_not(low), pair, jnp.zeros_like(pair))
        qcat_ref[rs, :KV_LORA] = _dot(qm, wuk_ref[p]).astype(BF16)
        qcat_ref[rs, KV_LORA:KV_LORA + ROPE_DIM] = qpe[:, hd * ROPE_DIM:(hd + 1) * ROPE_DIM].astype(BF16)


def _lane_tile(x, width):
    return jnp.concatenate([x] * (width // LANES), axis=1)


def _mla_scores(qcat_ref, kcat, sc):
    sc[0][...] = _dot_nt(qcat_ref[...], kcat)


def _mla_softmax(sc, visible_fn):
    s_ref, p_ref, m_ref, l_ref, acc_ref = sc
    rows, tk = s_ref.shape
    for c in range(rows // MLA_ROW_CHUNK):
        rs = slice(c * MLA_ROW_CHUNK, (c + 1) * MLA_ROW_CHUNK)
        s = s_ref[rs, :] * (MLA_SCALE * LOG2_E)
        if visible_fn is not None:
            s = jnp.where(visible_fn(c), s, NEG_INF)
        m_prev = m_ref[rs, :]
        m_new = jnp.maximum(m_prev, jnp.max(s, axis=-1, keepdims=True))
        alpha = jnp.exp2(m_prev - m_new)
        p = jnp.exp2(s - _lane_tile(m_new, tk))
        l_ref[rs, :] = alpha * l_ref[rs, :] + jnp.sum(p, axis=-1, keepdims=True)
        m_ref[rs, :] = m_new
        p_ref[rs, :] = p.astype(BF16)
        acc_ref[rs, :] = acc_ref[rs, :] * _lane_tile(alpha, KV_LORA)


def _mla_pv(ck, sc):
    sc[4][...] += _dot(sc[1][...], ck)


def _mla_tile(qcat_ref, kcat, sc, visible_fn):
    _mla_scores(qcat_ref, kcat, sc)
    _mla_softmax(sc, visible_fn)
    _mla_pv(kcat[:, :KV_LORA], sc)


def _mla_init(sc):
    m_ref, l_ref, acc_ref = sc[2], sc[3], sc[4]
    m_ref[...] = jnp.full_like(m_ref, -jnp.inf)
    l_ref[...] = jnp.zeros_like(l_ref)
    acc_ref[...] = jnp.zeros_like(acc_ref)


def _mla_finish(sc, wuv_ref, o_ref, tq):
    l_ref, acc_ref = sc[3], sc[4]
    o_lat = (acc_ref[...] / _lane_tile(l_ref[...], KV_LORA)).astype(BF16)
    for p in range(MLA_HEADS // 2):
        h0, h1 = 2 * p, 2 * p + 1
        o_ref[:, p * LANES:(p + 1) * LANES] = (
            _dot(o_lat[h0 * tq:(h0 + 1) * tq], wuv_ref[h0]) + _dot(o_lat[h1 * tq:(h1 + 1) * tq], wuv_ref[h1])
        ).astype(BF16)


def _mla_prompt_body(qn_ref, qp_ref, ckv_ref, kpe_ref, wuk_ref, wuv_ref, o_ref, qcat_ref, *sc, tq, tk):
    qi = pl.program_id(1)
    _mla_queries(qn_ref[...], qp_ref[...], wuk_ref, qcat_ref, tq)
    _mla_init(sc)

    n_full = (qi * tq) // tk
    col = lax.broadcasted_iota(jnp.int32, (MLA_ROW_CHUNK, tk), 1)

    def visible(c):
        q_chunk_end = qi * tq + ((c * MLA_ROW_CHUNK) % tq) // CHUNK * CHUNK + CHUNK
        return col < q_chunk_end - n_full * tk

    def keys(kj):
        start = pl.multiple_of(kj * tk, tk)
        return jnp.concatenate([ckv_ref[pl.ds(start, tk), :], kpe_ref[pl.ds(start, tk), :],
                                jnp.zeros((tk, MLA_QK - KV_LORA - LANES), BF16)], axis=1)

    def body(kj, c):
        _mla_tile(qcat_ref, keys(kj), sc, None)
        return c

    lax.fori_loop(0, n_full, body, 0)
    _mla_tile(qcat_ref, keys(n_full), sc, visible)
    _mla_finish(sc, wuv_ref, o_ref, tq)


def _mla_scratch(rows, tk):
    return [pltpu.VMEM((rows, MLA_QK), BF16), pltpu.VMEM((rows, tk), F32), pltpu.VMEM((rows, tk), BF16),
            pltpu.VMEM((rows, LANES), F32), pltpu.VMEM((rows, LANES), F32), pltpu.VMEM((rows, KV_LORA), F32)]


def _mla_prompt(qn, qp, ckvb, kpeb, wuk, wuv, *, batch, seq_len, tq, tk):
    assert tq % CHUNK == 0 and CHUNK % MLA_ROW_CHUNK == 0 and seq_len % tq == 0 and seq_len % tk == 0
    n = qn.shape[0]
    nq = seq_len // tq
    width = MLA_HEADS * V_DIM
    return pl.pallas_call(
        functools.partial(_mla_prompt_body, tq=tq, tk=tk),
        grid=(batch, nq),
        in_specs=[pl.BlockSpec((tq, MLA_HEADS * NOPE_DIM), lambda b, i: (b * nq + i, 0)),
                  pl.BlockSpec((tq, MLA_HEADS * ROPE_DIM), lambda b, i: (b * nq + i, 0)),
                  pl.BlockSpec((seq_len, KV_LORA), lambda b, i: (b, 0)),
                  pl.BlockSpec((seq_len, LANES), lambda b, i: (b, 0)),
                  _const_spec(wuk.shape), _const_spec(wuv.shape)],
        out_specs=pl.BlockSpec((tq, width), lambda b, i: (b * nq + i, 0)),
        out_shape=jax.ShapeDtypeStruct((n, width), BF16),
        scratch_shapes=_mla_scratch(MLA_HEADS * tq, tk),
        compiler_params=_cparams(("arbitrary", "arbitrary")), name="mla_attn_prompt")(qn, qp, ckvb, kpeb, wuk, wuv)


def _mla_sample_body(qn_ref, qp_ref, cn_ref, pn_ref, cc_ref, pc_ref, wuk_ref, wuv_ref, o_ref, kcat_ref, qcat_ref,
                     *sc, dec, past, kblk, tk):
    j = pl.program_id(1)
    rows = MLA_HEADS * dec

    @pl.when(j == 0)
    def _():
        _mla_queries(qn_ref[...], qp_ref[...], wuk_ref, qcat_ref, dec)
        _mla_init(sc)
        kcat_ref[:, KV_LORA:] = jnp.zeros((tk, MLA_QK - KV_LORA), BF16)
        new = jnp.concatenate([cn_ref[...], pn_ref[...], jnp.zeros((dec, MLA_QK - KV_LORA - LANES), BF16)], axis=1)
        kcat = jnp.concatenate([new, jnp.zeros((tk - dec, MLA_QK), BF16)], axis=0)

        def visible(c):
            col = lax.broadcasted_iota(jnp.int32, (MLA_ROW_CHUNK, tk), 1)
            row = c * MLA_ROW_CHUNK + lax.broadcasted_iota(jnp.int32, (MLA_ROW_CHUNK, tk), 0)
            return (col < dec) & ((past + col) // CHUNK <= (past + row % dec) // CHUNK)

        _mla_tile(qcat_ref, kcat, sc, visible)

    def body(it, c):
        start = pl.multiple_of(it * tk, tk)
        kcat_ref[:, :KV_LORA] = cc_ref[0, pl.ds(start, tk), :].astype(BF16)
        kp_t = jnp.concatenate([pc_ref[0, :, pl.ds(start, tk)], jnp.zeros((LANES - ROPE_DIM, tk), F32)], axis=0)
        kcat_ref[:, KV_LORA:KV_LORA + LANES] = kp_t.T.astype(BF16)
        _mla_tile(qcat_ref, kcat_ref[...], sc, None)
        return c

    lax.fori_loop(0, kblk // tk, body, 0)

    @pl.when(j == pl.num_programs(1) - 1)
    def _():
        _mla_finish(sc, wuv_ref, o_ref, dec)


def _mla_sample(qn, qp, ckvb, kpeb, cache_ckv, cache_kpe_t, wuk, wuv, *, batch, dec, kblk=1024, tk=1024):
    n = qn.shape[0]
    past = cache_ckv.shape[1]
    width = MLA_HEADS * V_DIM
    return pl.pallas_call(
        functools.partial(_mla_sample_body, dec=dec, past=past, kblk=kblk, tk=tk),
        grid=(batch, past // kblk),
        in_specs=[pl.BlockSpec((dec, MLA_HEADS * NOPE_DIM), lambda b, j: (b, 0)),
                  pl.BlockSpec((dec, MLA_HEADS * ROPE_DIM), lambda b, j: (b, 0)),
                  pl.BlockSpec((dec, KV_LORA), lambda b, j: (b, 0)),
                  pl.BlockSpec((dec, LANES), lambda b, j: (b, 0)),
                  pl.BlockSpec((1, kblk, KV_LORA), lambda b, j: (b, j, 0)),
                  pl.BlockSpec((1, ROPE_DIM, kblk), lambda b, j: (b, 0, j)),
                  _const_spec(wuk.shape), _const_spec(wuv.shape)],
        out_specs=pl.BlockSpec((dec, width), lambda b, j: (b, 0)),
        out_shape=jax.ShapeDtypeStruct((n, width), BF16),
        scratch_shapes=[pltpu.VMEM((tk, MLA_QK), BF16)] + _mla_scratch(MLA_HEADS * dec, tk),
        compiler_params=_cparams(("arbitrary", "arbitrary")), name="mla_attn_sample")(
            qn, qp, ckvb, kpeb, cache_ckv, cache_kpe_t, wuk, wuv)


def _out_ffn_body(a_ref, b_ref, x_ref, woa_ref, wob_ref, gpost_ref, gpre_ref, wup_ref, wdn_ref, gfpost_ref, o_ref):
    tm = x_ref.shape[0]
    groups = [slice(r * tm // FFN_ROW_GROUPS, (r + 1) * tm // FFN_ROW_GROUPS) for r in range(FFN_ROW_GROUPS)]
    mixed = [_dot(a_ref[rs, :], woa_ref[...]) + _dot(b_ref[rs, :], wob_ref[...]) for rs in groups]
    x1 = [x_ref[rs, :] + _rms(m, gpost_ref[...]) for rs, m in zip(groups, mixed)]
    h = [_rms(x, gpre_ref[...]).astype(BF16) for x in x1]
    down = [jnp.zeros_like(x) for x in x1]
    for c in range(D_FF // FF_CHUNK):
        for r in range(FFN_ROW_GROUPS):
            up = _dot(h[r], wup_ref[:, c * FF_CHUNK:(c + 1) * FF_CHUNK])
            act = jnp.square(jnp.maximum(up, 0.0)).astype(BF16)
            down[r] = down[r] + _dot(act, wdn_ref[c * FF_CHUNK:(c + 1) * FF_CHUNK, :])
    for r, rs in enumerate(groups):
        o_ref[rs, :] = x1[r] + _rms(down[r], gfpost_ref[...])


def _out_ffn(a, b, x, woa, wob, g_post, g_pre, w_up, w_down, g_fpost, *, tm):
    n = x.shape[0]
    row_spec = lambda w: pl.BlockSpec((tm, w), lambda i: (i, 0))
    consts = [woa, wob, g_post, g_pre, w_up, w_down, g_fpost]
    return pl.pallas_call(
        _out_ffn_body,
        grid=(n // tm,),
        in_specs=[row_spec(a.shape[1]), row_spec(b.shape[1]), row_spec(D_MODEL)] + [_const_spec(c.shape) for c in consts],
        out_specs=row_spec(D_MODEL),
        out_shape=jax.ShapeDtypeStruct((n, D_MODEL), F32),
        compiler_params=_cparams(("arbitrary",)), name="out_ffn")(a, b, x, *consts)


def _rope_tables(pos, reps):
    half = ROPE_DIM // 2
    inv = ROPE_THETA ** (-jnp.arange(half, dtype=F32) / half)
    ang = pos.astype(F32)[:, None] * inv[None, :]
    cos = jnp.tile(jnp.concatenate([jnp.cos(ang), jnp.cos(ang)], axis=1), (reps, LANES // ROPE_DIM))
    sin = jnp.tile(jnp.concatenate([-jnp.sin(ang), jnp.sin(ang)], axis=1), (reps, LANES // ROPE_DIM))
    return cos, sin


def _prep_even(p, j):
    w_in = p["even_w_in"][j]
    return dict(w_in=w_in.astype(BF16), w_kv_t=w_in[:, SB_WIDTH:3 * SB_WIDTH].T.astype(BF16), w_conv=p["even_w_conv"][j],
                woa=p["even_w_out"][j, :SB_WIDTH].astype(BF16), wob=p["even_w_out"][j, SB_WIDTH:].astype(BF16))


def _prep_odd(p, j, gate_len):
    w_in = p["odd_w_in"][j]
    w_in = jnp.pad(w_in, ((0, 0), (0, ODD_IN_PAD - w_in.shape[1]))).astype(BF16)
    reps = SGU_CHUNK // gate_len
    w_s = jnp.tile(p["sgu_w_s"][j, :, :gate_len, :gate_len], (1, reps, reps))
    b_s = jnp.tile(p["sgu_b_s"][j, :, :gate_len], (1, reps))
    b_s = jnp.repeat(b_s.T, SGU_GROUP_DIM, axis=1)
    w_uq = p["mla_w_uq"][j].reshape(Q_LORA, MLA_HEADS, NOPE_DIM + ROPE_DIM)
    wqn = w_uq[:, :, :NOPE_DIM].reshape(Q_LORA, MLA_HEADS * NOPE_DIM).astype(BF16)
    wqp = w_uq[:, :, NOPE_DIM:].reshape(Q_LORA, MLA_HEADS * ROPE_DIM).astype(BF16)
    wuk = p["mla_w_uk"][j].reshape(MLA_HEADS // 2, 2 * NOPE_DIM, KV_LORA).astype(BF16)
    w_uv = p["mla_w_uv"][j]
    wuv = jnp.stack([jnp.pad(w_uv[h], ((0, 0), ((h % 2) * V_DIM, (1 - h % 2) * V_DIM))) for h in range(MLA_HEADS)])
    return dict(w_in=w_in, ln_g=p["sgu_ln_g"][j][None], ln_b=p["sgu_ln_b"][j][None], w_s=w_s, b_s=b_s,
                qg=p["mla_q_norm_g"][j][None], kvg=p["mla_kv_norm_g"][j][None], wqn=wqn, wqp=wqp, wuk=wuk,
                wuv=wuv.astype(BF16),
                woa=p["odd_w_out"][j, :SGU_DIM].astype(BF16), wob=p["odd_w_out"][j, SGU_DIM:].astype(BF16))


def _run_trunk(x, pos, past, p, *, batch, seq_len):
    depth = p["mix_pre_g"].shape[0]
    n = batch * seq_len
    x = x.reshape(n, D_MODEL)
    is_sample = past is not None
    tm = min(ROW_TILE, n)
    assert n % tm == 0 and (tm % seq_len == 0 or seq_len % tm == 0) and tm % SGU_CHUNK == 0
    gate_len = min(seq_len, SGU_CHUNK)
    tm_proj = PROJ_ROW_TILE if (seq_len % PROJ_ROW_TILE == 0 and seq_len > PROJ_ROW_TILE) else tm
    cos, sin = _rope_tables(pos, max(1, tm_proj // seq_len))
    st = {k: [] for k in ("sb_k", "sb_v", "conv", "ckv", "kpe", "sgu_v")}
    for layer in range(depth):
        j = layer // 2
        g_pre = p["mix_pre_g"][layer][None]
        if layer % 2 == 0:
            w = _prep_even(p, j)
            conv_prev = past["conv"][j] if is_sample else None
            q, k, v, kb, vb, b_mix, conv_state = _proj_even(x, g_pre, w["w_in"], w["w_kv_t"], w["w_conv"], conv_prev,
                                                            batch=batch, seq_len=seq_len, tm=tm_proj)
            if is_sample:
                assert tm % seq_len == 0

                def cache_t(c):
                    return jnp.transpose(c, (0, 2, 3, 1)).reshape(batch, SB_WIDTH, -1)

                a_mix = _sb_sample(q, kb, vb, cache_t(past["sb_k"][j]), cache_t(past["sb_v"][j]),
                                   batch=batch, dec=seq_len)
                k, v = (t.reshape(batch, seq_len, SB_HEADS, SB_HEAD_DIM) for t in (k, v))
            else:
                assert seq_len % tm == 0
                a_mix = _sb_prompt(q, kb, vb, batch=batch, seq_len=seq_len, tq=SB_TQ, tk=SB_TK)
                k, v = (jnp.transpose(t.reshape(batch, SB_HEADS, SB_HEAD_DIM, seq_len), (0, 3, 1, 2)) for t in (k, v))
            st["sb_k"].append(k)
            st["sb_v"].append(v)
            st["conv"].append(conv_state)
        else:
            w = _prep_odd(p, j, gate_len)
            outs = _proj_odd(x, g_pre, w["w_in"], w["ln_g"], w["ln_b"], w["w_s"], w["b_s"], w["qg"], w["kvg"],
                             w["wqn"], w["wqp"], cos, sin, batch=batch, seq_len=seq_len, tm=tm_proj, gate_len=gate_len,
                             emit_vn=is_sample)
            if is_sample:
                a_mix, vn, qn, qp, ckv, ckvb, kpe, kpeb = outs
                st["sgu_v"].append(vn.reshape(batch, seq_len, SGU_DIM))
                b_mix = _mla_sample(qn, qp, ckvb, kpeb, past["ckv"][j], jnp.transpose(past["kpe"][j], (0, 2, 1)),
                                    w["wuk"], w["wuv"],
                                    batch=batch, dec=seq_len)
            else:
                a_mix, qn, qp, ckv, ckvb, kpe, kpeb = outs
                b_mix = _mla_prompt(qn, qp, ckvb, kpeb, w["wuk"], w["wuv"], batch=batch, seq_len=seq_len,
                                    tq=MLA_TQ, tk=MLA_TK)
            st["ckv"].append(ckv.reshape(batch, seq_len, KV_LORA))
            st["kpe"].append(jnp.transpose(kpe, (0, 2, 1)) if kpe.ndim == 3 else kpe.reshape(batch, seq_len, ROPE_DIM))
        x = _out_ffn(a_mix, b_mix, x, w["woa"], w["wob"], p["mix_post_g"][layer][None], p["ffn_pre_g"][layer][None],
                     p["ffn_w_up"][layer].astype(BF16), p["ffn_w_down"][layer].astype(BF16),
                     p["ffn_post_g"][layer][None], tm=tm)
    states = {k: jnp.stack(v) for k, v in st.items() if v}
    return x.reshape(batch, seq_len, D_MODEL), states


def kernel(x_prompt, x_sample, cache_sb_k, cache_sb_v, state_conv, cache_mla_ckv, cache_mla_kpe,
           mix_pre_g, mix_post_g, ffn_pre_g, ffn_post_g, even_w_in, even_w_conv, even_w_out,
           odd_w_in, sgu_ln_g, sgu_ln_b, sgu_w_s, sgu_b_s, mla_q_norm_g, mla_kv_norm_g,
           mla_w_uq, mla_w_uk, mla_w_uv, odd_w_out, ffn_w_up, ffn_w_down):
    params = {
        "mix_pre_g": mix_pre_g, "mix_post_g": mix_post_g, "ffn_pre_g": ffn_pre_g, "ffn_post_g": ffn_post_g,
        "even_w_in": even_w_in, "even_w_conv": even_w_conv, "even_w_out": even_w_out,
        "odd_w_in": odd_w_in, "sgu_ln_g": sgu_ln_g, "sgu_ln_b": sgu_ln_b, "sgu_w_s": sgu_w_s,
        "sgu_b_s": sgu_b_s, "mla_q_norm_g": mla_q_norm_g, "mla_kv_norm_g": mla_kv_norm_g,
        "mla_w_uq": mla_w_uq, "mla_w_uk": mla_w_uk, "mla_w_uv": mla_w_uv, "odd_w_out": odd_w_out,
        "ffn_w_up": ffn_w_up, "ffn_w_down": ffn_w_down,
    }
    batch, seq_len, _ = x_prompt.shape
    pos_p = jnp.arange(seq_len, dtype=jnp.int32)
    y_prompt, st_p = _run_trunk(x_prompt, pos_p, None, params, batch=batch, seq_len=seq_len)
    dec_batch, dec_seq, _ = x_sample.shape
    past_len = cache_sb_k.shape[2]
    pos_s = past_len + jnp.arange(dec_seq, dtype=jnp.int32)
    past = {"sb_k": cache_sb_k, "sb_v": cache_sb_v, "conv": state_conv, "ckv": cache_mla_ckv, "kpe": cache_mla_kpe}
    y_sample, st_s = _run_trunk(x_sample, pos_s, past, params, batch=dec_batch, seq_len=dec_seq)
    return (y_prompt, y_sample,
            st_p["sb_k"], st_p["sb_v"], st_p["conv"], st_p["ckv"], st_p["kpe"],
            st_s["sb_k"], st_s["sb_v"], st_s["conv"], st_s["ckv"], st_s["kpe"], st_s["sgu_v"])
```

```python
import functools
import math

import numpy as np
import jax
import jax.numpy as jnp
from jax import lax
from jax.experimental import pallas as pl
from jax.experimental.pallas import tpu as pltpu

F32 = jnp.float32
BF16 = jnp.bfloat16

EPS = 1e-6
D_MODEL = 1024
CHUNK = 64
SB_HEADS = 8
SB_HEAD_DIM = 64
SB_WIDTH = SB_HEADS * SB_HEAD_DIM
SB_SCALE = 1.0 / math.sqrt(SB_HEAD_DIM)
CONV_DIM = D_MODEL // 2
CONV_W = 3
SGU_CHUNK = 128
SGU_GROUPS = 4
SGU_DIM = D_MODEL // 2
SGU_GROUP_DIM = SGU_DIM // SGU_GROUPS
MLA_HEADS = 8
Q_LORA = 384
KV_LORA = 256
NOPE_DIM = 64
ROPE_DIM = 32
V_DIM = 64
ROPE_THETA = 10000.0
MLA_SCALE = 1.0 / math.sqrt(NOPE_DIM + ROPE_DIM)
D_FF = 4 * D_MODEL
FF_CHUNK = 1024
FFN_ROW_GROUPS = 2

LANES = 128
HEAD_PAIRS = SB_HEADS // 2
MLA_QK = 2 * KV_LORA
ODD_IN_PAD =2 * SGU_DIM + Q_LORA + KV_LORA + LANES
VMEM_LIMIT = 56 * 1024 * 1024
NEG_INF = -1e30
LOG2_E = math.log2(math.e)

ROW_TILE = 512
PROJ_ROW_TILE = 1024
SB_TQ, SB_TK = 1024, 128
MLA_TQ, MLA_TK = 256, 256
MLA_ROW_CHUNK = 64


def _cparams(sem):
    return pltpu.CompilerParams(dimension_semantics=sem, vmem_limit_bytes=VMEM_LIMIT)


def _const_spec(shape):
    nd = len(shape)
    return pl.BlockSpec(shape, lambda *_: (0,) * nd, pipeline_mode=pl.Buffered(1))


def _rms(x, g):
    return x * lax.rsqrt(jnp.mean(x * x, axis=-1, keepdims=True) + EPS) * g


def _dot(a, b):
    return jnp.dot(a, b, preferred_element_type=F32)


def _dot_nt(a, b):
    return lax.dot_general(a, b, (((1,), (1,)), ((), ())), preferred_element_type=F32)


def _log_sigmoid(z):
    neg_abs = lax.bitcast_convert_type(lax.bitcast_convert_type(z, jnp.uint32) | jnp.uint32(0x80000000), F32)
    return jnp.minimum(z, 0.0) - jnp.log(1.0 + jnp.exp(neg_abs))


def _split_hi_lo(x):
    hi = x.astype(BF16)
    lo = (x - hi.astype(F32)).astype(BF16)
    return hi, lo


def _rope_block(x, cos, sin):
    half = ROPE_DIM // 2
    lane = lax.broadcasted_iota(jnp.int32, x.shape, 1)
    partner = jnp.where(lane % ROPE_DIM < half, pltpu.roll(x, LANES - half, 1), pltpu.roll(x, half, 1))
    return x * cos + partner * sin


def _proj_even_body(*refs, tm, seq_len, whole_seqs):
    if whole_seqs:
        (x_ref, g_ref, w_ref, wc_ref, e1_ref, e2_ref,
         q_ref, k_ref, v_ref, kb_ref, vb_ref, gc_ref, ci_ref) = refs
    else:
        (x_ref, g_ref, w_ref, wc_ref, wkv_ref,
         q_ref, k_ref, v_ref, kb_ref, vb_ref, gc_ref, cs_ref, tail_ref) = refs
    if not whole_seqs:
        @pl.when((pl.program_id(0) % (seq_len // tm)) == 0)
        def _():
            tail_ref[...] = jnp.zeros_like(tail_ref)

    h = _rms(x_ref[...], g_ref[...]).astype(BF16)

    def proj(j):
        return _dot(h, w_ref[:, j * SB_WIDTH:(j + 1) * SB_WIDTH])

    g_post = proj(3)
    ci = proj(4) * proj(5)
    r1 = pltpu.roll(ci, 1, 0)
    r2 = pltpu.roll(ci, 2, 0)
    row = lax.broadcasted_iota(jnp.int32, (tm, 1), 0)
    if whole_seqs:
        tpos = row % seq_len
        s1 = jnp.where(tpos < 1, e1_ref[...], r1)
        s2 = jnp.where(tpos < 2, e2_ref[...], r2)
        ci_ref[...] = ci
    else:
        t1 = tail_ref[7:8, :]
        t2 = tail_ref[6:7, :]
        s1 = jnp.where(row == 0, t1, r1)
        s2 = jnp.where(row == 0, t2, jnp.where(row == 1, t1, r2))
        tail_ref[...] = ci[tm - 8:, :]
        cs_ref[0] = ci[tm - (CONV_W - 1):, :]
    conv = wc_ref[0:1, :] * s2 + wc_ref[1:2, :] * s1 + wc_ref[2:3, :] * ci
    gc_ref[...] = (g_post * conv).astype(BF16)

    q_ref[...] = (proj(0) * SB_SCALE).astype(BF16)
    if whole_seqs:
        k = proj(1)
        v = proj(2)
        k_ref[...] = k
        v_ref[...] = v
        kb_ref[...] = k.astype(BF16)
        vb_ref[...] = v.astype(BF16)
    else:
        k = _dot_nt(wkv_ref[:SB_WIDTH, :], h)
        v = _dot_nt(wkv_ref[SB_WIDTH:, :], h)
        k_ref[0] = k
        v_ref[0] = v
        kb_ref[0] = k.astype(BF16)
        vb_ref[0] = v.astype(BF16)


def _proj_even(x, g, w_in, w_kv_t, w_conv, conv_prev, *, batch, seq_len, tm):
    n = x.shape[0]
    whole = tm % seq_len == 0
    grid = (n // tm,)
    row_spec = lambda w: pl.BlockSpec((tm, w), lambda i: (i, 0))
    in_specs = [row_spec(D_MODEL), _const_spec((1, D_MODEL)), _const_spec(w_in.shape), _const_spec(w_conv.shape)]
    args = [x, g, w_in, w_conv]
    if whole:
        kv_shape, kv_spec = (n, SB_WIDTH), row_spec(SB_WIDTH)
    else:
        seq_tiles = seq_len // tm
        kv_shape = (batch, SB_WIDTH, seq_len)
        kv_spec = pl.BlockSpec((1, SB_WIDTH, tm), lambda i: (i // seq_tiles, 0, i % seq_tiles))
    out_shape = [jax.ShapeDtypeStruct((n, SB_WIDTH), BF16), jax.ShapeDtypeStruct(kv_shape, F32),
                 jax.ShapeDtypeStruct(kv_shape, F32), jax.ShapeDtypeStruct(kv_shape, BF16),
                 jax.ShapeDtypeStruct(kv_shape, BF16), jax.ShapeDtypeStruct((n, CONV_DIM), BF16)]
    out_specs = [row_spec(SB_WIDTH)] + [kv_spec] * 4 + [row_spec(CONV_DIM)]
    scratch = []
    if whole:
        if conv_prev is None:
            conv_prev = jnp.zeros((batch, CONV_W - 1, CONV_DIM), F32)
        e1 =jnp.zeros((batch, seq_len, CONV_DIM), F32).at[:, 0].set(conv_prev[:, 1])
        e2 = jnp.zeros((batch, seq_len, CONV_DIM), F32).at[:, 0].set(conv_prev[:, 0]).at[:, 1].set(conv_prev[:, 1])
        args += [e1.reshape(n, CONV_DIM), e2.reshape(n, CONV_DIM)]
        in_specs += [row_spec(CONV_DIM), row_spec(CONV_DIM)]
        out_shape.append(jax.ShapeDtypeStruct((n, CONV_DIM), F32))
        out_specs.append(row_spec(CONV_DIM))
    else:
        assert seq_len % tm == 0 and conv_prev is None
        args.append(w_kv_t)
        in_specs.append(_const_spec(w_kv_t.shape))
        out_shape.append(jax.ShapeDtypeStruct((batch, CONV_W - 1, CONV_DIM), F32))
        out_specs.append(pl.BlockSpec((1, CONV_W - 1, CONV_DIM), lambda i: (i // seq_tiles, 0, 0)))
        scratch.append(pltpu.VMEM((8, CONV_DIM), F32))
    outs = pl.pallas_call(
        functools.partial(_proj_even_body, tm=tm, seq_len=seq_len, whole_seqs=whole),
        grid=grid, in_specs=in_specs, out_specs=out_specs, out_shape=out_shape, scratch_shapes=scratch,
        compiler_params=_cparams(("arbitrary",)), name="proj_even")(*args)
    q, k, v, kb, vb, gc, last = outs
    if whole:
        last = last.reshape(batch, seq_len, CONV_DIM)[:, seq_len - (CONV_W - 1):]
    return q, k, v, kb, vb, gc, last


def _suffix_rhs(tk):
    j = np.arange(2 * tk)[:, None] % tk
    c = np.arange(2 * tk)[None, :]
    return jnp.asarray(np.where(c < tk, j > c, True), dtype=BF16)


def _sb_prompt_body(q_ref, k_ref, v_ref, r_ref, o_ref, carry_ref, acc_ref, *, tq, tk):
    qi = pl.program_id(1)
    ratio = tq // tk
    even = lax.broadcasted_iota(jnp.int32, (LANES, tk), 0) < SB_HEAD_DIM
    rhs = r_ref[...]
    carry_ref[...] = jnp.zeros_like(carry_ref)
    acc_ref[...] = jnp.zeros_like(acc_ref)
    qs = [q_ref[:, p * LANES:(p + 1) * LANES] for p in range(HEAD_PAIRS)]

    def block_diag(blk):
        zero = jnp.zeros_like(blk)
        return jnp.concatenate([jnp.where(even, blk, zero), jnp.where(even, zero, blk)], axis=1)

    def tile(kj, masked, r0=0):
        start = pl.multiple_of(kj * tk, tk)
        if masked:
            q_pos = qi * tq + r0 + lax.broadcasted_iota(jnp.int32, (tq - r0, tk), 0)
            k_pos = kj * tk + lax.broadcasted_iota(jnp.int32, (tq - r0, tk), 1)
            visible = k_pos < q_pos
        zs = [_dot(qs[p][r0:], block_diag(k_ref[0, p * LANES:(p + 1) * LANES, pl.ds(start, tk)]))
              for p in range(HEAD_PAIRS)]
        log_betas, splits = [], []
        for hd in range(SB_HEADS):
            z = zs[hd // 2][:, (hd % 2) * tk:(hd % 2 + 1) * tk]
            log_beta = _log_sigmoid(z)
            log_1m = log_beta - z
            if masked:
                log_1m = jnp.where(visible, log_1m, 0.0)
            hi, lo = _split_hi_lo(log_1m)
            log_betas.append(log_beta)
            splits.append(jnp.concatenate([hi, lo], axis=1))
        sums = [_dot(sp, rhs) for sp in splits]
        ws = []
        for hd in range(SB_HEADS):
            carry = carry_ref[hd, r0:, :]
            w = jnp.exp(log_betas[hd] + sums[hd][:, :tk] + carry)
            if masked:
                w = jnp.where(visible, w, 0.0)
            carry_ref[hd, r0:, :] = carry + sums[hd][:, tk:]
            ws.append(w.astype(BF16))
        for p in range(HEAD_PAIRS):
            vbd = block_diag(v_ref[0, p * LANES:(p + 1) * LANES, pl.ds(start, tk)])
            acc_ref[p, r0:, :] += _dot_nt(jnp.concatenate([ws[2 * p], ws[2 * p + 1]], axis=1), vbd)

    for t in reversed(range(ratio)):
        tile(qi * ratio + t, True, r0=t * tk)

    def body(it, c):
        tile(qi * ratio - 1 - 2 * it, False)
        tile(qi * ratio - 2 - 2 * it, False)
        return c

    lax.fori_loop(0, qi * (ratio // 2), body, 0)
    for p in range(HEAD_PAIRS):
        o_ref[:, p * LANES:(p + 1) * LANES] = acc_ref[p].astype(BF16)


def _sb_prompt(q, kb, vb, *, batch, seq_len, tq, tk):
    assert (tq // tk) % 2 == 0 and tq % tk == 0 and seq_len % tq == 0
    n = q.shape[0]
    nq = seq_len // tq
    rhs = _suffix_rhs(tk)
    q_spec = pl.BlockSpec((tq, SB_WIDTH), lambda b, i: (b * nq + i, 0))
    kv_spec = pl.BlockSpec((1, SB_WIDTH, seq_len), lambda b, i: (b, 0, 0))
    return pl.pallas_call(
        functools.partial(_sb_prompt_body, tq=tq, tk=tk),
        grid=(batch, nq),
        in_specs=[q_spec, kv_spec, kv_spec, _const_spec(rhs.shape)],
        out_specs=q_spec,
        out_shape=jax.ShapeDtypeStruct((n, SB_WIDTH), BF16),
        scratch_shapes=[pltpu.VMEM((SB_HEADS, tq, LANES), F32), pltpu.VMEM((HEAD_PAIRS, tq, LANES), F32)],
        compiler_params=_cparams(("arbitrary", "arbitrary")), name="sb_attn_prompt")(q, kb, vb, rhs)


def _sb_sample_body(q_ref, kn_ref, vn_ref, kc_ref, vc_ref, l_ref, o_ref, qh_ref, carry_ref, acc_ref,
                    *, dec, kblk, tk):
    j = pl.program_id(1)
    rows = SB_HEADS * dec
    rhs = l_ref[...]

    def tiles(kts, vts, visible):
        log_betas, sums = [], []
        for kt in kts:
            z = _dot(qh_ref[...], kt)
            log_beta = _log_sigmoid(z)
            log_1m = log_beta - z
            if visible is not None:
                log_1m = jnp.where(visible, log_1m, 0.0)
            hi, lo = _split_hi_lo(log_1m)
            log_betas.append(log_beta)
            sums.append(_dot(jnp.concatenate([hi, lo], axis=1), rhs))
        carry = carry_ref[...]
        acc = acc_ref[...]
        for log_beta, sm, vt in zip(log_betas, sums, vts):
            w = jnp.exp(log_beta + sm[:, :tk] + carry)
            if visible is not None:
                w = jnp.where(visible, w, 0.0)
            acc = acc + _dot_nt(w.astype(BF16), vt)
            carry = carry + sm[:, tk:]
        carry_ref[...] = carry
        acc_ref[...] = acc

    def new_keys_t(x):
        x = jnp.concatenate([x.astype(F32), jnp.zeros((tk - dec, SB_WIDTH), F32)], axis=0)
        return x.T.astype(BF16)

    @pl.when(j == 0)
    def _():
        q = q_ref[...]
        qt = jnp.concatenate([q] * SB_HEADS, axis=0)
        rr = lax.broadcasted_iota(jnp.int32, (rows, SB_WIDTH), 0)
        cc = lax.broadcasted_iota(jnp.int32, (rows, SB_WIDTH), 1)
        qh_ref[...] = jnp.where(rr // dec == cc // SB_HEAD_DIM, qt, jnp.zeros_like(qt))
        carry_ref[...] = jnp.zeros_like(carry_ref)
        acc_ref[...] = jnp.zeros_like(acc_ref)
        query_t = lax.broadcasted_iota(jnp.int32, (rows, tk), 0) % dec
        key_i = lax.broadcasted_iota(jnp.int32, (rows, tk), 1)
        tiles([new_keys_t(kn_ref[...])], [new_keys_t(vn_ref[...])],
              key_i < query_t)

    order = list(reversed(range(kblk // tk)))
    tiles([kc_ref[0, :, s * tk:(s + 1) * tk].astype(BF16) for s in order],
          [vc_ref[0, :, s * tk:(s + 1) * tk].astype(BF16) for s in order], None)

    @pl.when(j == pl.num_programs(1) - 1)
    def _():
        cc = lax.broadcasted_iota(jnp.int32, (dec, SB_WIDTH), 1)
        out = jnp.zeros((dec, SB_WIDTH), F32)
        for h in range(SB_HEADS):
            out = out + jnp.where(cc // SB_HEAD_DIM == h, acc_ref[h * dec:(h + 1) * dec, :], 0.0)
        o_ref[...] = out.astype(BF16)


def _sb_sample(q, kb, vb, cache_kt, cache_vt, *, batch, dec, kblk=1024, tk=256):
    n = q.shape[0]
    past = cache_kt.shape[2]
    nkb = past // kblk
    rows = SB_HEADS * dec
    rhs = _suffix_rhs(tk)
    new_spec = pl.BlockSpec((dec, SB_WIDTH), lambda b, j: (b, 0))
    cache_spec = pl.BlockSpec((1, SB_WIDTH, kblk), lambda b, j: (b, 0, nkb - 1 - j))
    return pl.pallas_call(
        functools.partial(_sb_sample_body, dec=dec, kblk=kblk, tk=tk),
        grid=(batch, nkb),
        in_specs=[new_spec, new_spec, new_spec, cache_spec, cache_spec, _const_spec(rhs.shape)],
        out_specs=new_spec,
        out_shape=jax.ShapeDtypeStruct((n, SB_WIDTH), BF16),
        scratch_shapes=[pltpu.VMEM((rows, SB_WIDTH), BF16), pltpu.VMEM((rows, tk), F32),
                        pltpu.VMEM((rows, SB_WIDTH), F32)],
        compiler_params=_cparams(("arbitrary", "arbitrary")), name="sb_attn_sample")(
            q, kb, vb, cache_kt, cache_vt, rhs)


def _proj_odd_body(*refs, tm, gate_len, emit_vn, kpe_transposed):
    (x_ref, g_ref, w_ref, lng_ref, lnb_ref, ws_ref, bs_ref, qg_ref, kvg_ref, wqn_ref, wqp_ref,
     cos_ref, sin_ref) = refs[:13]
    outs = refs[13:]
    if emit_vn:
        sgu_ref, vn_ref, qn_ref, qp_ref, ckv_ref, ckvb_ref, kpe_ref, kpeb_ref = outs
    else:
        sgu_ref, qn_ref, qp_ref, ckv_ref, ckvb_ref, kpe_ref, kpeb_ref = outs
    h = _rms(x_ref[...], g_ref[...]).astype(BF16)
    o_v, o_q, o_kv, o_pe = SGU_DIM, 2 * SGU_DIM, 2 * SGU_DIM + Q_LORA, 2 * SGU_DIM + Q_LORA + KV_LORA
    cos = cos_ref[...]
    sin = sin_ref[...]
    kpe = _rope_block(_dot(h, w_ref[:, o_pe:]), cos, sin)
    if kpe_transposed:
        kpe_ref[0] = kpe.T[:ROPE_DIM, :]
    else:
        kpe_ref[...] = kpe[:, :ROPE_DIM]
    kpeb_ref[...] = kpe.astype(BF16)
    ckv = _rms(_dot(h, w_ref[:, o_kv:o_pe]), kvg_ref[...])
    ckv_ref[...] = ckv
    ckvb_ref[...] = ckv.astype(BF16)
    cq = _rms(_dot(h, w_ref[:, o_q:o_kv]), qg_ref[...]).astype(BF16)
    for blk in range(MLA_HEADS * ROPE_DIM // LANES):
        bs = slice(blk * LANES, (blk + 1) * LANES)
        qp_ref[:, bs] = _rope_block(_dot(cq, wqp_ref[:, bs]), cos, sin).astype(BF16)
    qn_ref[...] = _dot(cq, wqn_ref[...]).astype(BF16)
    u = _dot(h, w_ref[:, :o_v])
    v = _dot(h, w_ref[:, o_v:o_q])
    mu = jnp.mean(v, axis=-1, keepdims=True)
    vc = v - mu
    var = jnp.mean(vc * vc, axis=-1, keepdims=True)
    vn = vc * lax.rsqrt(var + EPS) * lng_ref[...] + lnb_ref[...]
    if emit_vn:
        vn_ref[...] = vn
    vnb = vn.astype(BF16)
    rr = lax.broadcasted_iota(jnp.int32, (SGU_CHUNK, SGU_CHUNK), 0)
    cc = lax.broadcasted_iota(jnp.int32, (SGU_CHUNK, SGU_CHUNK), 1)
    causal = (rr // gate_len == cc // gate_len) & (cc <= rr)
    for g in range(SGU_GROUPS):
        gs = slice(g * SGU_GROUP_DIM, (g + 1) * SGU_GROUP_DIM)
        wg = jnp.where(causal, ws_ref[g], 0.0).astype(BF16)
        for c in range(tm // SGU_CHUNK):
            rs = slice(c * SGU_CHUNK, (c + 1) * SGU_CHUNK)
            s = _dot(wg, vnb[rs, gs]) + bs_ref[:, gs]
            sgu_ref[rs, gs] = (u[rs, gs] * s).astype(BF16)


def _proj_odd(x, g, w_in, ln_g, ln_b, w_s, b_s, qg, kvg, wqn, wqp, cos, sin, *, batch, seq_len, tm, gate_len,
              emit_vn):
    n = x.shape[0]
    pos_tiles = cos.shape[0] // tm
    kpe_transposed = tm % seq_len != 0
    if kpe_transposed:
        seq_tiles = seq_len // tm
        kpe_shape = (batch, ROPE_DIM, seq_len)
        kpe_spec = pl.BlockSpec((1, ROPE_DIM, tm), lambda i: (i // seq_tiles, 0, i % seq_tiles))
    else:
        kpe_shape, kpe_spec = (n, ROPE_DIM), pl.BlockSpec((tm, ROPE_DIM), lambda i: (i, 0))
    row_spec = lambda w: pl.BlockSpec((tm, w), lambda i: (i, 0))
    tab_spec = pl.BlockSpec((tm, LANES), lambda i: (i % pos_tiles, 0))
    consts = [g, w_in, ln_g, ln_b, w_s, b_s, qg, kvg, wqn, wqp]
    out_shape = [jax.ShapeDtypeStruct((n, SGU_DIM), BF16)]
    out_specs = [row_spec(SGU_DIM)]
    if emit_vn:
        out_shape.append(jax.ShapeDtypeStruct((n, SGU_DIM), F32))
        out_specs.append(row_spec(SGU_DIM))
    out_shape += [jax.ShapeDtypeStruct((n, MLA_HEADS * NOPE_DIM), BF16),
                  jax.ShapeDtypeStruct((n, MLA_HEADS * ROPE_DIM), BF16),
                  jax.ShapeDtypeStruct((n, KV_LORA), F32), jax.ShapeDtypeStruct((n, KV_LORA), BF16),
                  jax.ShapeDtypeStruct(kpe_shape, F32), jax.ShapeDtypeStruct((n, LANES), BF16)]
    out_specs += [row_spec(MLA_HEADS * NOPE_DIM), row_spec(MLA_HEADS * ROPE_DIM),
                  row_spec(KV_LORA), row_spec(KV_LORA), kpe_spec, row_spec(LANES)]
    return pl.pallas_call(
        functools.partial(_proj_odd_body, tm=tm, gate_len=gate_len, emit_vn=emit_vn, kpe_transposed=kpe_transposed),
        grid=(n // tm,),
        in_specs=[row_spec(D_MODEL)] + [_const_spec(c.shape) for c in consts] + [tab_spec, tab_spec],
        out_specs=out_specs, out_shape=out_shape,
        compiler_params=_cparams(("arbitrary",)), name="proj_odd")(x, *consts, cos, sin)


def _mla_queries(qn, qpe, wuk_ref, qcat_ref, tq):
    lane = lax.broadcasted_iota(jnp.int32, (tq, LANES), 1)
    low = lane < NOPE_DIM
    qpe = qpe.astype(F32)
    qcat_ref[:, KV_LORA:] = jnp.zeros((MLA_HEADS * tq, MLA_QK - KV_LORA), BF16)
    for hd in range(MLA_HEADS):
        p = hd // 2
        rs = slice(hd * tq, (hd + 1) * tq)
        pair = qn[:, p * LANES:(p + 1) * LANES]
        qm = jnp.where(low if hd % 2 == 0 else jnp.logical_not(low), pair, jnp.zeros_like(pair))
        qcat_ref[rs, :KV_LORA] = _dot(qm, wuk_ref[p]).astype(BF16)
        qcat_ref[rs, KV_LORA:KV_LORA + ROPE_DIM] = qpe[:, hd * ROPE_DIM:(hd + 1) * ROPE_DIM].astype(BF16)


def _lane_tile(x, width):
    return jnp.concatenate([x] * (width // LANES), axis=1)


def _mla_scores(qcat_ref, kcat, sc):
    sc[0][...] = _dot_nt(qcat_ref[...], kcat)


def _mla_softmax(sc, visible_fn):
    s_ref, p_ref, m_ref, l_ref, acc_ref = sc
    rows, tk = s_ref.shape
    for c in range(rows // MLA_ROW_CHUNK):
        rs = slice(c * MLA_ROW_CHUNK, (c + 1) * MLA_ROW_CHUNK)
        s = s_ref[rs, :] * (MLA_SCALE * LOG2_E)
        if visible_fn is not None:
            s = jnp.where(visible_fn(c), s, NEG_INF)
        m_prev = m_ref[rs, :]
        m_new = jnp.maximum(m_prev, jnp.max(s, axis=-1, keepdims=True))
        alpha = jnp.exp2(m_prev - m_new)
        p = jnp.exp2(s - _lane_tile(m_new, tk))
        l_ref[rs, :] = alpha * l_ref[rs, :] + jnp.sum(p, axis=-1, keepdims=True)
        m_ref[rs, :] = m_new
        p_ref[rs, :] = p.astype(BF16)
        acc_ref[rs, :] = acc_ref[rs, :] * _lane_tile(alpha, KV_LORA)


def _mla_pv(ck, sc):
    sc[4][...] += _dot(sc[1][...], ck)


def _mla_tile(qcat_ref, kcat, sc, visible_fn):
    _mla_scores(qcat_ref, kcat, sc)
    _mla_softmax(sc, visible_fn)
    _mla_pv(kcat[:, :KV_LORA], sc)


def _mla_init(sc):
    m_ref, l_ref, acc_ref = sc[2], sc[3], sc[4]
    m_ref[...] = jnp.full_like(m_ref, -jnp.inf)
    l_ref[...] = jnp.zeros_like(l_ref)
    acc_ref[...] = jnp.zeros_like(acc_ref)


def _mla_finish(sc, wuv_ref, o_ref, tq):
    l_ref, acc_ref = sc[3], sc[4]
    o_lat = (acc_ref[...] / _lane_tile(l_ref[...], KV_LORA)).astype(BF16)
    for p in range(MLA_HEADS // 2):
        h0, h1 = 2 * p, 2 * p + 1
        o_ref[:, p * LANES:(p + 1) * LANES] = (
            _dot(o_lat[h0 * tq:(h0 + 1) * tq], wuv_ref[h0]) + _dot(o_lat[h1 * tq:(h1 + 1) * tq], wuv_ref[h1])
        ).astype(BF16)


def _mla_prompt_body(qn_ref, qp_ref, ckv_ref, kpe_ref, wuk_ref, wuv_ref, o_ref, qcat_ref, *sc, tq, tk):
    qi = pl.program_id(1)
    _mla_queries(qn_ref[...], qp_ref[...], wuk_ref, qcat_ref, tq)
    _mla_init(sc)

    n_full = (qi * tq) // tk
    col = lax.broadcasted_iota(jnp.int32, (MLA_ROW_CHUNK, tk), 1)

    def visible(c):
        q_chunk_end = qi * tq + ((c * MLA_ROW_CHUNK) % tq) // CHUNK * CHUNK + CHUNK
        return col < q_chunk_end - n_full * tk

    def keys(kj):
        start = pl.multiple_of(kj * tk, tk)
        return jnp.concatenate([ckv_ref[pl.ds(start, tk), :], kpe_ref[pl.ds(start, tk), :],
                                jnp.zeros((tk, MLA_QK - KV_LORA - LANES), BF16)], axis=1)

    def body(kj, c):
        _mla_tile(qcat_ref, keys(kj), sc, None)
        return c

    lax.fori_loop(0, n_full, body, 0)
    _mla_tile(qcat_ref, keys(n_full), sc, visible)
    _mla_finish(sc, wuv_ref, o_ref, tq)


def _mla_scratch(rows, tk):
    return [pltpu.VMEM((rows, MLA_QK), BF16), pltpu.VMEM((rows, tk), F32), pltpu.VMEM((rows, tk), BF16),
            pltpu.VMEM((rows, LANES), F32), pltpu.VMEM((rows, LANES), F32), pltpu.VMEM((rows, KV_LORA), F32)]


def _mla_prompt(qn, qp, ckvb, kpeb, wuk, wuv, *, batch, seq_len, tq, tk):
    assert tq % CHUNK == 0 and CHUNK % MLA_ROW_CHUNK == 0 and seq_len % tq == 0 and seq_len % tk == 0
    n = qn.shape[0]
    nq = seq_len // tq
    width = MLA_HEADS * V_DIM
    return pl.pallas_call(
        functools.partial(_mla_prompt_body, tq=tq, tk=tk),
        grid=(batch, nq),
        in_specs=[pl.BlockSpec((tq, MLA_HEADS * NOPE_DIM), lambda b, i: (b * nq + i, 0)),
                  pl.BlockSpec((tq, MLA_HEADS * ROPE_DIM), lambda b, i: (b * nq + i, 0)),
                  pl.BlockSpec((seq_len, KV_LORA), lambda b, i: (b, 0)),
                  pl.BlockSpec((seq_len, LANES), lambda b, i: (b, 0)),
                  _const_spec(wuk.shape), _const_spec(wuv.shape)],
        out_specs=pl.BlockSpec((tq, width), lambda b, i: (b * nq + i, 0)),
        out_shape=jax.ShapeDtypeStruct((n, width), BF16),
        scratch_shapes=_mla_scratch(MLA_HEADS * tq, tk),
        compiler_params=_cparams(("arbitrary", "arbitrary")), name="mla_attn_prompt")(qn, qp, ckvb, kpeb, wuk, wuv)


def _mla_sample_body(qn_ref, qp_ref, cn_ref, pn_ref, cc_ref, pc_ref, wuk_ref, wuv_ref, o_ref, kcat_ref, qcat_ref,
                     *sc, dec, past, kblk, tk):
    j = pl.program_id(1)
    rows = MLA_HEADS * dec

    @pl.when(j == 0)
    def _():
        _mla_queries(qn_ref[...], qp_ref[...], wuk_ref, qcat_ref, dec)
        _mla_init(sc)
        kcat_ref[:, KV_LORA:] = jnp.zeros((tk, MLA_QK - KV_LORA), BF16)
        new = jnp.concatenate([cn_ref[...], pn_ref[...], jnp.zeros((dec, MLA_QK - KV_LORA - LANES), BF16)], axis=1)
        kcat = jnp.concatenate([new, jnp.zeros((tk - dec, MLA_QK), BF16)], axis=0)

        def visible(c):
            col = lax.broadcasted_iota(jnp.int32, (MLA_ROW_CHUNK, tk), 1)
            row = c * MLA_ROW_CHUNK + lax.broadcasted_iota(jnp.int32, (MLA_ROW_CHUNK, tk), 0)
            return (col < dec) & ((past + col) // CHUNK <= (past + row % dec) // CHUNK)

        _mla_tile(qcat_ref, kcat, sc, visible)

    def body(it, c):
        start = pl.multiple_of(it * tk, tk)
        kcat_ref[:, :KV_LORA] = cc_ref[0, pl.ds(start, tk), :].astype(BF16)
        kp_t = jnp.concatenate([pc_ref[0, :, pl.ds(start, tk)], jnp.zeros((LANES - ROPE_DIM, tk), F32)], axis=0)
        kcat_ref[:, KV_LORA:KV_LORA + LANES] = kp_t.T.astype(BF16)
        _mla_tile(qcat_ref, kcat_ref[...], sc, None)
        return c

    lax.fori_loop(0, kblk // tk, body, 0)

    @pl.when(j == pl.num_programs(1) - 1)
    def _():
        _mla_finish(sc, wuv_ref, o_ref, dec)


def _mla_sample(qn, qp, ckvb, kpeb, cache_ckv, cache_kpe_t, wuk, wuv, *, batch, dec, kblk=1024, tk=1024):
    n = qn.shape[0]
    past = cache_ckv.shape[1]
    width = MLA_HEADS * V_DIM
    return pl.pallas_call(
        functools.partial(_mla_sample_body, dec=dec, past=past, kblk=kblk, tk=tk),
        grid=(batch, past // kblk),
        in_specs=[pl.BlockSpec((dec, MLA_HEADS * NOPE_DIM), lambda b, j: (b, 0)),
                  pl.BlockSpec((dec, MLA_HEADS * ROPE_DIM), lambda b, j: (b, 0)),
                  pl.BlockSpec((dec, KV_LORA), lambda b, j: (b, 0)),
                  pl.BlockSpec((dec, LANES), lambda b, j: (b, 0)),
                  pl.BlockSpec((1, kblk, KV_LORA), lambda b, j: (b, j, 0)),
                  pl.BlockSpec((1, ROPE_DIM, kblk), lambda b, j: (b, 0, j)),
                  _const_spec(wuk.shape), _const_spec(wuv.shape)],
        out_specs=pl.BlockSpec((dec, width), lambda b, j: (b, 0)),
        out_shape=jax.ShapeDtypeStruct((n, width), BF16),
        scratch_shapes=[pltpu.VMEM((tk, MLA_QK), BF16)] + _mla_scratch(MLA_HEADS * dec, tk),
        compiler_params=_cparams(("arbitrary", "arbitrary")), name="mla_attn_sample")(
            qn, qp, ckvb, kpeb, cache_ckv, cache_kpe_t, wuk, wuv)


def _out_ffn_body(a_ref, b_ref, x_ref, woa_ref, wob_ref, gpost_ref, gpre_ref, wup_ref, wdn_ref, gfpost_ref, o_ref):
    tm = x_ref.shape[0]
    groups = [slice(r * tm // FFN_ROW_GROUPS, (r + 1) * tm // FFN_ROW_GROUPS) for r in range(FFN_ROW_GROUPS)]
    mixed = [_dot(a_ref[rs, :], woa_ref[...]) + _dot(b_ref[rs, :], wob_ref[...]) for rs in groups]
    x1 = [x_ref[rs, :] + _rms(m, gpost_ref[...]) for rs, m in zip(groups, mixed)]
    h = [_rms(x, gpre_ref[...]).astype(BF16) for x in x1]
    down = [jnp.zeros_like(x) for x in x1]
    for c in range(D_FF // FF_CHUNK):
        for r in range(FFN_ROW_GROUPS):
            up = _dot(h[r], wup_ref[:, c * FF_CHUNK:(c + 1) * FF_CHUNK])
            act = jnp.square(jnp.maximum(up, 0.0)).astype(BF16)
            down[r] = down[r] + _dot(act, wdn_ref[c * FF_CHUNK:(c + 1) * FF_CHUNK, :])
    for r, rs in enumerate(groups):
        o_ref[rs, :] = x1[r] + _rms(down[r], gfpost_ref[...])


def _out_ffn(a, b, x, woa, wob, g_post, g_pre, w_up, w_down, g_fpost, *, tm):
    n = x.shape[0]
    row_spec = lambda w: pl.BlockSpec((tm, w), lambda i: (i, 0))
    consts = [woa, wob, g_post, g_pre, w_up, w_down, g_fpost]
    return pl.pallas_call(
        _out_ffn_body,
        grid=(n // tm,),
        in_specs=[row_spec(a.shape[1]), row_spec(b.shape[1]), row_spec(D_MODEL)] + [_const_spec(c.shape) for c in consts],
        out_specs=row_spec(D_MODEL),
        out_shape=jax.ShapeDtypeStruct((n, D_MODEL), F32),
        compiler_params=_cparams(("arbitrary",)), name="out_ffn")(a, b, x, *consts)


def _rope_tables(pos, reps):
    half = ROPE_DIM // 2
    inv = ROPE_THETA ** (-jnp.arange(half, dtype=F32) / half)
    ang = pos.astype(F32)[:, None] * inv[None, :]
    cos = jnp.tile(jnp.concatenate([jnp.cos(ang), jnp.cos(ang)], axis=1), (reps, LANES // ROPE_DIM))
    sin = jnp.tile(jnp.concatenate([-jnp.sin(ang), jnp.sin(ang)], axis=1), (reps, LANES // ROPE_DIM))
    return cos, sin


def _prep_even(p, j):
    w_in = p["even_w_in"][j]
    return dict(w_in=w_in.astype(BF16), w_kv_t=w_in[:, SB_WIDTH:3 * SB_WIDTH].T.astype(BF16), w_conv=p["even_w_conv"][j],
                woa=p["even_w_out"][j, :SB_WIDTH].astype(BF16), wob=p["even_w_out"][j, SB_WIDTH:].astype(BF16))


def _prep_odd(p, j, gate_len):
    w_in = p["odd_w_in"][j]
    w_in = jnp.pad(w_in, ((0, 0), (0, ODD_IN_PAD - w_in.shape[1]))).astype(BF16)
    reps = SGU_CHUNK // gate_len
    w_s = jnp.tile(p["sgu_w_s"][j, :, :gate_len, :gate_len], (1, reps, reps))
    b_s = jnp.tile(p["sgu_b_s"][j, :, :gate_len], (1, reps))
    b_s = jnp.repeat(b_s.T, SGU_GROUP_DIM, axis=1)
    w_uq = p["mla_w_uq"][j].reshape(Q_LORA, MLA_HEADS, NOPE_DIM + ROPE_DIM)
    wqn = w_uq[:, :, :NOPE_DIM].reshape(Q_LORA, MLA_HEADS * NOPE_DIM).astype(BF16)
    wqp = w_uq[:, :, NOPE_DIM:].reshape(Q_LORA, MLA_HEADS * ROPE_DIM).astype(BF16)
    wuk = p["mla_w_uk"][j].reshape(MLA_HEADS // 2, 2 * NOPE_DIM, KV_LORA).astype(BF16)
    w_uv = p["mla_w_uv"][j]
    wuv = jnp.stack([jnp.pad(w_uv[h], ((0, 0), ((h % 2) * V_DIM, (1 - h % 2) * V_DIM))) for h in range(MLA_HEADS)])
    return dict(w_in=w_in, ln_g=p["sgu_ln_g"][j][None], ln_b=p["sgu_ln_b"][j][None], w_s=w_s, b_s=b_s,
                qg=p["mla_q_norm_g"][j][None], kvg=p["mla_kv_norm_g"][j][None], wqn=wqn, wqp=wqp, wuk=wuk,
                wuv=wuv.astype(BF16),
                woa=p["odd_w_out"][j, :SGU_DIM].astype(BF16), wob=p["odd_w_out"][j, SGU_DIM:].astype(BF16))


def _run_trunk(x, pos, past, p, *, batch, seq_len):
    depth = p["mix_pre_g"].shape[0]
    n = batch * seq_len
    x = x.reshape(n, D_MODEL)
    is_sample = past is not None
    tm = min(ROW_TILE, n)
    assert n % tm == 0 and (tm % seq_len == 0 or seq_len % tm == 0) and tm % SGU_CHUNK == 0
    gate_len = min(seq_len, SGU_CHUNK)
    tm_proj = PROJ_ROW_TILE if (seq_len % PROJ_ROW_TILE == 0 and seq_len > PROJ_ROW_TILE) else tm
    cos, sin = _rope_tables(pos, max(1, tm_proj // seq_len))
    st = {k: [] for k in ("sb_k", "sb_v", "conv", "ckv", "kpe", "sgu_v")}
    for layer in range(depth):
        j = layer // 2
        g_pre = p["mix_pre_g"][layer][None]
        if layer % 2 == 0:
            w = _prep_even(p, j)
            conv_prev = past["conv"][j] if is_sample else None
            q, k, v, kb, vb, b_mix, conv_state = _proj_even(x, g_pre, w["w_in"], w["w_kv_t"], w["w_conv"], conv_prev,
                                                            batch=batch, seq_len=seq_len, tm=tm_proj)
            if is_sample:
                assert tm % seq_len == 0

                def cache_t(c):
                    return jnp.transpose(c, (0, 2, 3, 1)).reshape(batch, SB_WIDTH, -1)

                a_mix = _sb_sample(q, kb, vb, cache_t(past["sb_k"][j]), cache_t(past["sb_v"][j]),
                                   batch=batch, dec=seq_len)
                k, v = (t.reshape(batch, seq_len, SB_HEADS, SB_HEAD_DIM) for t in (k, v))
            else:
                assert seq_len % tm == 0
                a_mix = _sb_prompt(q, kb, vb, batch=batch, seq_len=seq_len, tq=SB_TQ, tk=SB_TK)
                k, v = (jnp.transpose(t.reshape(batch, SB_HEADS, SB_HEAD_DIM, seq_len), (0, 3, 1, 2)) for t in (k, v))
            st["sb_k"].append(k)
            st["sb_v"].append(v)
            st["conv"].append(conv_state)
        else:
            w = _prep_odd(p, j, gate_len)
            outs = _proj_odd(x, g_pre, w["w_in"], w["ln_g"], w["ln_b"], w["w_s"], w["b_s"], w["qg"], w["kvg"],
                             w["wqn"], w["wqp"], cos, sin, batch=batch, seq_len=seq_len, tm=tm_proj, gate_len=gate_len,
                             emit_vn=is_sample)
            if is_sample:
                a_mix, vn, qn, qp, ckv, ckvb, kpe, kpeb = outs
                st["sgu_v"].append(vn.reshape(batch, seq_len, SGU_DIM))
                b_mix = _mla_sample(qn, qp, ckvb, kpeb, past["ckv"][j], jnp.transpose(past["kpe"][j], (0, 2, 1)),
                                    w["wuk"], w["wuv"],
                                    batch=batch, dec=seq_len)
            else:
                a_mix, qn, qp, ckv, ckvb, kpe, kpeb = outs
                b_mix = _mla_prompt(qn, qp, ckvb, kpeb, w["wuk"], w["wuv"], batch=batch, seq_len=seq_len,
                                    tq=MLA_TQ, tk=MLA_TK)
            st["ckv"].append(ckv.reshape(batch, seq_len, KV_LORA))
            st["kpe"].append(jnp.transpose(kpe, (0, 2, 1)) if kpe.ndim == 3 else kpe.reshape(batch, seq_len, ROPE_DIM))
        x = _out_ffn(a_mix, b_mix, x, w["woa"], w["wob"], p["mix_post_g"][layer][None], p["ffn_pre_g"][layer][None],
                     p["ffn_w_up"][layer].astype(BF16), p["ffn_w_down"][layer].astype(BF16),
                     p["ffn_post_g"][layer][None], tm=tm)
    states = {k: jnp.stack(v) for k, v in st.items() if v}
    return x.reshape(batch, seq_len, D_MODEL), states


def kernel(x_prompt, x_sample, cache_sb_k, cache_sb_v, state_conv, cache_mla_ckv, cache_mla_kpe,
           mix_pre_g, mix_post_g, ffn_pre_g, ffn_post_g, even_w_in, even_w_conv, even_w_out,
           odd_w_in, sgu_ln_g, sgu_ln_b, sgu_w_s, sgu_b_s, mla_q_norm_g, mla_kv_norm_g,
           mla_w_uq, mla_w_uk, mla_w_uv, odd_w_out, ffn_w_up, ffn_w_down):
    params = {
        "mix_pre_g": mix_pre_g, "mix_post_g": mix_post_g, "ffn_pre_g": ffn_pre_g, "ffn_post_g": ffn_post_g,
        "even_w_in": even_w_in, "even_w_conv": even_w_conv, "even_w_out": even_w_out,
        "odd_w_in": odd_w_in, "sgu_ln_g": sgu_ln_g, "sgu_ln_b": sgu_ln_b, "sgu_w_s": sgu_w_s,
        "sgu_b_s": sgu_b_s, "mla_q_norm_g": mla_q_norm_g, "mla_kv_norm_g": mla_kv_norm_g,
        "mla_w_uq": mla_w_uq, "mla_w_uk": mla_w_uk, "mla_w_uv": mla_w_uv, "odd_w_out": odd_w_out,
        "ffn_w_up": ffn_w_up, "ffn_w_down": ffn_w_down,
    }
    batch, seq_len, _ = x_prompt.shape
    pos_p = jnp.arange(seq_len, dtype=jnp.int32)
    y_prompt, st_p = _run_trunk(x_prompt, pos_p, None, params, batch=batch, seq_len=seq_len)
    dec_batch, dec_seq, _ = x_sample.shape
    past_len = cache_sb_k.shape[2]
    pos_s = past_len + jnp.arange(dec_seq, dtype=jnp.int32)
    past = {"sb_k": cache_sb_k, "sb_v": cache_sb_v, "conv": state_conv, "ckv": cache_mla_ckv, "kpe": cache_mla_kpe}
    y_sample, st_s = _run_trunk(x_sample, pos_s, past, params, batch=dec_batch, seq_len=dec_seq)
    return (y_prompt, y_sample,
            st_p["sb_k"], st_p["sb_v"], st_p["conv"], st_p["ckv"], st_p["kpe"],
            st_s["sb_k"], st_s["sb_v"], st_s["conv"], st_s["ckv"], st_s["kpe"], st_s["sgu_v"])
```

```python
import functools
import math

import numpy as np
import jax
import jax.numpy as jnp
from jax import lax
from jax.experimental import pallas as pl
from jax.experimental.pallas import tpu as pltpu

F32 = jnp.float32
BF16 = jnp.bfloat16

EPS = 1e-6
D_MODEL = 1024
CHUNK = 64
SB_HEADS = 8
SB_HEAD_DIM = 64
SB_WIDTH = SB_HEADS * SB_HEAD_DIM
SB_SCALE = 1.0 / math.sqrt(SB_HEAD_DIM)
CONV_DIM = D_MODEL // 2
CONV_W = 3
SGU_CHUNK = 128
SGU_GROUPS = 4
SGU_DIM = D_MODEL // 2
SGU_GROUP_DIM = SGU_DIM // SGU_GROUPS
MLA_HEADS = 8
Q_LORA = 384
KV_LORA = 256
NOPE_DIM = 64
ROPE_DIM = 32
V_DIM = 64
ROPE_THETA = 10000.0
MLA_SCALE = 1.0 / math.sqrt(NOPE_DIM + ROPE_DIM)
D_FF = 4 * D_MODEL
FF_CHUNK = 1024
FFN_ROW_GROUPS = 2

LANES = 128
HEAD_PAIRS = SB_HEADS // 2
MLA_QK = 2 * KV_LORA
ODD_IN_PAD =2 * SGU_DIM + Q_LORA + KV_LORA + LANES
VMEM_LIMIT = 56 * 1024 * 1024
NEG_INF = -1e30
LOG2_E = math.log2(math.e)

ROW_TILE = 512
PROJ_ROW_TILE = 1024
SB_TQ, SB_TK = 1024, 128
MLA_TQ, MLA_TK = 512, 512
MLA_ROW_CHUNK = 64


def _cparams(sem):
    return pltpu.CompilerParams(dimension_semantics=sem, vmem_limit_bytes=VMEM_LIMIT)


def _const_spec(shape):
    nd = len(shape)
    return pl.BlockSpec(shape, lambda *_: (0,) * nd, pipeline_mode=pl.Buffered(1))


def _rms(x, g):
    return x * lax.rsqrt(jnp.mean(x * x, axis=-1, keepdims=True) + EPS) * g


def _dot(a, b):
    return jnp.dot(a, b, preferred_element_type=F32)


def _dot_nt(a, b):
    return lax.dot_general(a, b, (((1,), (1,)), ((), ())), preferred_element_type=F32)


def _log_sigmoid(z):
    neg_abs = lax.bitcast_convert_type(lax.bitcast_convert_type(z, jnp.uint32) | jnp.uint32(0x80000000), F32)
    return jnp.minimum(z, 0.0) - jnp.log(1.0 + jnp.exp(neg_abs))


def _split_hi_lo(x):
    hi = x.astype(BF16)
    lo = (x - hi.astype(F32)).astype(BF16)
    return hi, lo


def _rope_block(x, cos, sin):
    half = ROPE_DIM // 2
    lane = lax.broadcasted_iota(jnp.int32, x.shape, 1)
    partner = jnp.where(lane % ROPE_DIM < half, pltpu.roll(x, LANES - half, 1), pltpu.roll(x, half, 1))
    return x * cos + partner * sin


def _proj_even_body(*refs, tm, seq_len, whole_seqs):
    if whole_seqs:
        (x_ref, g_ref, w_ref, wc_ref, e1_ref, e2_ref,
         q_ref, k_ref, v_ref, kb_ref, vb_ref, gc_ref, ci_ref) = refs
    else:
        (x_ref, g_ref, w_ref, wc_ref, wkv_ref,
         q_ref, k_ref, v_ref, kb_ref, vb_ref, gc_ref, cs_ref, tail_ref) = refs
    if not whole_seqs:
        @pl.when((pl.program_id(0) % (seq_len // tm)) == 0)
        def _():
            tail_ref[...] = jnp.zeros_like(tail_ref)

    h = _rms(x_ref[...], g_ref[...]).astype(BF16)

    def proj(j):
        return _dot(h, w_ref[:, j * SB_WIDTH:(j + 1) * SB_WIDTH])

    g_post = proj(3)
    ci = proj(4) * proj(5)
    r1 = pltpu.roll(ci, 1, 0)
    r2 = pltpu.roll(ci, 2, 0)
    row = lax.broadcasted_iota(jnp.int32, (tm, 1), 0)
    if whole_seqs:
        tpos = row % seq_len
        s1 = jnp.where(tpos < 1, e1_ref[...], r1)
        s2 = jnp.where(tpos < 2, e2_ref[...], r2)
        ci_ref[...] = ci
    else:
        t1 = tail_ref[7:8, :]
        t2 = tail_ref[6:7, :]
        s1 = jnp.where(row == 0, t1, r1)
        s2 = jnp.where(row == 0, t2, jnp.where(row == 1, t1, r2))
        tail_ref[...] = ci[tm - 8:, :]
        cs_ref[0] = ci[tm - (CONV_W - 1):, :]
    conv = wc_ref[0:1, :] * s2 + wc_ref[1:2, :] * s1 + wc_ref[2:3, :] * ci
    gc_ref[...] = (g_post * conv).astype(BF16)

    q_ref[...] = (proj(0) * SB_SCALE).astype(BF16)
    if whole_seqs:
        k = proj(1)
        v = proj(2)
        k_ref[...] = k
        v_ref[...] = v
        kb_ref[...] = k.astype(BF16)
        vb_ref[...] = v.astype(BF16)
    else:
        k = _dot_nt(wkv_ref[:SB_WIDTH, :], h)
        v = _dot_nt(wkv_ref[SB_WIDTH:, :], h)
        k_ref[0] = k
        v_ref[0] = v
        kb_ref[0] = k.astype(BF16)
        vb_ref[0] = v.astype(BF16)


def _proj_even(x, g, w_in, w_kv_t, w_conv, conv_prev, *, batch, seq_len, tm):
    n = x.shape[0]
    whole = tm % seq_len == 0
    grid = (n // tm,)
    row_spec = lambda w: pl.BlockSpec((tm, w), lambda i: (i, 0))
    in_specs = [row_spec(D_MODEL), _const_spec((1, D_MODEL)), _const_spec(w_in.shape), _const_spec(w_conv.shape)]
    args = [x, g, w_in, w_conv]
    if whole:
        kv_shape, kv_spec = (n, SB_WIDTH), row_spec(SB_WIDTH)
    else:
        seq_tiles = seq_len // tm
        kv_shape = (batch, SB_WIDTH, seq_len)
        kv_spec = pl.BlockSpec((1, SB_WIDTH, tm), lambda i: (i // seq_tiles, 0, i % seq_tiles))
    out_shape = [jax.ShapeDtypeStruct((n, SB_WIDTH), BF16), jax.ShapeDtypeStruct(kv_shape, F32),
                 jax.ShapeDtypeStruct(kv_shape, F32), jax.ShapeDtypeStruct(kv_shape, BF16),
                 jax.ShapeDtypeStruct(kv_shape, BF16), jax.ShapeDtypeStruct((n, CONV_DIM), BF16)]
    out_specs = [row_spec(SB_WIDTH)] + [kv_spec] * 4 + [row_spec(CONV_DIM)]
    scratch = []
    if whole:
        if conv_prev is None:
            conv_prev = jnp.zeros((batch, CONV_W - 1, CONV_DIM), F32)
        e1 =jnp.zeros((batch, seq_len, CONV_DIM), F32).at[:, 0].set(conv_prev[:, 1])
        e2 = jnp.zeros((batch, seq_len, CONV_DIM), F32).at[:, 0].set(conv_prev[:, 0]).at[:, 1].set(conv_prev[:, 1])
        args += [e1.reshape(n, CONV_DIM), e2.reshape(n, CONV_DIM)]
        in_specs += [row_spec(CONV_DIM), row_spec(CONV_DIM)]
        out_shape.append(jax.ShapeDtypeStruct((n, CONV_DIM), F32))
        out_specs.append(row_spec(CONV_DIM))
    else:
        assert seq_len % tm == 0 and conv_prev is None
        args.append(w_kv_t)
        in_specs.append(_const_spec(w_kv_t.shape))
        out_shape.append(jax.ShapeDtypeStruct((batch, CONV_W - 1, CONV_DIM), F32))
        out_specs.append(pl.BlockSpec((1, CONV_W - 1, CONV_DIM), lambda i: (i // seq_tiles, 0, 0)))
        scratch.append(pltpu.VMEM((8, CONV_DIM), F32))
    outs = pl.pallas_call(
        functools.partial(_proj_even_body, tm=tm, seq_len=seq_len, whole_seqs=whole),
        grid=grid, in_specs=in_specs, out_specs=out_specs, out_shape=out_shape, scratch_shapes=scratch,
        compiler_params=_cparams(("arbitrary",)), name="proj_even")(*args)
    q, k, v, kb, vb, gc, last = outs
    if whole:
        last = last.reshape(batch, seq_len, CONV_DIM)[:, seq_len - (CONV_W - 1):]
    return q, k, v, kb, vb, gc, last


def _suffix_rhs(tk):
    j = np.arange(2 * tk)[:, None] % tk
    c = np.arange(2 * tk)[None, :]
    return jnp.asarray(np.where(c < tk, j > c, True), dtype=BF16)


def _sb_prompt_body(q_ref, k_ref, v_ref, r_ref, o_ref, carry_ref, acc_ref, *, tq, tk):
    qi = pl.program_id(1)
    ratio = tq // tk
    even = lax.broadcasted_iota(jnp.int32, (LANES, tk), 0) < SB_HEAD_DIM
    rhs = r_ref[...]
    carry_ref[...] = jnp.zeros_like(carry_ref)
    acc_ref[...] = jnp.zeros_like(acc_ref)
    qs = [q_ref[:, p * LANES:(p + 1) * LANES] for p in range(HEAD_PAIRS)]

    def block_diag(blk):
        zero = jnp.zeros_like(blk)
        return jnp.concatenate([jnp.where(even, blk, zero), jnp.where(even, zero, blk)], axis=1)

    def tile(kj, masked, r0=0):
        start = pl.multiple_of(kj * tk, tk)
        if masked:
            q_pos = qi * tq + r0 + lax.broadcasted_iota(jnp.int32, (tq - r0, tk), 0)
            k_pos = kj * tk + lax.broadcasted_iota(jnp.int32, (tq - r0, tk), 1)
            visible = k_pos < q_pos
        zs = [_dot(qs[p][r0:], block_diag(k_ref[0, p * LANES:(p + 1) * LANES, pl.ds(start, tk)]))
              for p in range(HEAD_PAIRS)]
        log_betas, splits = [], []
        for hd in range(SB_HEADS):
            z = zs[hd // 2][:, (hd % 2) * tk:(hd % 2 + 1) * tk]
            log_beta = _log_sigmoid(z)
            log_1m = log_beta - z
            if masked:
                log_1m = jnp.where(visible, log_1m, 0.0)
            hi, lo = _split_hi_lo(log_1m)
            log_betas.append(log_beta)
            splits.append(jnp.concatenate([hi, lo], axis=1))
        sums = [_dot(sp, rhs) for sp in splits]
        ws = []
        for hd in range(SB_HEADS):
            carry = carry_ref[hd, r0:, :]
            w = jnp.exp(log_betas[hd] + sums[hd][:, :tk] + carry)
            if masked:
                w = jnp.where(visible, w, 0.0)
            carry_ref[hd, r0:, :] = carry + sums[hd][:, tk:]
            ws.append(w.astype(BF16))
        for p in range(HEAD_PAIRS):
            vbd = block_diag(v_ref[0, p * LANES:(p + 1) * LANES, pl.ds(start, tk)])
            acc_ref[p, r0:, :] += _dot_nt(jnp.concatenate([ws[2 * p], ws[2 * p + 1]], axis=1), vbd)

    for t in reversed(range(ratio)):
        tile(qi * ratio + t, True, r0=t * tk)

    def body(it, c):
        tile(qi * ratio - 1 - 2 * it, False)
        tile(qi * ratio - 2 - 2 * it, False)
        return c

    lax.fori_loop(0, qi * (ratio // 2), body, 0)
    for p in range(HEAD_PAIRS):
        o_ref[:, p * LANES:(p + 1) * LANES] = acc_ref[p].astype(BF16)


def _sb_prompt(q, kb, vb, *, batch, seq_len, tq, tk):
    assert (tq // tk) % 2 == 0 and tq % tk == 0 and seq_len % tq == 0
    n = q.shape[0]
    nq = seq_len // tq
    rhs = _suffix_rhs(tk)
    q_spec = pl.BlockSpec((tq, SB_WIDTH), lambda b, i: (b * nq + i, 0))
    kv_spec = pl.BlockSpec((1, SB_WIDTH, seq_len), lambda b, i: (b, 0, 0))
    return pl.pallas_call(
        functools.partial(_sb_prompt_body, tq=tq, tk=tk),
        grid=(batch, nq),
        in_specs=[q_spec, kv_spec, kv_spec, _const_spec(rhs.shape)],
        out_specs=q_spec,
        out_shape=jax.ShapeDtypeStruct((n, SB_WIDTH), BF16),
        scratch_shapes=[pltpu.VMEM((SB_HEADS, tq, LANES), F32), pltpu.VMEM((HEAD_PAIRS, tq, LANES), F32)],
        compiler_params=_cparams(("arbitrary", "arbitrary")), name="sb_attn_prompt")(q, kb, vb, rhs)


def _sb_sample_body(q_ref, kn_ref, vn_ref, kc_ref, vc_ref, l_ref, o_ref, qh_ref, carry_ref, acc_ref,
                    *, dec, kblk, tk):
    j = pl.program_id(1)
    rows = SB_HEADS * dec
    rhs = l_ref[...]

    def tiles(kts, vts, visible):
        log_betas, sums = [], []
        for kt in kts:
            z = _dot(qh_ref[...], kt)
            log_beta = _log_sigmoid(z)
            log_1m = log_beta - z
            if visible is not None:
                log_1m = jnp.where(visible, log_1m, 0.0)
            hi, lo = _split_hi_lo(log_1m)
            log_betas.append(log_beta)
            sums.append(_dot(jnp.concatenate([hi, lo], axis=1), rhs))
        carry = carry_ref[...]
        acc = acc_ref[...]
        for log_beta, sm, vt in zip(log_betas, sums, vts):
            w = jnp.exp(log_beta + sm[:, :tk] + carry)
            if visible is not None:
                w = jnp.where(visible, w, 0.0)
            acc = acc + _dot_nt(w.astype(BF16), vt)
            carry = carry + sm[:, tk:]
        carry_ref[...] = carry
        acc_ref[...] = acc

    def new_keys_t(x):
        x = jnp.concatenate([x.astype(F32), jnp.zeros((tk - dec, SB_WIDTH), F32)], axis=0)
        return x.T.astype(BF16)

    @pl.when(j == 0)
    def _():
        q = q_ref[...]
        qt = jnp.concatenate([q] * SB_HEADS, axis=0)
        rr = lax.broadcasted_iota(jnp.int32, (rows, SB_WIDTH), 0)
        cc = lax.broadcasted_iota(jnp.int32, (rows, SB_WIDTH), 1)
        qh_ref[...] = jnp.where(rr // dec == cc // SB_HEAD_DIM, qt, jnp.zeros_like(qt))
        carry_ref[...] = jnp.zeros_like(carry_ref)
        acc_ref[...] = jnp.zeros_like(acc_ref)
        query_t = lax.broadcasted_iota(jnp.int32, (rows, tk), 0) % dec
        key_i = lax.broadcasted_iota(jnp.int32, (rows, tk), 1)
        tiles([new_keys_t(kn_ref[...])], [new_keys_t(vn_ref[...])],
              key_i < query_t)

    order = list(reversed(range(kblk // tk)))
    tiles([kc_ref[0, :, s * tk:(s + 1) * tk].astype(BF16) for s in order],
          [vc_ref[0, :, s * tk:(s + 1) * tk].astype(BF16) for s in order], None)

    @pl.when(j == pl.num_programs(1) - 1)
    def _():
        cc = lax.broadcasted_iota(jnp.int32, (dec, SB_WIDTH), 1)
        out = jnp.zeros((dec, SB_WIDTH), F32)
        for h in range(SB_HEADS):
            out = out + jnp.where(cc // SB_HEAD_DIM == h, acc_ref[h * dec:(h + 1) * dec, :], 0.0)
        o_ref[...] = out.astype(BF16)


def _sb_sample(q, kb, vb, cache_kt, cache_vt, *, batch, dec, kblk=1024, tk=256):
    n = q.shape[0]
    past = cache_kt.shape[2]
    nkb = past // kblk
    rows = SB_HEADS * dec
    rhs = _suffix_rhs(tk)
    new_spec = pl.BlockSpec((dec, SB_WIDTH), lambda b, j: (b, 0))
    cache_spec = pl.BlockSpec((1, SB_WIDTH, kblk), lambda b, j: (b, 0, nkb - 1 - j))
    return pl.pallas_call(
        functools.partial(_sb_sample_body, dec=dec, kblk=kblk, tk=tk),
        grid=(batch, nkb),
        in_specs=[new_spec, new_spec, new_spec, cache_spec, cache_spec, _const_spec(rhs.shape)],
        out_specs=new_spec,
        out_shape=jax.ShapeDtypeStruct((n, SB_WIDTH), BF16),
        scratch_shapes=[pltpu.VMEM((rows, SB_WIDTH), BF16), pltpu.VMEM((rows, tk), F32),
                        pltpu.VMEM((rows, SB_WIDTH), F32)],
        compiler_params=_cparams(("arbitrary", "arbitrary")), name="sb_attn_sample")(
            q, kb, vb, cache_kt, cache_vt, rhs)


def _proj_odd_body(*refs, tm, gate_len, emit_vn, kpe_transposed):
    (x_ref, g_ref, w_ref, lng_ref, lnb_ref, ws_ref, bs_ref, qg_ref, kvg_ref, wqn_ref, wqp_ref,
     cos_ref, sin_ref) = refs[:13]
    outs = refs[13:]
    if emit_vn:
        sgu_ref, vn_ref, qn_ref, qp_ref, ckv_ref, ckvb_ref, kpe_ref, kpeb_ref = outs
    else:
        sgu_ref, qn_ref, qp_ref, ckv_ref, ckvb_ref, kpe_ref, kpeb_ref = outs
    h = _rms(x_ref[...], g_ref[...]).astype(BF16)
    o_v, o_q, o_kv, o_pe = SGU_DIM, 2 * SGU_DIM, 2 * SGU_DIM + Q_LORA, 2 * SGU_DIM + Q_LORA + KV_LORA
    cos = cos_ref[...]
    sin = sin_ref[...]
    kpe = _rope_block(_dot(h, w_ref[:, o_pe:]), cos, sin)
    if kpe_transposed:
        kpe_ref[0] = kpe.T[:ROPE_DIM, :]
    else:
        kpe_ref[...] = kpe[:, :ROPE_DIM]
    kpeb_ref[...] = kpe.astype(BF16)
    ckv = _rms(_dot(h, w_ref[:, o_kv:o_pe]), kvg_ref[...])
    ckv_ref[...] = ckv
    ckvb_ref[...] = ckv.astype(BF16)
    cq = _rms(_dot(h, w_ref[:, o_q:o_kv]), qg_ref[...]).astype(BF16)
    for blk in range(MLA_HEADS * ROPE_DIM // LANES):
        bs = slice(blk * LANES, (blk + 1) * LANES)
        qp_ref[:, bs] = _rope_block(_dot(cq, wqp_ref[:, bs]), cos, sin).astype(BF16)
    qn_ref[...] = _dot(cq, wqn_ref[...]).astype(BF16)
    u = _dot(h, w_ref[:, :o_v])
    v = _dot(h, w_ref[:, o_v:o_q])
    mu = jnp.mean(v, axis=-1, keepdims=True)
    vc = v - mu
    var = jnp.mean(vc * vc, axis=-1, keepdims=True)
    vn = vc * lax.rsqrt(var + EPS) * lng_ref[...] + lnb_ref[...]
    if emit_vn:
        vn_ref[...] = vn
    vnb = vn.astype(BF16)
    rr = lax.broadcasted_iota(jnp.int32, (SGU_CHUNK, SGU_CHUNK), 0)
    cc = lax.broadcasted_iota(jnp.int32, (SGU_CHUNK, SGU_CHUNK), 1)
    causal = (rr // gate_len == cc // gate_len) & (cc <= rr)
    for g in range(SGU_GROUPS):
        gs = slice(g * SGU_GROUP_DIM, (g + 1) * SGU_GROUP_DIM)
        wg = jnp.where(causal, ws_ref[g], 0.0).astype(BF16)
        for c in range(tm // SGU_CHUNK):
            rs = slice(c * SGU_CHUNK, (c + 1) * SGU_CHUNK)
            s = _dot(wg, vnb[rs, gs]) + bs_ref[:, gs]
            sgu_ref[rs, gs] = (u[rs, gs] * s).astype(BF16)


def _proj_odd(x, g, w_in, ln_g, ln_b, w_s, b_s, qg, kvg, wqn, wqp, cos, sin, *, batch, seq_len, tm, gate_len,
              emit_vn):
    n = x.shape[0]
    pos_tiles = cos.shape[0] // tm
    kpe_transposed = tm % seq_len != 0
    if kpe_transposed:
        seq_tiles = seq_len // tm
        kpe_shape = (batch, ROPE_DIM, seq_len)
        kpe_spec = pl.BlockSpec((1, ROPE_DIM, tm), lambda i: (i // seq_tiles, 0, i % seq_tiles))
    else:
        kpe_shape, kpe_spec = (n, ROPE_DIM), pl.BlockSpec((tm, ROPE_DIM), lambda i: (i, 0))
    row_spec = lambda w: pl.BlockSpec((tm, w), lambda i: (i, 0))
    tab_spec = pl.BlockSpec((tm, LANES), lambda i: (i % pos_tiles, 0))
    consts = [g, w_in, ln_g, ln_b, w_s, b_s, qg, kvg, wqn, wqp]
    out_shape = [jax.ShapeDtypeStruct((n, SGU_DIM), BF16)]
    out_specs = [row_spec(SGU_DIM)]
    if emit_vn:
        out_shape.append(jax.ShapeDtypeStruct((n, SGU_DIM), F32))
        out_specs.append(row_spec(SGU_DIM))
    out_shape += [jax.ShapeDtypeStruct((n, MLA_HEADS * NOPE_DIM), BF16),
                  jax.ShapeDtypeStruct((n, MLA_HEADS * ROPE_DIM), BF16),
                  jax.ShapeDtypeStruct((n, KV_LORA), F32), jax.ShapeDtypeStruct((n, KV_LORA), BF16),
                  jax.ShapeDtypeStruct(kpe_shape, F32), jax.ShapeDtypeStruct((n, LANES), BF16)]
    out_specs += [row_spec(MLA_HEADS * NOPE_DIM), row_spec(MLA_HEADS * ROPE_DIM),
                  row_spec(KV_LORA), row_spec(KV_LORA), kpe_spec, row_spec(LANES)]
    return pl.pallas_call(
        functools.partial(_proj_odd_body, tm=tm, gate_len=gate_len, emit_vn=emit_vn, kpe_transposed=kpe_transposed),
        grid=(n // tm,),
        in_specs=[row_spec(D_MODEL)] + [_const_spec(c.shape) for c in consts] + [tab_spec, tab_spec],
        out_specs=out_specs, out_shape=out_shape,
        compiler_params=_cparams(("arbitrary",)), name="proj_odd")(x, *consts, cos, sin)


def _mla_queries(qn, qpe, wuk_ref, qcat_ref, tq):
    lane = lax.broadcasted_iota(jnp.int32, (tq, LANES), 1)
    low = lane < NOPE_DIM
    qpe = qpe.astype(F32)
    qcat_ref[:, KV_LORA:] = jnp.zeros((MLA_HEADS * tq, MLA_QK - KV_LORA), BF16)
    for hd in range(MLA_HEADS):
        p = hd // 2
        rs = slice(hd * tq, (hd + 1) * tq)
        pair = qn[:, p * LANES:(p + 1) * LANES]
        qm = jnp.where(low if hd % 2 == 0 else jnp.logical_not(low), pair, jnp.zeros_like(pair))
        qcat_ref[rs, :KV_LORA] = _dot(qm, wuk_ref[p]).astype(BF16)
        qcat_ref[rs, KV_LORA:KV_LORA + ROPE_DIM] = qpe[:, hd * ROPE_DIM:(hd + 1) * ROPE_DIM].astype(BF16)


def _lane_tile(x, width):
    return jnp.concatenate([x] * (width // LANES), axis=1)


def _mla_scores(qcat_ref, kcat, sc):
    sc[0][...] = _dot_nt(qcat_ref[...], kcat)


def _mla_softmax(sc, visible_fn):
    s_ref, p_ref, m_ref, l_ref, acc_ref = sc
    rows, tk = s_ref.shape
    for c in range(rows // MLA_ROW_CHUNK):
        rs = slice(c * MLA_ROW_CHUNK, (c + 1) * MLA_ROW_CHUNK)
        s = s_ref[rs, :] * (MLA_SCALE * LOG2_E)
        if visible_fn is not None:
            s = jnp.where(visible_fn(c), s, NEG_INF)
        m_prev = m_ref[rs, :]
        m_new = jnp.maximum(m_prev, jnp.max(s, axis=-1, keepdims=True))
        alpha = jnp.exp2(m_prev - m_new)
        p = jnp.exp2(s - _lane_tile(m_new, tk))
        l_ref[rs, :] = alpha * l_ref[rs, :] + jnp.sum(p, axis=-1, keepdims=True)
        m_ref[rs, :] = m_new
        p_ref[rs, :] = p.astype(BF16)
        acc_ref[rs, :] = acc_ref[rs, :] * _lane_tile(alpha, KV_LORA)


def _mla_pv(ck, sc):
    sc[4][...] += _dot(sc[1][...], ck)


def _mla_tile(qcat_ref, kcat, sc, visible_fn):
    _mla_scores(qcat_ref, kcat, sc)
    _mla_softmax(sc, visible_fn)
    _mla_pv(kcat[:, :KV_LORA], sc)


def _mla_init(sc):
    m_ref, l_ref, acc_ref = sc[2], sc[3], sc[4]
    m_ref[...] = jnp.full_like(m_ref, -jnp.inf)
    l_ref[...] = jnp.zeros_like(l_ref)
    acc_ref[...] = jnp.zeros_like(acc_ref)


def _mla_finish(sc, wuv_ref, o_ref, tq):
    l_ref, acc_ref = sc[3], sc[4]
    o_lat = (acc_ref[...] / _lane_tile(l_ref[...], KV_LORA)).astype(BF16)
    for p in range(MLA_HEADS // 2):
        h0, h1 = 2 * p, 2 * p + 1
        o_ref[:, p * LANES:(p + 1) * LANES] = (
            _dot(o_lat[h0 * tq:(h0 + 1) * tq], wuv_ref[h0]) + _dot(o_lat[h1 * tq:(h1 + 1) * tq], wuv_ref[h1])
        ).astype(BF16)


def _mla_prompt_body(qn_ref, qp_ref, ckv_ref, kpe_ref, wuk_ref, wuv_ref, o_ref, qcat_ref, *sc, tq, tk):
    qi = pl.program_id(1)
    _mla_queries(qn_ref[...], qp_ref[...], wuk_ref, qcat_ref, tq)
    _mla_init(sc)

    n_full = (qi * tq) // tk
    col = lax.broadcasted_iota(jnp.int32, (MLA_ROW_CHUNK, tk), 1)

    def visible(c):
        q_chunk_end = qi * tq + ((c * MLA_ROW_CHUNK) % tq) // CHUNK * CHUNK + CHUNK
        return col < q_chunk_end - n_full * tk

    def keys(kj):
        start = pl.multiple_of(kj * tk, tk)
        return jnp.concatenate([ckv_ref[pl.ds(start, tk), :], kpe_ref[pl.ds(start, tk), :],
                                jnp.zeros((tk, MLA_QK - KV_LORA - LANES), BF16)], axis=1)

    def body(kj, c):
        _mla_tile(qcat_ref, keys(kj), sc, None)
        return c

    lax.fori_loop(0, n_full, body, 0)
    _mla_tile(qcat_ref, keys(n_full), sc, visible)
    _mla_finish(sc, wuv_ref, o_ref, tq)


def _mla_scratch(rows, tk):
    return [pltpu.VMEM((rows, MLA_QK), BF16), pltpu.VMEM((rows, tk), F32), pltpu.VMEM((rows, tk), BF16),
            pltpu.VMEM((rows, LANES), F32), pltpu.VMEM((rows, LANES), F32), pltpu.VMEM((rows, KV_LORA), F32)]


def _mla_prompt(qn, qp, ckvb, kpeb, wuk, wuv, *, batch, seq_len, tq, tk):
    assert tq % CHUNK == 0 and CHUNK % MLA_ROW_CHUNK == 0 and seq_len % tq == 0 and seq_len % tk == 0
    n = qn.shape[0]
    nq = seq_len // tq
    width = MLA_HEADS * V_DIM
    return pl.pallas_call(
        functools.partial(_mla_prompt_body, tq=tq, tk=tk),
        grid=(batch, nq),
        in_specs=[pl.BlockSpec((tq, MLA_HEADS * NOPE_DIM), lambda b, i: (b * nq + i, 0)),
                  pl.BlockSpec((tq, MLA_HEADS * ROPE_DIM), lambda b, i: (b * nq + i, 0)),
                  pl.BlockSpec((seq_len, KV_LORA), lambda b, i: (b, 0)),
                  pl.BlockSpec((seq_len, LANES), lambda b, i: (b, 0)),
                  _const_spec(wuk.shape), _const_spec(wuv.shape)],
        out_specs=pl.BlockSpec((tq, width), lambda b, i: (b * nq + i, 0)),
        out_shape=jax.ShapeDtypeStruct((n, width), BF16),
        scratch_shapes=_mla_scratch(MLA_HEADS * tq, tk),
        compiler_params=_cparams(("arbitrary", "arbitrary")), name="mla_attn_prompt")(qn, qp, ckvb, kpeb, wuk, wuv)


def _mla_sample_body(qn_ref, qp_ref, cn_ref, pn_ref, cc_ref, pc_ref, wuk_ref, wuv_ref, o_ref, kcat_ref, qcat_ref,
                     *sc, dec, past, kblk, tk):
    j = pl.program_id(1)
    rows = MLA_HEADS * dec

    @pl.when(j == 0)
    def _():
        _mla_queries(qn_ref[...], qp_ref[...], wuk_ref, qcat_ref, dec)
        _mla_init(sc)
        kcat_ref[:, KV_LORA:] = jnp.zeros((tk, MLA_QK - KV_LORA), BF16)
        new = jnp.concatenate([cn_ref[...], pn_ref[...], jnp.zeros((dec, MLA_QK - KV_LORA - LANES), BF16)], axis=1)
        kcat = jnp.concatenate([new, jnp.zeros((tk - dec, MLA_QK), BF16)], axis=0)

        def visible(c):
            col = lax.broadcasted_iota(jnp.int32, (MLA_ROW_CHUNK, tk), 1)
            row = c * MLA_ROW_CHUNK + lax.broadcasted_iota(jnp.int32, (MLA_ROW_CHUNK, tk), 0)
            return (col < dec) & ((past + col) // CHUNK <= (past + row % dec) // CHUNK)

        _mla_tile(qcat_ref, kcat, sc, visible)

    def body(it, c):
        start = pl.multiple_of(it * tk, tk)
        kcat_ref[:, :KV_LORA] = cc_ref[0, pl.ds(start, tk), :].astype(BF16)
        kp_t = jnp.concatenate([pc_ref[0, :, pl.ds(start, tk)], jnp.zeros((LANES - ROPE_DIM, tk), F32)], axis=0)
        kcat_ref[:, KV_LORA:KV_LORA + LANES] = kp_t.T.astype(BF16)
        _mla_tile(qcat_ref, kcat_ref[...], sc, None)
        return c

    lax.fori_loop(0, kblk // tk, body, 0)

    @pl.when(j == pl.num_programs(1) - 1)
    def _():
        _mla_finish(sc, wuv_ref, o_ref, dec)


def _mla_sample(qn, qp, ckvb, kpeb, cache_ckv, cache_kpe_t, wuk, wuv, *, batch, dec, kblk=1024, tk=1024):
    n = qn.shape[0]
    past = cache_ckv.shape[1]
    width = MLA_HEADS * V_DIM
    return pl.pallas_call(
        functools.partial(_mla_sample_body, dec=dec, past=past, kblk=kblk, tk=tk),
        grid=(batch, past // kblk),
        in_specs=[pl.BlockSpec((dec, MLA_HEADS * NOPE_DIM), lambda b, j: (b, 0)),
                  pl.BlockSpec((dec, MLA_HEADS * ROPE_DIM), lambda b, j: (b, 0)),
                  pl.BlockSpec((dec, KV_LORA), lambda b, j: (b, 0)),
                  pl.BlockSpec((dec, LANES), lambda b, j: (b, 0)),
                  pl.BlockSpec((1, kblk, KV_LORA), lambda b, j: (b, j, 0)),
                  pl.BlockSpec((1, ROPE_DIM, kblk), lambda b, j: (b, 0, j)),
                  _const_spec(wuk.shape), _const_spec(wuv.shape)],
        out_specs=pl.BlockSpec((dec, width), lambda b, j: (b, 0)),
        out_shape=jax.ShapeDtypeStruct((n, width), BF16),
        scratch_shapes=[pltpu.VMEM((tk, MLA_QK), BF16)] + _mla_scratch(MLA_HEADS * dec, tk),
        compiler_params=_cparams(("arbitrary", "arbitrary")), name="mla_attn_sample")(
            qn, qp, ckvb, kpeb, cache_ckv, cache_kpe_t, wuk, wuv)


def _out_ffn_body(a_ref, b_ref, x_ref, woa_ref, wob_ref, gpost_ref, gpre_ref, wup_ref, wdn_ref, gfpost_ref, o_ref):
    tm = x_ref.shape[0]
    groups = [slice(r * tm // FFN_ROW_GROUPS, (r + 1) * tm // FFN_ROW_GROUPS) for r in range(FFN_ROW_GROUPS)]
    mixed = [_dot(a_ref[rs, :], woa_ref[...]) + _dot(b_ref[rs, :], wob_ref[...]) for rs in groups]
    x1 = [x_ref[rs, :] + _rms(m, gpost_ref[...]) for rs, m in zip(groups, mixed)]
    h = [_rms(x, gpre_ref[...]).astype(BF16) for x in x1]
    down = [jnp.zeros_like(x) for x in x1]
    for c in range(D_FF // FF_CHUNK):
        for r in range(FFN_ROW_GROUPS):
            up = _dot(h[r], wup_ref[:, c * FF_CHUNK:(c + 1) * FF_CHUNK])
            act = jnp.square(jnp.maximum(up, 0.0)).astype(BF16)
            down[r] = down[r] + _dot(act, wdn_ref[c * FF_CHUNK:(c + 1) * FF_CHUNK, :])
    for r, rs in enumerate(groups):
        o_ref[rs, :] = x1[r] + _rms(down[r], gfpost_ref[...])


def _out_ffn(a, b, x, woa, wob, g_post, g_pre, w_up, w_down, g_fpost, *, tm):
    n = x.shape[0]
    row_spec = lambda w: pl.BlockSpec((tm, w), lambda i: (i, 0))
    consts = [woa, wob, g_post, g_pre, w_up, w_down, g_fpost]
    return pl.pallas_call(
        _out_ffn_body,
        grid=(n // tm,),
        in_specs=[row_spec(a.shape[1]), row_spec(b.shape[1]), row_spec(D_MODEL)] + [_const_spec(c.shape) for c in consts],
        out_specs=row_spec(D_MODEL),
        out_shape=jax.ShapeDtypeStruct((n, D_MODEL), F32),
        compiler_params=_cparams(("arbitrary",)), name="out_ffn")(a, b, x, *consts)


def _rope_tables(pos, reps):
    half = ROPE_DIM // 2
    inv = ROPE_THETA ** (-jnp.arange(half, dtype=F32) / half)
    ang = pos.astype(F32)[:, None] * inv[None, :]
    cos = jnp.tile(jnp.concatenate([jnp.cos(ang), jnp.cos(ang)], axis=1), (reps, LANES // ROPE_DIM))
    sin = jnp.tile(jnp.concatenate([-jnp.sin(ang), jnp.sin(ang)], axis=1), (reps, LANES // ROPE_DIM))
    return cos, sin


def _prep_even(p, j):
    w_in = p["even_w_in"][j]
    return dict(w_in=w_in.astype(BF16), w_kv_t=w_in[:, SB_WIDTH:3 * SB_WIDTH].T.astype(BF16), w_conv=p["even_w_conv"][j],
                woa=p["even_w_out"][j, :SB_WIDTH].astype(BF16), wob=p["even_w_out"][j, SB_WIDTH:].astype(BF16))


def _prep_odd(p, j, gate_len):
    w_in = p["odd_w_in"][j]
    w_in = jnp.pad(w_in, ((0, 0), (0, ODD_IN_PAD - w_in.shape[1]))).astype(BF16)
    reps = SGU_CHUNK // gate_len
    w_s = jnp.tile(p["sgu_w_s"][j, :, :gate_len, :gate_len], (1, reps, reps))
    b_s = jnp.tile(p["sgu_b_s"][j, :, :gate_len], (1, reps))
    b_s = jnp.repeat(b_s.T, SGU_GROUP_DIM, axis=1)
    w_uq = p["mla_w_uq"][j].reshape(Q_LORA, MLA_HEADS, NOPE_DIM + ROPE_DIM)
    wqn = w_uq[:, :, :NOPE_DIM].reshape(Q_LORA, MLA_HEADS * NOPE_DIM).astype(BF16)
    wqp = w_uq[:, :, NOPE_DIM:].reshape(Q_LORA, MLA_HEADS * ROPE_DIM).astype(BF16)
    wuk = p["mla_w_uk"][j].reshape(MLA_HEADS // 2, 2 * NOPE_DIM, KV_LORA).astype(BF16)
    w_uv = p["mla_w_uv"][j]
    wuv = jnp.stack([jnp.pad(w_uv[h], ((0, 0), ((h % 2) * V_DIM, (1 - h % 2) * V_DIM))) for h in range(MLA_HEADS)])
    return dict(w_in=w_in, ln_g=p["sgu_ln_g"][j][None], ln_b=p["sgu_ln_b"][j][None], w_s=w_s, b_s=b_s,
                qg=p["mla_q_norm_g"][j][None], kvg=p["mla_kv_norm_g"][j][None], wqn=wqn, wqp=wqp, wuk=wuk,
                wuv=wuv.astype(BF16),
                woa=p["odd_w_out"][j, :SGU_DIM].astype(BF16), wob=p["odd_w_out"][j, SGU_DIM:].astype(BF16))


def _run_trunk(x, pos, past, p, *, batch, seq_len):
    depth = p["mix_pre_g"].shape[0]
    n = batch * seq_len
    x = x.reshape(n, D_MODEL)
    is_sample = past is not None
    tm = min(ROW_TILE, n)
    assert n % tm == 0 and (tm % seq_len == 0 or seq_len % tm == 0) and tm % SGU_CHUNK == 0
    gate_len = min(seq_len, SGU_CHUNK)
    tm_proj = PROJ_ROW_TILE if (seq_len % PROJ_ROW_TILE == 0 and seq_len > PROJ_ROW_TILE) else tm
    cos, sin = _rope_tables(pos, max(1, tm_proj // seq_len))
    st = {k: [] for k in ("sb_k", "sb_v", "conv", "ckv", "kpe", "sgu_v")}
    for layer in range(depth):
        j = layer // 2
        g_pre = p["mix_pre_g"][layer][None]
        if layer % 2 == 0:
            w = _prep_even(p, j)
            conv_prev = past["conv"][j] if is_sample else None
            q, k, v, kb, vb, b_mix, conv_state = _proj_even(x, g_pre, w["w_in"], w["w_kv_t"], w["w_conv"], conv_prev,
                                                            batch=batch, seq_len=seq_len, tm=tm_proj)
            if is_sample:
                assert tm % seq_len == 0

                def cache_t(c):
                    return jnp.transpose(c, (0, 2, 3, 1)).reshape(batch, SB_WIDTH, -1)

                a_mix = _sb_sample(q, kb, vb, cache_t(past["sb_k"][j]), cache_t(past["sb_v"][j]),
                                   batch=batch, dec=seq_len)
                k, v = (t.reshape(batch, seq_len, SB_HEADS, SB_HEAD_DIM) for t in (k, v))
            else:
                assert seq_len % tm == 0
                a_mix = _sb_prompt(q, kb, vb, batch=batch, seq_len=seq_len, tq=SB_TQ, tk=SB_TK)
                k, v = (jnp.transpose(t.reshape(batch, SB_HEADS, SB_HEAD_DIM, seq_len), (0, 3, 1, 2)) for t in (k, v))
            st["sb_k"].append(k)
            st["sb_v"].append(v)
            st["conv"].append(conv_state)
        else:
            w = _prep_odd(p, j, gate_len)
            outs = _proj_odd(x, g_pre, w["w_in"], w["ln_g"], w["ln_b"], w["w_s"], w["b_s"], w["qg"], w["kvg"],
                             w["wqn"], w["wqp"], cos, sin, batch=batch, seq_len=seq_len, tm=tm_proj, gate_len=gate_len,
                             emit_vn=is_sample)
            if is_sample:
                a_mix, vn, qn, qp, ckv, ckvb, kpe, kpeb = outs
                st["sgu_v"].append(vn.reshape(batch, seq_len, SGU_DIM))
                b_mix = _mla_sample(qn, qp, ckvb, kpeb, past["ckv"][j], jnp.transpose(past["kpe"][j], (0, 2, 1)),
                                    w["wuk"], w["wuv"],
                                    batch=batch, dec=seq_len)
            else:
                a_mix, qn, qp, ckv, ckvb, kpe, kpeb = outs
                b_mix = _mla_prompt(qn, qp, ckvb, kpeb, w["wuk"], w["wuv"], batch=batch, seq_len=seq_len,
                                    tq=MLA_TQ, tk=MLA_TK)
            st["ckv"].append(ckv.reshape(batch, seq_len, KV_LORA))
            st["kpe"].append(jnp.transpose(kpe, (0, 2, 1)) if kpe.ndim == 3 else kpe.reshape(batch, seq_len, ROPE_DIM))
        x = _out_ffn(a_mix, b_mix, x, w["woa"], w["wob"], p["mix_post_g"][layer][None], p["ffn_pre_g"][layer][None],
                     p["ffn_w_up"][layer].astype(BF16), p["ffn_w_down"][layer].astype(BF16),
                     p["ffn_post_g"][layer][None], tm=tm)
    states = {k: jnp.stack(v) for k, v in st.items() if v}
    return x.reshape(batch, seq_len, D_MODEL), states


def kernel(x_prompt, x_sample, cache_sb_k, cache_sb_v, state_conv, cache_mla_ckv, cache_mla_kpe,
           mix_pre_g, mix_post_g, ffn_pre_g, ffn_post_g, even_w_in, even_w_conv, even_w_out,
           odd_w_in, sgu_ln_g, sgu_ln_b, sgu_w_s, sgu_b_s, mla_q_norm_g, mla_kv_norm_g,
           mla_w_uq, mla_w_uk, mla_w_uv, odd_w_out, ffn_w_up, ffn_w_down):
    params = {
        "mix_pre_g": mix_pre_g, "mix_post_g": mix_post_g, "ffn_pre_g": ffn_pre_g, "ffn_post_g": ffn_post_g,
        "even_w_in": even_w_in, "even_w_conv": even_w_conv, "even_w_out": even_w_out,
        "odd_w_in": odd_w_in, "sgu_ln_g": sgu_ln_g, "sgu_ln_b": sgu_ln_b, "sgu_w_s": sgu_w_s,
        "sgu_b_s": sgu_b_s, "mla_q_norm_g": mla_q_norm_g, "mla_kv_norm_g": mla_kv_norm_g,
        "mla_w_uq": mla_w_uq, "mla_w_uk": mla_w_uk, "mla_w_uv": mla_w_uv, "odd_w_out": odd_w_out,
        "ffn_w_up": ffn_w_up, "ffn_w_down": ffn_w_down,
    }
    batch, seq_len, _ = x_prompt.shape
    pos_p = jnp.arange(seq_len, dtype=jnp.int32)
    y_prompt, st_p = _run_trunk(x_prompt, pos_p, None, params, batch=batch, seq_len=seq_len)
    dec_batch, dec_seq, _ = x_sample.shape
    past_len = cache_sb_k.shape[2]
    pos_s = past_len + jnp.arange(dec_seq, dtype=jnp.int32)
    past = {"sb_k": cache_sb_k, "sb_v": cache_sb_v, "conv": state_conv, "ckv": cache_mla_ckv, "kpe": cache_mla_kpe}
    y_sample, st_s = _run_trunk(x_sample, pos_s, past, params, batch=dec_batch, seq_len=dec_seq)
    return (y_prompt, y_sample,
            st_p["sb_k"], st_p["sb_v"], st_p["conv"], st_p["ckv"], st_p["kpe"],
            st_s["sb_k"], st_s["sb_v"], st_s["conv"], st_s["ckv"], st_s["kpe"], st_s["sgu_v"])
```

```python
import functools
import math

import numpy as np
import jax
import jax.numpy as jnp
from jax import lax
from jax.experimental import pallas as pl
from jax.experimental.pallas import tpu as pltpu

F32 = jnp.float32
BF16 = jnp.bfloat16

EPS = 1e-6
D_MODEL = 1024
CHUNK = 64
SB_HEADS = 8
SB_HEAD_DIM = 64
SB_WIDTH = SB_HEADS * SB_HEAD_DIM
SB_SCALE = 1.0 / math.sqrt(SB_HEAD_DIM)
CONV_DIM = D_MODEL // 2
CONV_W = 3
SGU_CHUNK = 128
SGU_GROUPS = 4
SGU_DIM = D_MODEL // 2
SGU_GROUP_DIM = SGU_DIM // SGU_GROUPS
MLA_HEADS = 8
Q_LORA = 384
KV_LORA = 256
NOPE_DIM = 64
ROPE_DIM = 32
V_DIM = 64
ROPE_THETA = 10000.0
MLA_SCALE = 1.0 / math.sqrt(NOPE_DIM + ROPE_DIM)
D_FF = 4 * D_MODEL
FF_CHUNK = 1024
FFN_ROW_GROUPS = 2

LANES = 128
HEAD_PAIRS = SB_HEADS // 2
MLA_QK = 2 * KV_LORA
ODD_IN_PAD =2 * SGU_DIM + Q_LORA + KV_LORA + LANES
VMEM_LIMIT = 56 * 1024 * 1024
NEG_INF = -1e30
LOG2_E = math.log2(math.e)

ROW_TILE = 512
PROJ_ROW_TILE = 1024
SB_TQ, SB_TK = 1024, 128
MLA_TQ, MLA_TK = 512, 512
MLA_ROW_CHUNK = 64


def _cparams(sem):
    return pltpu.CompilerParams(dimension_semantics=sem, vmem_limit_bytes=VMEM_LIMIT)


def _const_spec(shape):
    nd = len(shape)
    return pl.BlockSpec(shape, lambda *_: (0,) * nd, pipeline_mode=pl.Buffered(1))


def _rms(x, g):
    return x * lax.rsqrt(jnp.mean(x * x, axis=-1, keepdims=True) + EPS) * g


def _dot(a, b):
    return jnp.dot(a, b, preferred_element_type=F32)


def _dot_nt(a, b):
    return lax.dot_general(a, b, (((1,), (1,)), ((), ())), preferred_element_type=F32)


def _log_sigmoid(z):
    neg_abs = lax.bitcast_convert_type(lax.bitcast_convert_type(z, jnp.uint32) | jnp.uint32(0x80000000), F32)
    return jnp.minimum(z, 0.0) - jnp.log(1.0 + jnp.exp(neg_abs))


def _split_hi_lo(x):
    hi = x.astype(BF16)
    lo = (x - hi.astype(F32)).astype(BF16)
    return hi, lo


def _rope_block(x, cos, sin):
    half = ROPE_DIM // 2
    lane = lax.broadcasted_iota(jnp.int32, x.shape, 1)
    partner = jnp.where(lane % ROPE_DIM < half, pltpu.roll(x, LANES - half, 1), pltpu.roll(x, half, 1))
    return x * cos + partner * sin


def _proj_even_body(*refs, tm, seq_len, whole_seqs):
    if whole_seqs:
        (x_ref, g_ref, w_ref, wc_ref, e1_ref, e2_ref,
         q_ref, k_ref, v_ref, kb_ref, vb_ref, gc_ref, ci_ref) = refs
    else:
        (x_ref, g_ref, w_ref, wc_ref, wkv_ref,
         q_ref, k_ref, v_ref, kb_ref, vb_ref, gc_ref, cs_ref, tail_ref) = refs
    if not whole_seqs:
        @pl.when((pl.program_id(0) % (seq_len // tm)) == 0)
        def _():
            tail_ref[...] = jnp.zeros_like(tail_ref)

    h = _rms(x_ref[...], g_ref[...]).astype(BF16)

    def proj(j):
        return _dot(h, w_ref[:, j * SB_WIDTH:(j + 1) * SB_WIDTH])

    g_post = proj(3)
    ci = proj(4) * proj(5)
    r1 = pltpu.roll(ci, 1, 0)
    r2 = pltpu.roll(ci, 2, 0)
    row = lax.broadcasted_iota(jnp.int32, (tm, 1), 0)
    if whole_seqs:
        tpos = row % seq_len
        s1 = jnp.where(tpos < 1, e1_ref[...], r1)
        s2 = jnp.where(tpos < 2, e2_ref[...], r2)
        ci_ref[...] = ci
    else:
        t1 = tail_ref[7:8, :]
        t2 = tail_ref[6:7, :]
        s1 = jnp.where(row == 0, t1, r1)
        s2 = jnp.where(row == 0, t2, jnp.where(row == 1, t1, r2))
        tail_ref[...] = ci[tm - 8:, :]
        cs_ref[0] = ci[tm - (CONV_W - 1):, :]
    conv = wc_ref[0:1, :] * s2 + wc_ref[1:2, :] * s1 + wc_ref[2:3, :] * ci
    gc_ref[...] = (g_post * conv).astype(BF16)

    q_ref[...] = (proj(0) * SB_SCALE).astype(BF16)
    if whole_seqs:
        k = proj(1)
        v = proj(2)
        k_ref[...] = k
        v_ref[...] = v
        kb_ref[...] = k.astype(BF16)
        vb_ref[...] = v.astype(BF16)
    else:
        k = _dot_nt(wkv_ref[:SB_WIDTH, :], h)
        v = _dot_nt(wkv_ref[SB_WIDTH:, :], h)
        k_ref[0] = k
        v_ref[0] = v
        kb_ref[0] = k.astype(BF16)
        vb_ref[0] = v.astype(BF16)


def _proj_even(x, g, w_in, w_kv_t, w_conv, conv_prev, *, batch, seq_len, tm):
    n = x.shape[0]
    whole = tm % seq_len == 0
    grid = (n // tm,)
    row_spec = lambda w: pl.BlockSpec((tm, w), lambda i: (i, 0))
    in_specs = [row_spec(D_MODEL), _const_spec((1, D_MODEL)), _const_spec(w_in.shape), _const_spec(w_conv.shape)]
    args = [x, g, w_in, w_conv]
    if whole:
        kv_shape, kv_spec = (n, SB_WIDTH), row_spec(SB_WIDTH)
    else:
        seq_tiles = seq_len // tm
        kv_shape = (batch, SB_WIDTH, seq_len)
        kv_spec = pl.BlockSpec((1, SB_WIDTH, tm), lambda i: (i // seq_tiles, 0, i % seq_tiles))
    out_shape = [jax.ShapeDtypeStruct((n, SB_WIDTH), BF16), jax.ShapeDtypeStruct(kv_shape, F32),
                 jax.ShapeDtypeStruct(kv_shape, F32), jax.ShapeDtypeStruct(kv_shape, BF16),
                 jax.ShapeDtypeStruct(kv_shape, BF16), jax.ShapeDtypeStruct((n, CONV_DIM), BF16)]
    out_specs = [row_spec(SB_WIDTH)] + [kv_spec] * 4 + [row_spec(CONV_DIM)]
    scratch = []
    if whole:
        if conv_prev is None:
            conv_prev = jnp.zeros((batch, CONV_W - 1, CONV_DIM), F32)
        e1 =jnp.zeros((batch, seq_len, CONV_DIM), F32).at[:, 0].set(conv_prev[:, 1])
        e2 = jnp.zeros((batch, seq_len, CONV_DIM), F32).at[:, 0].set(conv_prev[:, 0]).at[:, 1].set(conv_prev[:, 1])
        args += [e1.reshape(n, CONV_DIM), e2.reshape(n, CONV_DIM)]
        in_specs += [row_spec(CONV_DIM), row_spec(CONV_DIM)]
        out_shape.append(jax.ShapeDtypeStruct((n, CONV_DIM), F32))
        out_specs.append(row_spec(CONV_DIM))
    else:
        assert seq_len % tm == 0 and conv_prev is None
        args.append(w_kv_t)
        in_specs.append(_const_spec(w_kv_t.shape))
        out_shape.append(jax.ShapeDtypeStruct((batch, CONV_W - 1, CONV_DIM), F32))
        out_specs.append(pl.BlockSpec((1, CONV_W - 1, CONV_DIM), lambda i: (i // seq_tiles, 0, 0)))
        scratch.append(pltpu.VMEM((8, CONV_DIM), F32))
    outs = pl.pallas_call(
        functools.partial(_proj_even_body, tm=tm, seq_len=seq_len, whole_seqs=whole),
        grid=grid, in_specs=in_specs, out_specs=out_specs, out_shape=out_shape, scratch_shapes=scratch,
        compiler_params=_cparams(("arbitrary",)), name="proj_even")(*args)
    q, k, v, kb, vb, gc, last = outs
    if whole:
        last = last.reshape(batch, seq_len, CONV_DIM)[:, seq_len - (CONV_W - 1):]
    return q, k, v, kb, vb, gc, last


def _suffix_rhs(tk):
    j = np.arange(2 * tk)[:, None] % tk
    c = np.arange(2 * tk)[None, :]
    return jnp.asarray(np.where(c < tk, j > c, True), dtype=BF16)


def _sb_prompt_body(q_ref, k_ref, v_ref, r_ref, o_ref, carry_ref, acc_ref, *, tq, tk):
    qi = pl.program_id(1)
    ratio = tq // tk
    even = lax.broadcasted_iota(jnp.int32, (LANES, tk), 0) < SB_HEAD_DIM
    rhs = r_ref[...]
    carry_ref[...] = jnp.zeros_like(carry_ref)
    acc_ref[...] = jnp.zeros_like(acc_ref)
    qs = [q_ref[:, p * LANES:(p + 1) * LANES] for p in range(HEAD_PAIRS)]

    def block_diag(blk):
        zero = jnp.zeros_like(blk)
        return jnp.concatenate([jnp.where(even, blk, zero), jnp.where(even, zero, blk)], axis=1)

    def tile(kj, masked, r0=0):
        start = pl.multiple_of(kj * tk, tk)
        if masked:
            q_pos = qi * tq + r0 + lax.broadcasted_iota(jnp.int32, (tq - r0, tk), 0)
            k_pos = kj * tk + lax.broadcasted_iota(jnp.int32, (tq - r0, tk), 1)
            visible = k_pos < q_pos
        zs = [_dot(qs[p][r0:], block_diag(k_ref[0, p * LANES:(p + 1) * LANES, pl.ds(start, tk)]))
              for p in range(HEAD_PAIRS)]
        log_betas, splits = [], []
        for hd in range(SB_HEADS):
            z = zs[hd // 2][:, (hd % 2) * tk:(hd % 2 + 1) * tk]
            log_beta = _log_sigmoid(z)
            log_1m = log_beta - z
            if masked:
                log_1m = jnp.where(visible, log_1m, 0.0)
            hi, lo = _split_hi_lo(log_1m)
            log_betas.append(log_beta)
            splits.append(jnp.concatenate([hi, lo], axis=1))
        sums = [_dot(sp, rhs) for sp in splits]
        ws = []
        for hd in range(SB_HEADS):
            carry = carry_ref[hd, r0:, :]
            w = jnp.exp(log_betas[hd] + sums[hd][:, :tk] + carry)
            if masked:
                w = jnp.where(visible, w, 0.0)
            carry_ref[hd, r0:, :] = carry + sums[hd][:, tk:]
            ws.append(w.astype(BF16))
        for p in range(HEAD_PAIRS):
            vbd = block_diag(v_ref[0, p * LANES:(p + 1) * LANES, pl.ds(start, tk)])
            acc_ref[p, r0:, :] += _dot_nt(jnp.concatenate([ws[2 * p], ws[2 * p + 1]], axis=1), vbd)

    for t in reversed(range(ratio)):
        tile(qi * ratio + t, True, r0=t * tk)

    def body(it, c):
        tile(qi * ratio - 1 - 2 * it, False)
        tile(qi * ratio - 2 - 2 * it, False)
        return c

    lax.fori_loop(0, qi * (ratio // 2), body, 0)
    for p in range(HEAD_PAIRS):
        o_ref[:, p * LANES:(p + 1) * LANES] = acc_ref[p].astype(BF16)


def _sb_prompt(q, kb, vb, *, batch, seq_len, tq, tk):
    assert (tq // tk) % 2 == 0 and tq % tk == 0 and seq_len % tq == 0
    n = q.shape[0]
    nq = seq_len // tq
    rhs = _suffix_rhs(tk)
    q_spec = pl.BlockSpec((tq, SB_WIDTH), lambda b, i: (b * nq + i, 0))
    kv_spec = pl.BlockSpec((1, SB_WIDTH, seq_len), lambda b, i: (b, 0, 0))
    return pl.pallas_call(
        functools.partial(_sb_prompt_body, tq=tq, tk=tk),
        grid=(batch, nq),
        in_specs=[q_spec, kv_spec, kv_spec, _const_spec(rhs.shape)],
        out_specs=q_spec,
        out_shape=jax.ShapeDtypeStruct((n, SB_WIDTH), BF16),
        scratch_shapes=[pltpu.VMEM((SB_HEADS, tq, LANES), F32), pltpu.VMEM((HEAD_PAIRS, tq, LANES), F32)],
        compiler_params=_cparams(("arbitrary", "arbitrary")), name="sb_attn_prompt")(q, kb, vb, rhs)


def _sb_sample_body(q_ref, kn_ref, vn_ref, kc_ref, vc_ref, l_ref, o_ref, qh_ref, carry_ref, acc_ref,
                    *, dec, kblk, tk):
    j = pl.program_id(1)
    rows = SB_HEADS * dec
    rhs = l_ref[...]

    def tiles(kts, vts, visible):
        log_betas, sums = [], []
        for kt in kts:
            z = _dot(qh_ref[...], kt)
            log_beta = _log_sigmoid(z)
            log_1m = log_beta - z
            if visible is not None:
                log_1m = jnp.where(visible, log_1m, 0.0)
            hi, lo = _split_hi_lo(log_1m)
            log_betas.append(log_beta)
            sums.append(_dot(jnp.concatenate([hi, lo], axis=1), rhs))
        carry = carry_ref[...]
        acc = acc_ref[...]
        for log_beta, sm, vt in zip(log_betas, sums, vts):
            w = jnp.exp(log_beta + sm[:, :tk] + carry)
            if visible is not None:
                w = jnp.where(visible, w, 0.0)
            acc = acc + _dot_nt(w.astype(BF16), vt)
            carry = carry + sm[:, tk:]
        carry_ref[...] = carry
        acc_ref[...] = acc

    def new_keys_t(x):
        x = jnp.concatenate([x.astype(F32), jnp.zeros((tk - dec, SB_WIDTH), F32)], axis=0)
        return x.T.astype(BF16)

    @pl.when(j == 0)
    def _():
        q = q_ref[...]
        qt = jnp.concatenate([q] * SB_HEADS, axis=0)
        rr = lax.broadcasted_iota(jnp.int32, (rows, SB_WIDTH), 0)
        cc = lax.broadcasted_iota(jnp.int32, (rows, SB_WIDTH), 1)
        qh_ref[...] = jnp.where(rr // dec == cc // SB_HEAD_DIM, qt, jnp.zeros_like(qt))
        carry_ref[...] = jnp.zeros_like(carry_ref)
        acc_ref[...] = jnp.zeros_like(acc_ref)
        query_t = lax.broadcasted_iota(jnp.int32, (rows, tk), 0) % dec
        key_i = lax.broadcasted_iota(jnp.int32, (rows, tk), 1)
        tiles([new_keys_t(kn_ref[...])], [new_keys_t(vn_ref[...])],
              key_i < query_t)

    order = list(reversed(range(kblk // tk)))
    tiles([kc_ref[0, :, s * tk:(s + 1) * tk].astype(BF16) for s in order],
          [vc_ref[0, :, s * tk:(s + 1) * tk].astype(BF16) for s in order], None)

    @pl.when(j == pl.num_programs(1) - 1)
    def _():
        cc = lax.broadcasted_iota(jnp.int32, (dec, SB_WIDTH), 1)
        out = jnp.zeros((dec, SB_WIDTH), F32)
        for h in range(SB_HEADS):
            out = out + jnp.where(cc // SB_HEAD_DIM == h, acc_ref[h * dec:(h + 1) * dec, :], 0.0)
        o_ref[...] = out.astype(BF16)


def _sb_sample(q, kb, vb, cache_kt, cache_vt, *, batch, dec, kblk=2048, tk=256):
    n = q.shape[0]
    past = cache_kt.shape[2]
    nkb = past // kblk
    rows = SB_HEADS * dec
    rhs = _suffix_rhs(tk)
    new_spec = pl.BlockSpec((dec, SB_WIDTH), lambda b, j: (b, 0))
    cache_spec = pl.BlockSpec((1, SB_WIDTH, kblk), lambda b, j: (b, 0, nkb - 1 - j))
    return pl.pallas_call(
        functools.partial(_sb_sample_body, dec=dec, kblk=kblk, tk=tk),
        grid=(batch, nkb),
        in_specs=[new_spec, new_spec, new_spec, cache_spec, cache_spec, _const_spec(rhs.shape)],
        out_specs=new_spec,
        out_shape=jax.ShapeDtypeStruct((n, SB_WIDTH), BF16),
        scratch_shapes=[pltpu.VMEM((rows, SB_WIDTH), BF16), pltpu.VMEM((rows, tk), F32),
                        pltpu.VMEM((rows, SB_WIDTH), F32)],
        compiler_params=_cparams(("arbitrary", "arbitrary")), name="sb_attn_sample")(
            q, kb, vb, cache_kt, cache_vt, rhs)


def _proj_odd_body(*refs, tm, gate_len, emit_vn, kpe_transposed):
    (x_ref, g_ref, w_ref, lng_ref, lnb_ref, ws_ref, bs_ref, qg_ref, kvg_ref, wqn_ref, wqp_ref,
     cos_ref, sin_ref) = refs[:13]
    outs = refs[13:]
    if emit_vn:
        sgu_ref, vn_ref, qn_ref, qp_ref, ckv_ref, ckvb_ref, kpe_ref, kpeb_ref = outs
    else:
        sgu_ref, qn_ref, qp_ref, ckv_ref, ckvb_ref, kpe_ref, kpeb_ref = outs
    h = _rms(x_ref[...], g_ref[...]).astype(BF16)
    o_v, o_q, o_kv, o_pe = SGU_DIM, 2 * SGU_DIM, 2 * SGU_DIM + Q_LORA, 2 * SGU_DIM + Q_LORA + KV_LORA
    cos = cos_ref[...]
    sin = sin_ref[...]
    kpe = _rope_block(_dot(h, w_ref[:, o_pe:]), cos, sin)
    if kpe_transposed:
        kpe_ref[0] = kpe.T[:ROPE_DIM, :]
    else:
        kpe_ref[...] = kpe[:, :ROPE_DIM]
    kpeb_ref[...] = kpe.astype(BF16)
    ckv = _rms(_dot(h, w_ref[:, o_kv:o_pe]), kvg_ref[...])
    ckv_ref[...] = ckv
    ckvb_ref[...] = ckv.astype(BF16)
    cq = _rms(_dot(h, w_ref[:, o_q:o_kv]), qg_ref[...]).astype(BF16)
    for blk in range(MLA_HEADS * ROPE_DIM // LANES):
        bs = slice(blk * LANES, (blk + 1) * LANES)
        qp_ref[:, bs] = _rope_block(_dot(cq, wqp_ref[:, bs]), cos, sin).astype(BF16)
    qn_ref[...] = _dot(cq, wqn_ref[...]).astype(BF16)
    u = _dot(h, w_ref[:, :o_v])
    v = _dot(h, w_ref[:, o_v:o_q])
    mu = jnp.mean(v, axis=-1, keepdims=True)
    vc = v - mu
    var = jnp.mean(vc * vc, axis=-1, keepdims=True)
    vn = vc * lax.rsqrt(var + EPS) * lng_ref[...] + lnb_ref[...]
    if emit_vn:
        vn_ref[...] = vn
    vnb = vn.astype(BF16)
    rr = lax.broadcasted_iota(jnp.int32, (SGU_CHUNK, SGU_CHUNK), 0)
    cc = lax.broadcasted_iota(jnp.int32, (SGU_CHUNK, SGU_CHUNK), 1)
    causal = (rr // gate_len == cc // gate_len) & (cc <= rr)
    for g in range(SGU_GROUPS):
        gs = slice(g * SGU_GROUP_DIM, (g + 1) * SGU_GROUP_DIM)
        wg = jnp.where(causal, ws_ref[g], 0.0).astype(BF16)
        for c in range(tm // SGU_CHUNK):
            rs = slice(c * SGU_CHUNK, (c + 1) * SGU_CHUNK)
            s = _dot(wg, vnb[rs, gs]) + bs_ref[:, gs]
            sgu_ref[rs, gs] = (u[rs, gs] * s).astype(BF16)


def _proj_odd(x, g, w_in, ln_g, ln_b, w_s, b_s, qg, kvg, wqn, wqp, cos, sin, *, batch, seq_len, tm, gate_len,
              emit_vn):
    n = x.shape[0]
    pos_tiles = cos.shape[0] // tm
    kpe_transposed = tm % seq_len != 0
    if kpe_transposed:
        seq_tiles = seq_len // tm
        kpe_shape = (batch, ROPE_DIM, seq_len)
        kpe_spec = pl.BlockSpec((1, ROPE_DIM, tm), lambda i: (i // seq_tiles, 0, i % seq_tiles))
    else:
        kpe_shape, kpe_spec = (n, ROPE_DIM), pl.BlockSpec((tm, ROPE_DIM), lambda i: (i, 0))
    row_spec = lambda w: pl.BlockSpec((tm, w), lambda i: (i, 0))
    tab_spec = pl.BlockSpec((tm, LANES), lambda i: (i % pos_tiles, 0))
    consts = [g, w_in, ln_g, ln_b, w_s, b_s, qg, kvg, wqn, wqp]
    out_shape = [jax.ShapeDtypeStruct((n, SGU_DIM), BF16)]
    out_specs = [row_spec(SGU_DIM)]
    if emit_vn:
        out_shape.append(jax.ShapeDtypeStruct((n, SGU_DIM), F32))
        out_specs.append(row_spec(SGU_DIM))
    out_shape += [jax.ShapeDtypeStruct((n, MLA_HEADS * NOPE_DIM), BF16),
                  jax.ShapeDtypeStruct((n, MLA_HEADS * ROPE_DIM), BF16),
                  jax.ShapeDtypeStruct((n, KV_LORA), F32), jax.ShapeDtypeStruct((n, KV_LORA), BF16),
                  jax.ShapeDtypeStruct(kpe_shape, F32), jax.ShapeDtypeStruct((n, LANES), BF16)]
    out_specs += [row_spec(MLA_HEADS * NOPE_DIM), row_spec(MLA_HEADS * ROPE_DIM),
                  row_spec(KV_LORA), row_spec(KV_LORA), kpe_spec, row_spec(LANES)]
    return pl.pallas_call(
        functools.partial(_proj_odd_body, tm=tm, gate_len=gate_len, emit_vn=emit_vn, kpe_transposed=kpe_transposed),
        grid=(n // tm,),
        in_specs=[row_spec(D_MODEL)] + [_const_spec(c.shape) for c in consts] + [tab_spec, tab_spec],
        out_specs=out_specs, out_shape=out_shape,
        compiler_params=_cparams(("arbitrary",)), name="proj_odd")(x, *consts, cos, sin)


def _mla_queries(qn, qpe, wuk_ref, qcat_ref, tq):
    lane = lax.broadcasted_iota(jnp.int32, (tq, LANES), 1)
    low = lane < NOPE_DIM
    qpe = qpe.astype(F32)
    qcat_ref[:, KV_LORA:] = jnp.zeros((MLA_HEADS * tq, MLA_QK - KV_LORA), BF16)
    for hd in range(MLA_HEADS):
        p = hd // 2
        rs = slice(hd * tq, (hd + 1) * tq)
        pair = qn[:, p * LANES:(p + 1) * LANES]
        qm = jnp.where(low if hd % 2 == 0 else jnp.logical_not(low), pair, jnp.zeros_like(pair))
        qcat_ref[rs, :KV_LORA] = _dot(qm, wuk_ref[p]).astype(BF16)
        qcat_ref[rs, KV_LORA:KV_LORA + ROPE_DIM] = qpe[:, hd * ROPE_DIM:(hd + 1) * ROPE_DIM].astype(BF16)


def _lane_tile(x, width):
    return jnp.concatenate([x] * (width // LANES), axis=1)


def _mla_scores(qcat_ref, kcat, sc):
    sc[0][...] = _dot_nt(qcat_ref[...], kcat)


def _mla_softmax(sc, visible_fn):
    s_ref, p_ref, m_ref, l_ref, acc_ref = sc
    rows, tk = s_ref.shape
    for c in range(rows // MLA_ROW_CHUNK):
        rs = slice(c * MLA_ROW_CHUNK, (c + 1) * MLA_ROW_CHUNK)
        s = s_ref[rs, :] * (MLA_SCALE * LOG2_E)
        if visible_fn is not None:
            s = jnp.where(visible_fn(c), s, NEG_INF)
        m_prev = m_ref[rs, :]
        m_new = jnp.maximum(m_prev, jnp.max(s, axis=-1, keepdims=True))
        alpha = jnp.exp2(m_prev - m_new)
        p = jnp.exp2(s - _lane_tile(m_new, tk))
        l_ref[rs, :] = alpha * l_ref[rs, :] + jnp.sum(p, axis=-1, keepdims=True)
        m_ref[rs, :] = m_new
        p_ref[rs, :] = p.astype(BF16)
        acc_ref[rs, :] = acc_ref[rs, :] * _lane_tile(alpha, KV_LORA)


def _mla_pv(ck, sc):
    sc[4][...] += _dot(sc[1][...], ck)


def _mla_tile(qcat_ref, kcat, sc, visible_fn):
    _mla_scores(qcat_ref, kcat, sc)
    _mla_softmax(sc, visible_fn)
    _mla_pv(kcat[:, :KV_LORA], sc)


def _mla_init(sc):
    m_ref, l_ref, acc_ref = sc[2], sc[3], sc[4]
    m_ref[...] = jnp.full_like(m_ref, -jnp.inf)
    l_ref[...] = jnp.zeros_like(l_ref)
    acc_ref[...] = jnp.zeros_like(acc_ref)


def _mla_finish(sc, wuv_ref, o_ref, tq):
    l_ref, acc_ref = sc[3], sc[4]
    o_lat = (acc_ref[...] / _lane_tile(l_ref[...], KV_LORA)).astype(BF16)
    for p in range(MLA_HEADS // 2):
        h0, h1 = 2 * p, 2 * p + 1
        o_ref[:, p * LANES:(p + 1) * LANES] = (
            _dot(o_lat[h0 * tq:(h0 + 1) * tq], wuv_ref[h0]) + _dot(o_lat[h1 * tq:(h1 + 1) * tq], wuv_ref[h1])
        ).astype(BF16)


def _mla_prompt_body(qn_ref, qp_ref, ckv_ref, kpe_ref, wuk_ref, wuv_ref, o_ref, qcat_ref, *sc, tq, tk):
    qi = pl.program_id(1)
    _mla_queries(qn_ref[...], qp_ref[...], wuk_ref, qcat_ref, tq)
    _mla_init(sc)

    n_full = (qi * tq) // tk
    col = lax.broadcasted_iota(jnp.int32, (MLA_ROW_CHUNK, tk), 1)

    def visible(c):
        q_chunk_end = qi * tq + ((c * MLA_ROW_CHUNK) % tq) // CHUNK * CHUNK + CHUNK
        return col < q_chunk_end - n_full * tk

    def keys(kj):
        start = pl.multiple_of(kj * tk, tk)
        return jnp.concatenate([ckv_ref[pl.ds(start, tk), :], kpe_ref[pl.ds(start, tk), :],
                                jnp.zeros((tk, MLA_QK - KV_LORA - LANES), BF16)], axis=1)

    def body(kj, c):
        _mla_tile(qcat_ref, keys(kj), sc, None)
        return c

    lax.fori_loop(0, n_full, body, 0)
    _mla_tile(qcat_ref, keys(n_full), sc, visible)
    _mla_finish(sc, wuv_ref, o_ref, tq)


def _mla_scratch(rows, tk):
    return [pltpu.VMEM((rows, MLA_QK), BF16), pltpu.VMEM((rows, tk), F32), pltpu.VMEM((rows, tk), BF16),
            pltpu.VMEM((rows, LANES), F32), pltpu.VMEM((rows, LANES), F32), pltpu.VMEM((rows, KV_LORA), F32)]


def _mla_prompt(qn, qp, ckvb, kpeb, wuk, wuv, *, batch, seq_len, tq, tk):
    assert tq % CHUNK == 0 and CHUNK % MLA_ROW_CHUNK == 0 and seq_len % tq == 0 and seq_len % tk == 0
    n = qn.shape[0]
    nq = seq_len // tq
    width = MLA_HEADS * V_DIM
    return pl.pallas_call(
        functools.partial(_mla_prompt_body, tq=tq, tk=tk),
        grid=(batch, nq),
        in_specs=[pl.BlockSpec((tq, MLA_HEADS * NOPE_DIM), lambda b, i: (b * nq + i, 0)),
                  pl.BlockSpec((tq, MLA_HEADS * ROPE_DIM), lambda b, i: (b * nq + i, 0)),
                  pl.BlockSpec((seq_len, KV_LORA), lambda b, i: (b, 0)),
                  pl.BlockSpec((seq_len, LANES), lambda b, i: (b, 0)),
                  _const_spec(wuk.shape), _const_spec(wuv.shape)],
        out_specs=pl.BlockSpec((tq, width), lambda b, i: (b * nq + i, 0)),
        out_shape=jax.ShapeDtypeStruct((n, width), BF16),
        scratch_shapes=_mla_scratch(MLA_HEADS * tq, tk),
        compiler_params=_cparams(("arbitrary", "arbitrary")), name="mla_attn_prompt")(qn, qp, ckvb, kpeb, wuk, wuv)


def _mla_sample_body(qn_ref, qp_ref, cn_ref, pn_ref, cc_ref, pc_ref, wuk_ref, wuv_ref, o_ref, kcat_ref, qcat_ref,
                     *sc, dec, past, kblk, tk):
    j = pl.program_id(1)
    rows = MLA_HEADS * dec

    @pl.when(j == 0)
    def _():
        _mla_queries(qn_ref[...], qp_ref[...], wuk_ref, qcat_ref, dec)
        _mla_init(sc)
        kcat_ref[:, KV_LORA:] = jnp.zeros((tk, MLA_QK - KV_LORA), BF16)
        new = jnp.concatenate([cn_ref[...], pn_ref[...], jnp.zeros((dec, MLA_QK - KV_LORA - LANES), BF16)], axis=1)
        kcat = jnp.concatenate([new, jnp.zeros((tk - dec, MLA_QK), BF16)], axis=0)

        def visible(c):
            col = lax.broadcasted_iota(jnp.int32, (MLA_ROW_CHUNK, tk), 1)
            row = c * MLA_ROW_CHUNK + lax.broadcasted_iota(jnp.int32, (MLA_ROW_CHUNK, tk), 0)
            return (col < dec) & ((past + col) // CHUNK <= (past + row % dec) // CHUNK)

        _mla_tile(qcat_ref, kcat, sc, visible)

    def body(it, c):
        start = pl.multiple_of(it * tk, tk)
        kcat_ref[:, :KV_LORA] = cc_ref[0, pl.ds(start, tk), :].astype(BF16)
        kp_t = jnp.concatenate([pc_ref[0, :, pl.ds(start, tk)], jnp.zeros((LANES - ROPE_DIM, tk), F32)], axis=0)
        kcat_ref[:, KV_LORA:KV_LORA + LANES] = kp_t.T.astype(BF16)
        _mla_tile(qcat_ref, kcat_ref[...], sc, None)
        return c

    lax.fori_loop(0, kblk // tk, body, 0)

    @pl.when(j == pl.num_programs(1) - 1)
    def _():
        _mla_finish(sc, wuv_ref, o_ref, dec)


def _mla_sample(qn, qp, ckvb, kpeb, cache_ckv, cache_kpe_t, wuk, wuv, *, batch, dec, kblk=2048, tk=1024):
    n = qn.shape[0]
    past = cache_ckv.shape[1]
    width = MLA_HEADS * V_DIM
    return pl.pallas_call(
        functools.partial(_mla_sample_body, dec=dec, past=past, kblk=kblk, tk=tk),
        grid=(batch, past // kblk),
        in_specs=[pl.BlockSpec((dec, MLA_HEADS * NOPE_DIM), lambda b, j: (b, 0)),
                  pl.BlockSpec((dec, MLA_HEADS * ROPE_DIM), lambda b, j: (b, 0)),
                  pl.BlockSpec((dec, KV_LORA), lambda b, j: (b, 0)),
                  pl.BlockSpec((dec, LANES), lambda b, j: (b, 0)),
                  pl.BlockSpec((1, kblk, KV_LORA), lambda b, j: (b, j, 0)),
                  pl.BlockSpec((1, ROPE_DIM, kblk), lambda b, j: (b, 0, j)),
                  _const_spec(wuk.shape), _const_spec(wuv.shape)],
        out_specs=pl.BlockSpec((dec, width), lambda b, j: (b, 0)),
        out_shape=jax.ShapeDtypeStruct((n, width), BF16),
        scratch_shapes=[pltpu.VMEM((tk, MLA_QK), BF16)] + _mla_scratch(MLA_HEADS * dec, tk),
        compiler_params=_cparams(("arbitrary", "arbitrary")), name="mla_attn_sample")(
            qn, qp, ckvb, kpeb, cache_ckv, cache_kpe_t, wuk, wuv)


def _out_ffn_body(a_ref, b_ref, x_ref, woa_ref, wob_ref, gpost_ref, gpre_ref, wup_ref, wdn_ref, gfpost_ref, o_ref):
    tm = x_ref.shape[0]
    groups = [slice(r * tm // FFN_ROW_GROUPS, (r + 1) * tm // FFN_ROW_GROUPS) for r in range(FFN_ROW_GROUPS)]
    mixed = [_dot(a_ref[rs, :], woa_ref[...]) + _dot(b_ref[rs, :], wob_ref[...]) for rs in groups]
    x1 = [x_ref[rs, :] + _rms(m, gpost_ref[...]) for rs, m in zip(groups, mixed)]
    h = [_rms(x, gpre_ref[...]).astype(BF16) for x in x1]
    down = [jnp.zeros_like(x) for x in x1]
    for c in range(D_FF // FF_CHUNK):
        for r in range(FFN_ROW_GROUPS):
            up = _dot(h[r], wup_ref[:, c * FF_CHUNK:(c + 1) * FF_CHUNK])
            act = jnp.square(jnp.maximum(up, 0.0)).astype(BF16)
            down[r] = down[r] + _dot(act, wdn_ref[c * FF_CHUNK:(c + 1) * FF_CHUNK, :])
    for r, rs in enumerate(groups):
        o_ref[rs, :] = x1[r] + _rms(down[r], gfpost_ref[...])


def _out_ffn(a, b, x, woa, wob, g_post, g_pre, w_up, w_down, g_fpost, *, tm):
    n = x.shape[0]
    row_spec = lambda w: pl.BlockSpec((tm, w), lambda i: (i, 0))
    consts = [woa, wob, g_post, g_pre, w_up, w_down, g_fpost]
    return pl.pallas_call(
        _out_ffn_body,
        grid=(n // tm,),
        in_specs=[row_spec(a.shape[1]), row_spec(b.shape[1]), row_spec(D_MODEL)] + [_const_spec(c.shape) for c in consts],
        out_specs=row_spec(D_MODEL),
        out_shape=jax.ShapeDtypeStruct((n, D_MODEL), F32),
        compiler_params=_cparams(("arbitrary",)), name="out_ffn")(a, b, x, *consts)


def _rope_tables(pos, reps):
    half = ROPE_DIM // 2
    inv = ROPE_THETA ** (-jnp.arange(half, dtype=F32) / half)
    ang = pos.astype(F32)[:, None] * inv[None, :]
    cos = jnp.tile(jnp.concatenate([jnp.cos(ang), jnp.cos(ang)], axis=1), (reps, LANES // ROPE_DIM))
    sin = jnp.tile(jnp.concatenate([-jnp.sin(ang), jnp.sin(ang)], axis=1), (reps, LANES // ROPE_DIM))
    return cos, sin


def _prep_even(p, j):
    w_in = p["even_w_in"][j]
    return dict(w_in=w_in.astype(BF16), w_kv_t=w_in[:, SB_WIDTH:3 * SB_WIDTH].T.astype(BF16), w_conv=p["even_w_conv"][j],
                woa=p["even_w_out"][j, :SB_WIDTH].astype(BF16), wob=p["even_w_out"][j, SB_WIDTH:].astype(BF16))


def _prep_odd(p, j, gate_len):
    w_in = p["odd_w_in"][j]
    w_in = jnp.pad(w_in, ((0, 0), (0, ODD_IN_PAD - w_in.shape[1]))).astype(BF16)
    reps = SGU_CHUNK // gate_len
    w_s = jnp.tile(p["sgu_w_s"][j, :, :gate_len, :gate_len], (1, reps, reps))
    b_s = jnp.tile(p["sgu_b_s"][j, :, :gate_len], (1, reps))
    b_s = jnp.repeat(b_s.T, SGU_GROUP_DIM, axis=1)
    w_uq = p["mla_w_uq"][j].reshape(Q_LORA, MLA_HEADS, NOPE_DIM + ROPE_DIM)
    wqn = w_uq[:, :, :NOPE_DIM].reshape(Q_LORA, MLA_HEADS * NOPE_DIM).astype(BF16)
    wqp = w_uq[:, :, NOPE_DIM:].reshape(Q_LORA, MLA_HEADS * ROPE_DIM).astype(BF16)
    wuk = p["mla_w_uk"][j].reshape(MLA_HEADS // 2, 2 * NOPE_DIM, KV_LORA).astype(BF16)
    w_uv = p["mla_w_uv"][j]
    wuv = jnp.stack([jnp.pad(w_uv[h], ((0, 0), ((h % 2) * V_DIM, (1 - h % 2) * V_DIM))) for h in range(MLA_HEADS)])
    return dict(w_in=w_in, ln_g=p["sgu_ln_g"][j][None], ln_b=p["sgu_ln_b"][j][None], w_s=w_s, b_s=b_s,
                qg=p["mla_q_norm_g"][j][None], kvg=p["mla_kv_norm_g"][j][None], wqn=wqn, wqp=wqp, wuk=wuk,
                wuv=wuv.astype(BF16),
                woa=p["odd_w_out"][j, :SGU_DIM].astype(BF16), wob=p["odd_w_out"][j, SGU_DIM:].astype(BF16))


def _run_trunk(x, pos, past, p, *, batch, seq_len):
    depth = p["mix_pre_g"].shape[0]
    n = batch * seq_len
    x = x.reshape(n, D_MODEL)
    is_sample = past is not None
    tm = min(ROW_TILE, n)
    assert n % tm == 0 and (tm % seq_len == 0 or seq_len % tm == 0) and tm % SGU_CHUNK == 0
    gate_len = min(seq_len, SGU_CHUNK)
    tm_proj = PROJ_ROW_TILE if (seq_len % PROJ_ROW_TILE == 0 and seq_len > PROJ_ROW_TILE) else tm
    cos, sin = _rope_tables(pos, max(1, tm_proj // seq_len))
    st = {k: [] for k in ("sb_k", "sb_v", "conv", "ckv", "kpe", "sgu_v")}
    for layer in range(depth):
        j = layer // 2
        g_pre = p["mix_pre_g"][layer][None]
        if layer % 2 == 0:
            w = _prep_even(p, j)
            conv_prev = past["conv"][j] if is_sample else None
            q, k, v, kb, vb, b_mix, conv_state = _proj_even(x, g_pre, w["w_in"], w["w_kv_t"], w["w_conv"], conv_prev,
                                                            batch=batch, seq_len=seq_len, tm=tm_proj)
            if is_sample:
                assert tm % seq_len == 0

                def cache_t(c):
                    return jnp.transpose(c, (0, 2, 3, 1)).reshape(batch, SB_WIDTH, -1)

                a_mix = _sb_sample(q, kb, vb, cache_t(past["sb_k"][j]), cache_t(past["sb_v"][j]),
                                   batch=batch, dec=seq_len)
                k, v = (t.reshape(batch, seq_len, SB_HEADS, SB_HEAD_DIM) for t in (k, v))
            else:
                assert seq_len % tm == 0
                a_mix = _sb_prompt(q, kb, vb, batch=batch, seq_len=seq_len, tq=SB_TQ, tk=SB_TK)
                k, v = (jnp.transpose(t.reshape(batch, SB_HEADS, SB_HEAD_DIM, seq_len), (0, 3, 1, 2)) for t in (k, v))
            st["sb_k"].append(k)
            st["sb_v"].append(v)
            st["conv"].append(conv_state)
        else:
            w = _prep_odd(p, j, gate_len)
            outs = _proj_odd(x, g_pre, w["w_in"], w["ln_g"], w["ln_b"], w["w_s"], w["b_s"], w["qg"], w["kvg"],
                             w["wqn"], w["wqp"], cos, sin, batch=batch, seq_len=seq_len, tm=tm_proj, gate_len=gate_len,
                             emit_vn=is_sample)
            if is_sample:
                a_mix, vn, qn, qp, ckv, ckvb, kpe, kpeb = outs
                st["sgu_v"].append(vn.reshape(batch, seq_len, SGU_DIM))
                b_mix = _mla_sample(qn, qp, ckvb, kpeb, past["ckv"][j], jnp.transpose(past["kpe"][j], (0, 2, 1)),
                                    w["wuk"], w["wuv"],
                                    batch=batch, dec=seq_len)
            else:
                a_mix, qn, qp, ckv, ckvb, kpe, kpeb = outs
                b_mix = _mla_prompt(qn, qp, ckvb, kpeb, w["wuk"], w["wuv"], batch=batch, seq_len=seq_len,
                                    tq=MLA_TQ, tk=MLA_TK)
            st["ckv"].append(ckv.reshape(batch, seq_len, KV_LORA))
            st["kpe"].append(jnp.transpose(kpe, (0, 2, 1)) if kpe.ndim == 3 else kpe.reshape(batch, seq_len, ROPE_DIM))
        x = _out_ffn(a_mix, b_mix, x, w["woa"], w["wob"], p["mix_post_g"][layer][None], p["ffn_pre_g"][layer][None],
                     p["ffn_w_up"][layer].astype(BF16), p["ffn_w_down"][layer].astype(BF16),
                     p["ffn_post_g"][layer][None], tm=tm)
    states = {k: jnp.stack(v) for k, v in st.items() if v}
    return x.reshape(batch, seq_len, D_MODEL), states


def kernel(x_prompt, x_sample, cache_sb_k, cache_sb_v, state_conv, cache_mla_ckv, cache_mla_kpe,
           mix_pre_g, mix_post_g, ffn_pre_g, ffn_post_g, even_w_in, even_w_conv, even_w_out,
           odd_w_in, sgu_ln_g, sgu_ln_b, sgu_w_s, sgu_b_s, mla_q_norm_g, mla_kv_norm_g,
           mla_w_uq, mla_w_uk, mla_w_uv, odd_w_out, ffn_w_up, ffn_w_down):
    params = {
        "mix_pre_g": mix_pre_g, "mix_post_g": mix_post_g, "ffn_pre_g": ffn_pre_g, "ffn_post_g": ffn_post_g,
        "even_w_in": even_w_in, "even_w_conv": even_w_conv, "even_w_out": even_w_out,
        "odd_w_in": odd_w_in, "sgu_ln_g": sgu_ln_g, "sgu_ln_b": sgu_ln_b, "sgu_w_s": sgu_w_s,
        "sgu_b_s": sgu_b_s, "mla_q_norm_g": mla_q_norm_g, "mla_kv_norm_g": mla_kv_norm_g,
        "mla_w_uq": mla_w_uq, "mla_w_uk": mla_w_uk, "mla_w_uv": mla_w_uv, "odd_w_out": odd_w_out,
        "ffn_w_up": ffn_w_up, "ffn_w_down": ffn_w_down,
    }
    batch, seq_len, _ = x_prompt.shape
    pos_p = jnp.arange(seq_len, dtype=jnp.int32)
    y_prompt, st_p = _run_trunk(x_prompt, pos_p, None, params, batch=batch, seq_len=seq_len)
    dec_batch, dec_seq, _ = x_sample.shape
    past_len = cache_sb_k.shape[2]
    pos_s = past_len + jnp.arange(dec_seq, dtype=jnp.int32)
    past = {"sb_k": cache_sb_k, "sb_v": cache_sb_v, "conv": state_conv, "ckv": cache_mla_ckv, "kpe": cache_mla_kpe}
    y_sample, st_s = _run_trunk(x_sample, pos_s, past, params, batch=dec_batch, seq_len=dec_seq)
    return (y_prompt, y_sample,
            st_p["sb_k"], st_p["sb_v"], st_p["conv"], st_p["ckv"], st_p["kpe"],
            st_s["sb_k"], st_s["sb_v"], st_s["conv"], st_s["ckv"], st_s["kpe"], st_s["sgu_v"])
```
